```python
import math
import jax
import jax.numpy as jnp
from jax import lax
import numpy as np

D_MODEL = 2048
BATCH = 4
SEQ = 4096
DEPTH = 4

GRID_W = 64
CTX_LEN = 256
Q_BLOCK = 128
ROPE_THETA = 10000.0
EPS = 1e-6

HEAD_DIM = 128
BRANCH_WIDTH = 1024
N_BRANCHES = 3
GQA_HEADS = 8
GQA_KV_HEADS = 2
GQA_GROUP = GQA_HEADS // GQA_KV_HEADS
MLA_HEADS = 8
MLA_NOPE = 128
MLA_ROPE = 64
MLA_V = 128
MLA_KV_RANK = 512
DIFF_HEADS = 8
DIFF_QK = 64
DIFF_V = 128

IN_SIZES = (
    GQA_HEADS * HEAD_DIM,
    GQA_KV_HEADS * HEAD_DIM,
    GQA_KV_HEADS * HEAD_DIM,
    MLA_HEADS * (MLA_NOPE + MLA_ROPE),
    MLA_KV_RANK,
    MLA_ROPE,
    DIFF_HEADS * 2 * DIFF_QK,
    DIFF_HEADS * 2 * DIFF_QK,
    DIFF_HEADS * DIFF_V,
    N_BRANCHES * BRANCH_WIDTH,
    N_BRANCHES * D_MODEL,
)
IN_COLS = sum(IN_SIZES)

kernel_name = 'hybrid_gqa_mla_diffattn_prefix_block'


def rms_norm(x, w):
    x32 = x.astype(jnp.float32)
    y = x32 * lax.rsqrt(jnp.mean(jnp.square(x32), axis=-1, keepdims=True) + EPS)
    return (y * w.astype(jnp.float32)).astype(x.dtype)


def axial_rope_tables(pos_row, pos_col, rot_dim):
    axis_dim = rot_dim // 2
    inv_freq = ROPE_THETA ** (-jnp.arange(0, axis_dim, 2, dtype=jnp.float32) / axis_dim)
    ang_r = pos_row[:, None] * inv_freq
    ang_c = pos_col[:, None] * inv_freq
    ang = jnp.concatenate([ang_r, ang_r, ang_c, ang_c], axis=-1)
    return jnp.cos(ang), jnp.sin(ang)


def rotate_half(x):
    x1, x2 = jnp.split(x, 2, axis=-1)
    return jnp.concatenate([-x2, x1], axis=-1)


def apply_axial_rope(x, tab):
    cos, sin = tab
    shape = (cos.shape[0],) + (1,) * (x.ndim - 3) + (cos.shape[1],)
    cos = cos.reshape(shape).astype(x.dtype)
    sin = sin.reshape(shape).astype(x.dtype)
    xr, xc = jnp.split(x, 2, axis=-1)
    rot = jnp.concatenate([rotate_half(xr), rotate_half(xc)], axis=-1)
    return x * cos + rot * sin


def attn_probs(q, k):
    s = jnp.einsum('bqhgd,bkhd->bhgqk', q.astype(jnp.float32), k.astype(jnp.float32))
    return jax.nn.softmax(s * (q.shape[-1] ** -0.5), axis=-1)


def attn_apply(p, v):
    return jnp.einsum('bhgqk,bkhd->bqhgd', p.astype(v.dtype), v)


def sweep_query_blocks(fn, qs):
    b, n = qs[0].shape[:2]
    nblk = n // Q_BLOCK
    to_blocks = lambda a: jnp.moveaxis(a.reshape((b, nblk, Q_BLOCK) + a.shape[2:]), 1, 0)
    from_blocks = lambda a: jnp.moveaxis(a, 0, 1).reshape((b, n) + a.shape[3:])
    out = lax.map(fn, tuple(to_blocks(q) for q in qs))
    return jax.tree_util.tree_map(from_blocks, out)


def stream_qkv(h, p, tabs):
    b, n, _ = h.shape
    splits = [int(i) for i in np.cumsum(IN_SIZES)[:-1]]
    (gq, gk, gv, mq, mckv, mkr, dq, dk, dv, gate_in, merge_in) = jnp.split(h @ p['w_in'], splits, axis=-1)
    gq = rms_norm(gq.reshape(b, n, GQA_KV_HEADS, GQA_GROUP, HEAD_DIM), p['gqa_q_norm'])
    gk = rms_norm(gk.reshape(b, n, GQA_KV_HEADS, HEAD_DIM), p['gqa_k_norm'])
    gv = gv.reshape(b, n, GQA_KV_HEADS, HEAD_DIM)
    mq = mq.reshape(b, n, MLA_HEADS, 1, MLA_NOPE + MLA_ROPE)
    mq_nope = rms_norm(mq[..., :MLA_NOPE], p['mla_q_nope_norm'])
    mq_rope = rms_norm(mq[..., MLA_NOPE:], p['mla_q_rope_norm'])
    c_kv = rms_norm(mckv, p['mla_kv_norm'])
    mk_nope = rms_norm((c_kv @ p['mla_w_uk']).reshape(b, n, MLA_HEADS, MLA_NOPE), p['mla_k_nope_norm'])
    mv = (c_kv @ p['mla_w_uv']).reshape(b, n, MLA_HEADS, MLA_V)
    mk_rope = rms_norm(mkr, p['mla_k_rope_norm'])[:, :, None, :]
    dq = rms_norm(dq.reshape(b, n, DIFF_HEADS, 2, DIFF_QK), p['diff_q_norm'])
    dk = rms_norm(dk.reshape(b, n, DIFF_HEADS, 2, DIFF_QK), p['diff_k_norm'])
    dv = dv.reshape(b, n, DIFF_HEADS, DIFF_V)
    if tabs is not None:
        gq = apply_axial_rope(gq, tabs[HEAD_DIM])
        gk = apply_axial_rope(gk, tabs[HEAD_DIM])
        mq_rope = apply_axial_rope(mq_rope, tabs[MLA_ROPE])
        mk_rope = apply_axial_rope(mk_rope, tabs[MLA_ROPE])
        dq = apply_axial_rope(dq, tabs[DIFF_QK])
        dk = apply_axial_rope(dk, tabs[DIFF_QK])
    mq = jnp.concatenate([mq_nope, mq_rope], axis=-1)
    mk = jnp.concatenate([mk_nope, jnp.broadcast_to(mk_rope, (b, n, MLA_HEADS, MLA_ROPE))], axis=-1)
    queries = (gq, mq, dq[:, :, :, 0:1, :], dq[:, :, :, 1:2, :])
    kv = (gk, gv, mk, mv, dk[:, :, :, 0, :], dk[:, :, :, 1, :], dv)
    return queries, kv, gate_in, merge_in


def attend_all(queries, kv, lam):
    gq, mq, dq1, dq2 = queries
    gk, gv, mk, mv, dk1, dk2, dv = kv
    o_a = attn_apply(attn_probs(gq, gk), gv)
    o_b = attn_apply(attn_probs(mq, mk), mv)
    o_c = attn_apply(attn_probs(dq1, dk1) - lam * attn_probs(dq2, dk2), dv)
    return o_a, o_b, o_c


def merge_branches(outs, gate_in, merge_in, p, lam_init):
    o_a, o_b, o_c = outs
    b, n = gate_in.shape[:2]
    o_c = rms_norm(o_c.reshape(b, n, DIFF_HEADS, DIFF_V), p['diff_subln']) * (1.0 - lam_init)
    branches = (o_a.reshape(b, n, BRANCH_WIDTH), o_b.reshape(b, n, BRANCH_WIDTH), o_c.reshape(b, n, BRANCH_WIDTH))
    g = jnp.split(jax.nn.silu(gate_in), N_BRANCHES, axis=-1)
    m = jnp.split(jax.nn.sigmoid(merge_in + p['b_merge']), N_BRANCHES, axis=-1)
    y = (m[0] * ((branches[0] * g[0]) @ p['w_br_gqa'])
         + m[1] * ((branches[1] * g[1]) @ p['w_br_mla'])
         + m[2] * ((branches[2] * g[2]) @ p['w_br_diff']))
    return y @ p['w_out']


def setup_inputs(seed: int = 0) -> dict:
    key = jax.random.key(seed)
    ks = jax.random.split(key, 32)
    counter = iter(range(32))

    def nrm(shape, scale):
        return jax.random.normal(ks[next(counter)], shape, jnp.float32) * scale

    def gain(dim):
        return 1.0 + nrm((DEPTH, dim), 0.02)

    L = DEPTH
    return {
        'x': nrm((BATCH, SEQ, D_MODEL), 1.0),
        'c': nrm((BATCH, D_MODEL), 1.0),
        'ctx': nrm((BATCH, CTX_LEN, D_MODEL), 1.0),
        'c_ctx': nrm((D_MODEL,), 1.0),
        'norm_w': gain(D_MODEL),
        'w_ada': nrm((L, D_MODEL, 3 * D_MODEL), 0.5 * D_MODEL ** -0.5),
        'b_ada': nrm((L, 3 * D_MODEL), 0.01),
        'w_in': nrm((L, D_MODEL, IN_COLS), D_MODEL ** -0.5),
        'b_merge': nrm((L, N_BRANCHES * D_MODEL), 0.01),
        'gqa_q_norm': gain(HEAD_DIM),
        'gqa_k_norm': gain(HEAD_DIM),
        'mla_q_nope_norm': gain(MLA_NOPE),
        'mla_q_rope_norm': gain(MLA_ROPE),
        'mla_kv_norm': gain(MLA_KV_RANK),
        'mla_w_uk': nrm((L, MLA_KV_RANK, MLA_HEADS * MLA_NOPE), MLA_KV_RANK ** -0.5),
        'mla_w_uv': nrm((L, MLA_KV_RANK, MLA_HEADS * MLA_V), MLA_KV_RANK ** -0.5),
        'mla_k_nope_norm': gain(MLA_NOPE),
        'mla_k_rope_norm': gain(MLA_ROPE),
        'diff_q_norm': gain(DIFF_QK),
        'diff_k_norm': gain(DIFF_QK),
        'diff_lambda_q1': nrm((L, DIFF_QK), 0.1),
        'diff_lambda_k1': nrm((L, DIFF_QK), 0.1),
        'diff_lambda_q2': nrm((L, DIFF_QK), 0.1),
        'diff_lambda_k2': nrm((L, DIFF_QK), 0.1),
        'diff_subln': gain(DIFF_V),
        'w_br_gqa': nrm((L, BRANCH_WIDTH, D_MODEL), BRANCH_WIDTH ** -0.5),
        'w_br_mla': nrm((L, BRANCH_WIDTH, D_MODEL), BRANCH_WIDTH ** -0.5),
        'w_br_diff': nrm((L, BRANCH_WIDTH, D_MODEL), BRANCH_WIDTH ** -0.5),
        'w_out': nrm((L, D_MODEL, D_MODEL), D_MODEL ** -0.5),
    }


def reference(x, c, ctx, c_ctx, norm_w, w_ada, b_ada, w_in, b_merge,
              gqa_q_norm, gqa_k_norm,
              mla_q_nope_norm, mla_q_rope_norm, mla_kv_norm, mla_w_uk, mla_w_uv,
              mla_k_nope_norm, mla_k_rope_norm,
              diff_q_norm, diff_k_norm, diff_lambda_q1, diff_lambda_k1,
              diff_lambda_q2, diff_lambda_k2, diff_subln,
              w_br_gqa, w_br_mla, w_br_diff, w_out):
    n = x.shape[1]
    rows = n // GRID_W
    pos_row = jnp.repeat(jnp.arange(rows, dtype=jnp.float32), GRID_W)
    pos_col = jnp.tile(jnp.arange(GRID_W, dtype=jnp.float32), rows)
    tabs = {d: axial_rope_tables(pos_row, pos_col, d) for d in (HEAD_DIM, MLA_ROPE, DIFF_QK)}

    silu_c = jax.nn.silu(c)
    silu_cc = jax.nn.silu(c_ctx)
    for l in range(DEPTH):
        p = dict(
            w_in=w_in[l], b_merge=b_merge[l],
            gqa_q_norm=gqa_q_norm[l], gqa_k_norm=gqa_k_norm[l],
            mla_q_nope_norm=mla_q_nope_norm[l], mla_q_rope_norm=mla_q_rope_norm[l],
            mla_kv_norm=mla_kv_norm[l], mla_w_uk=mla_w_uk[l], mla_w_uv=mla_w_uv[l],
            mla_k_nope_norm=mla_k_nope_norm[l], mla_k_rope_norm=mla_k_rope_norm[l],
            diff_q_norm=diff_q_norm[l], diff_k_norm=diff_k_norm[l], diff_subln=diff_subln[l],
            w_br_gqa=w_br_gqa[l], w_br_mla=w_br_mla[l], w_br_diff=w_br_diff[l], w_out=w_out[l],
        )
        shift, scale, gate = jnp.split(silu_c @ w_ada[l] + b_ada[l], 3, axis=-1)
        shift_c, scale_c, gate_c = jnp.split(silu_cc @ w_ada[l] + b_ada[l], 3, axis=-1)
        h_lat = rms_norm(x, norm_w[l]) * (1.0 + scale[:, None, :]) + shift[:, None, :]
        h_ctx = rms_norm(ctx, norm_w[l]) * (1.0 + scale_c) + shift_c

        q_lat, kv_lat, gate_lat, merge_lat = stream_qkv(h_lat, p, tabs)
        q_ctx, kv_ctx, gate_ctx, merge_ctx = stream_qkv(h_ctx, p, None)

        lam_init = 0.8 - 0.6 * math.exp(-0.3 * l)
        lam = (jnp.exp(jnp.sum(diff_lambda_q1[l].astype(jnp.float32) * diff_lambda_k1[l].astype(jnp.float32)))
               - jnp.exp(jnp.sum(diff_lambda_q2[l].astype(jnp.float32) * diff_lambda_k2[l].astype(jnp.float32)))
               + lam_init)

        kv_all = tuple(jnp.concatenate([kc, kl], axis=1) for kc, kl in zip(kv_ctx, kv_lat))
        o_lat = sweep_query_blocks(lambda qb: attend_all(qb, kv_all, lam), q_lat)
        out_lat = merge_branches(o_lat, gate_lat, merge_lat, p, lam_init)
        if l < DEPTH - 1:
            o_ctx = attend_all(q_ctx, kv_ctx, lam)
            out_ctx = merge_branches(o_ctx, gate_ctx, merge_ctx, p, lam_init)
            ctx = ctx + gate_c * out_ctx
        x = x + gate[:, None, :] * out_lat
    return x
```

```python
import functools
import math

import jax
import jax.numpy as jnp
from jax import lax
from jax.experimental import pallas as pl
from jax.experimental.pallas import tpu as pltpu

F32 = jnp.float32
BF16 = jnp.bfloat16

GRID_W = 64
ROPE_THETA = 10000.0
EPS = 1e-6

HEAD_DIM = 128
BRANCH_WIDTH = 1024
GQA_HEADS = 8
GQA_KV_HEADS = 2
GQA_GROUP = GQA_HEADS // GQA_KV_HEADS
MLA_HEADS = 8
MLA_NOPE = 128
MLA_ROPE = 64
MLA_QK_PAD = 256
MLA_KV_RANK = 512
DIFF_HEADS = 8
DIFF_QK = 64

LANES = 128
MXU_N = 256
VMEM_LIMIT = 56 * 1024 * 1024


def _params(sem, vmem=VMEM_LIMIT):
    return pltpu.CompilerParams(dimension_semantics=sem, vmem_limit_bytes=vmem)


def _resident(shape, index_map):
    return pl.BlockSpec(shape, index_map, pipeline_mode=pl.Buffered(1))


def _rms(y, gain, n):
    ms = jnp.sum(y * y, axis=-1, keepdims=True) * (1.0 / n)
    return y * lax.rsqrt(ms + EPS) * gain


def _seg_rms(y, gain, lo):
    ss = y * y
    s_lo = jnp.sum(jnp.where(lo, ss, 0.0), axis=-1, keepdims=True)
    s_hi = jnp.sum(jnp.where(lo, 0.0, ss), axis=-1, keepdims=True)
    ms = jnp.where(lo, s_lo, s_hi) * (1.0 / DIFF_QK)
    return y * lax.rsqrt(ms + EPS) * gain


def _rope(y, cos, sin_signed, half):
    lane = lax.broadcasted_iota(jnp.int32, (1, LANES), 1)
    lower = (lane & (2 * half - 1)) < half
    partner = jnp.where(lower, pltpu.roll(y, LANES - half, 1), pltpu.roll(y, half, 1))
    return y * cos + partner * sin_signed


def _sigmoid(z):
    return 1.0 / (1.0 + jnp.exp(-z))


def _ada_kernel(c_ref, w_ref, b_ref, o_ref):
    c = c_ref[...]
    a = (c * _sigmoid(c)).astype(BF16)
    o_ref[0] = jnp.dot(a, w_ref[0].astype(BF16), preferred_element_type=F32) + b_ref[0]


def _ada_call(c_all, w_ada, b_ada):
    depth, d, n3 = w_ada.shape
    tn = 512
    return pl.pallas_call(
        _ada_kernel,
        grid=(depth, n3 // tn),
        in_specs=[pl.BlockSpec((8, d), lambda l, j: (0, 0)),
                  pl.BlockSpec((1, d, tn), lambda l, j: (l, 0, j)),
                  pl.BlockSpec((1, 1, tn), lambda l, j: (l, 0, j))],
        out_specs=pl.BlockSpec((1, 8, tn), lambda l, j: (l, 0, j)),
        out_shape=jax.ShapeDtypeStruct((depth, 8, n3), F32),
        compiler_params=_params(("parallel", "parallel")),
        name="ada_mod",
    )(c_all, w_ada, b_ada.reshape(depth, 1, n3))


def _h_kernel(x_ref, nw_ref, mod_ref, h_ref):
    x = x_ref[0]
    ms = jnp.mean(x * x, axis=-1, keepdims=True)
    y = x * lax.rsqrt(ms + EPS) * nw_ref[...]
    shift = mod_ref[0, 0:1, :]
    scale = mod_ref[0, 1:2, :]
    h_ref[0] = (y * (1.0 + scale) + shift).astype(BF16)


def _h_call(xc, norm_w, mod, n_lat):
    b, n, d = xc.shape
    tr = 256
    lat_tiles = n_lat // tr
    return pl.pallas_call(
        _h_kernel,
        grid=(b, n // tr),
        in_specs=[pl.BlockSpec((1, tr, d), lambda bi, t: (bi, t, 0)),
                  pl.BlockSpec((1, d), lambda bi, t: (0, 0)),
                  pl.BlockSpec((1, 3, d), lambda bi, t: (jnp.where(t >= lat_tiles, b, bi), 0, 0))],
        out_specs=pl.BlockSpec((1, tr, d), lambda bi, t: (bi, t, 0)),
        out_shape=jax.ShapeDtypeStruct((b, n, d), BF16),
        compiler_params=_params(("parallel", "parallel")),
        name="norm_mod",
    )(xc, norm_w, mod)


def _proj_a_kernel(h_ref, w_ref, gq_ref, gk_ref, cos_ref, sin_ref, q_ref, k_ref, v_ref):
    h = h_ref[0]
    cos = cos_ref[...]
    sin = sin_ref[...]
    scale = HEAD_DIM ** -0.5
    n_q = GQA_HEADS * HEAD_DIM // MXU_N
    for c in range(n_q + 2):
        y = jnp.dot(h, w_ref[:, c * MXU_N:(c + 1) * MXU_N], preferred_element_type=F32)
        for u in range(2):
            yu = y[:, u * LANES:(u + 1) * LANES]
            if c < n_q:
                z = _rope(_rms(yu, gq_ref[...], HEAD_DIM), cos, sin, HEAD_DIM // 4)
                q_ref[0, 2 * c + u] = (z * scale).astype(BF16)
            elif c == n_q:
                z = _rope(_rms(yu, gk_ref[...], HEAD_DIM), cos, sin, HEAD_DIM // 4)
                k_ref[0, u] = z.astype(BF16)
            else:
                v_ref[0, u] = yu.astype(BF16)


def _proj_b_kernel(h_ref, w_ref, gq_ref, gckv_ref, gkr_ref, cos_ref, sin_ref,
                   q_ref, ckv_ref, kr_ref):
    h = h_ref[0]
    cos = cos_ref[...]
    sin = sin_ref[...]
    scale = (MLA_NOPE + MLA_ROPE) ** -0.5
    for hd in range(MLA_HEADS):
        y = jnp.dot(h, w_ref[:, hd * MLA_QK_PAD:(hd + 1) * MLA_QK_PAD], preferred_element_type=F32)
        nope = _rms(y[:, :LANES], gq_ref[:, :LANES], MLA_NOPE)
        rope = _rope(_rms(y[:, LANES:], gq_ref[:, LANES:], MLA_ROPE), cos, sin, MLA_ROPE // 4)
        q_ref[0, hd, :, 0:LANES] = (nope * scale).astype(BF16)
        q_ref[0, hd, :, LANES:2 * LANES] = (rope * scale).astype(BF16)
    base = MLA_HEADS * MLA_QK_PAD
    y0 = jnp.dot(h, w_ref[:, base:base + MXU_N], preferred_element_type=F32)
    y1 = jnp.dot(h, w_ref[:, base + MXU_N:base + 2 * MXU_N], preferred_element_type=F32)
    ms = (jnp.sum(y0 * y0, axis=-1, keepdims=True)
          + jnp.sum(y1 * y1, axis=-1, keepdims=True)) * (1.0 / MLA_KV_RANK)
    inv = lax.rsqrt(ms + EPS)
    ckv_ref[0, :, 0:MXU_N] = (y0 * inv * gckv_ref[:, 0:MXU_N]).astype(BF16)
    ckv_ref[0, :, MXU_N:2 * MXU_N] = (y1 * inv * gckv_ref[:, MXU_N:2 * MXU_N]).astype(BF16)
    base += MLA_KV_RANK
    yk = jnp.dot(h, w_ref[:, base:base + LANES], preferred_element_type=F32)
    kr = _rope(_rms(yk, gkr_ref[...], MLA_ROPE), cos, sin, MLA_ROPE // 4)
    kr_ref[0] = kr.astype(BF16)


def _proj_c_kernel(h_ref, w_ref, gq_ref, gk_ref, cos_ref, sin_ref, q_ref, k_ref, v_ref):
    h = h_ref[0]
    cos = cos_ref[...]
    sin = sin_ref[...]
    scale = DIFF_QK ** -0.5
    lo = lax.broadcasted_iota(jnp.int32, (1, LANES), 1) < DIFF_QK
    per = DIFF_HEADS // 2
    for c in range(3 * per):
        y = jnp.dot(h, w_ref[:, c * MXU_N:(c + 1) * MXU_N], preferred_element_type=F32)
        for u in range(2):
            yu = y[:, u * LANES:(u + 1) * LANES]
            if c < per:
                z = _rope(_seg_rms(yu, gq_ref[...], lo), cos, sin, DIFF_QK // 4) * scale
                q_ref[0, 2 * c + u, 0] = jnp.where(lo, z, 0.0).astype(BF16)
                q_ref[0, 2 * c + u, 1] = jnp.where(lo, 0.0, z).astype(BF16)
            elif c < 2 * per:
                z = _rope(_seg_rms(yu, gk_ref[...], lo), cos, sin, DIFF_QK // 4)
                k_ref[0, 2 * (c - per) + u] = z.astype(BF16)
            else:
                v_ref[0, 2 * (c - 2 * per) + u] = yu.astype(BF16)


def _proj_tm(n):
    return n // 8


def _proj_a_call(h, w, gq, gk, cos, sin):
    b, n, d = h.shape
    tm = _proj_tm(n)
    row = lambda bi, i: (bi, i, 0)
    head = lambda bi, i: (bi, 0, i, 0)
    const = lambda bi, i: (0, 0)
    return pl.pallas_call(
        _proj_a_kernel,
        grid=(b, n // tm),
        in_specs=[pl.BlockSpec((1, tm, d), row),
                  _resident(w.shape, const),
                  pl.BlockSpec((1, LANES), const), pl.BlockSpec((1, LANES), const),
                  pl.BlockSpec((tm, LANES), lambda bi, i: (i, 0)),
                  pl.BlockSpec((tm, LANES), lambda bi, i: (i, 0))],
        out_specs=[pl.BlockSpec((1, GQA_HEADS, tm, HEAD_DIM), head),
                   pl.BlockSpec((1, GQA_KV_HEADS, tm, HEAD_DIM), head),
                   pl.BlockSpec((1, GQA_KV_HEADS, tm, HEAD_DIM), head)],
        out_shape=[jax.ShapeDtypeStruct((b, GQA_HEADS, n, HEAD_DIM), BF16),
                   jax.ShapeDtypeStruct((b, GQA_KV_HEADS, n, HEAD_DIM), BF16),
                   jax.ShapeDtypeStruct((b, GQA_KV_HEADS, n, HEAD_DIM), BF16)],
        compiler_params=_params(("parallel", "parallel")),
        name="proj_gqa",
    )(h, w, gq, gk, cos, sin)


def _proj_b_call(h, w, gq, gckv, gkr, cos, sin):
    b, n, d = h.shape
    tm = _proj_tm(n)
    row = lambda bi, i: (bi, i, 0)
    const = lambda bi, i: (0, 0)
    return pl.pallas_call(
        _proj_b_kernel,
        grid=(b, n // tm),
        in_specs=[pl.BlockSpec((1, tm, d), row),
                  _resident(w.shape, const),
                  pl.BlockSpec((1, MLA_QK_PAD), const),
                  pl.BlockSpec((1, MLA_KV_RANK), const),
                  pl.BlockSpec((1, LANES), const),
                  pl.BlockSpec((tm, LANES), lambda bi, i: (i, 0)),
                  pl.BlockSpec((tm, LANES), lambda bi, i: (i, 0))],
        out_specs=[pl.BlockSpec((1, MLA_HEADS, tm, MLA_QK_PAD), lambda bi, i: (bi, 0, i, 0)),
                   pl.BlockSpec((1, tm, MLA_KV_RANK), row),
                   pl.BlockSpec((1, tm, LANES), row)],
        out_shape=[jax.ShapeDtypeStruct((b, MLA_HEADS, n, MLA_QK_PAD), BF16),
                   jax.ShapeDtypeStruct((b, n, MLA_KV_RANK), BF16),
                   jax.ShapeDtypeStruct((b, n, LANES), BF16)],
        compiler_params=_params(("parallel", "parallel")),
        name="proj_mla",
    )(h, w, gq, gckv, gkr, cos, sin)


def _proj_c_call(h, w, gq, gk, cos, sin):
    b, n, d = h.shape
    tm = _proj_tm(n)
    row = lambda bi, i: (bi, i, 0)
    const = lambda bi, i: (0, 0)
    return pl.pallas_call(
        _proj_c_kernel,
        grid=(b, n // tm),
        in_specs=[pl.BlockSpec((1, tm, d), row),
                  _resident(w.shape, const),
                  pl.BlockSpec((1, LANES), const), pl.BlockSpec((1, LANES), const),
                  pl.BlockSpec((tm, LANES), lambda bi, i: (i, 0)),
                  pl.BlockSpec((tm, LANES), lambda bi, i: (i, 0))],
        out_specs=[pl.BlockSpec((1, DIFF_HEADS, 2, tm, LANES), lambda bi, i: (bi, 0, 0, i, 0)),
                   pl.BlockSpec((1, DIFF_HEADS, tm, LANES), lambda bi, i: (bi, 0, i, 0)),
                   pl.BlockSpec((1, DIFF_HEADS, tm, LANES), lambda bi, i: (bi, 0, i, 0))],
        out_shape=[jax.ShapeDtypeStruct((b, DIFF_HEADS, 2, n, LANES), BF16),
                   jax.ShapeDtypeStruct((b, DIFF_HEADS, n, LANES), BF16),
                   jax.ShapeDtypeStruct((b, DIFF_HEADS, n, LANES), BF16)],
        compiler_params=_params(("parallel", "parallel")),
        name="proj_diff",
    )(h, w, gq, gk, cos, sin)


def _mla_up_kernel(ckv_ref, kr_ref, w_ref, gk_ref, k_ref, v_ref):
    ckv = ckv_ref[0]
    kr = kr_ref[0]
    per = MLA_HEADS // 2
    for c in range(2 * per):
        y = jnp.dot(ckv, w_ref[:, c * MXU_N:(c + 1) * MXU_N], preferred_element_type=F32)
        for u in range(2):
            yu = y[:, u * LANES:(u + 1) * LANES]
            if c < per:
                k_ref[0, 2 * c + u, :, 0:LANES] = _rms(yu, gk_ref[...], MLA_NOPE).astype(BF16)
                k_ref[0, 2 * c + u, :, LANES:2 * LANES] = kr
            else:
                v_ref[0, 2 * (c - per) + u] = yu.astype(BF16)


def _mla_up_call(ckv, kr, w, gk):
    b, n, r = ckv.shape
    tm = _proj_tm(n)
    row = lambda bi, i: (bi, i, 0)
    const = lambda bi, i: (0, 0)
    return pl.pallas_call(
        _mla_up_kernel,
        grid=(b, n // tm),
        in_specs=[pl.BlockSpec((1, tm, r), row),
                  pl.BlockSpec((1, tm, LANES), row),
                  _resident(w.shape, const),
                  pl.BlockSpec((1, LANES), const)],
        out_specs=[pl.BlockSpec((1, MLA_HEADS, tm, MLA_QK_PAD), lambda bi, i: (bi, 0, i, 0)),
                   pl.BlockSpec((1, MLA_HEADS, tm, LANES), lambda bi, i: (bi, 0, i, 0))],
        out_shape=[jax.ShapeDtypeStruct((b, MLA_HEADS, n, MLA_QK_PAD), BF16),
                   jax.ShapeDtypeStruct((b, MLA_HEADS, n, LANES), BF16)],
        compiler_params=_params(("parallel", "parallel")),
        name="mla_up",
    )(ckv, kr, w, gk)


def _gate_kernel(h_ref, w_ref, b_ref, o_ref, *, n_silu_tiles):
    j = pl.program_id(2)
    y = jnp.dot(h_ref[0], w_ref[...], preferred_element_type=F32)
    t = _sigmoid(y + b_ref[...])
    o_ref[0] = jnp.where(j < n_silu_tiles, y * t, t).astype(BF16)


def _gate_call(h, w, bias):
    b, n, d = h.shape
    cols = w.shape[1]
    tm = n // 4
    tn = 1024
    return pl.pallas_call(
        functools.partial(_gate_kernel, n_silu_tiles=3 * BRANCH_WIDTH // tn),
        grid=(b, n // tm, cols // tn),
        in_specs=[pl.BlockSpec((1, tm, d), lambda bi, i, j: (bi, i, 0)),
                  pl.BlockSpec((d, tn), lambda bi, i, j: (0, j)),
                  pl.BlockSpec((1, tn), lambda bi, i, j: (0, j))],
        out_specs=pl.BlockSpec((1, tm, tn), lambda bi, i, j: (bi, i, j)),
        out_shape=jax.ShapeDtypeStruct((b, n, cols), BF16),
        compiler_params=_params(("parallel", "parallel", "parallel")),
        name="gate_proj",
    )(h, w, bias)


def _flash_kernel(*refs, group, tq, chunks, diff, lam_init):
    if diff:
        q_ref, k_ref, v_ref, lamv_ref, sub_ref, o_ref = refs
    else:
        q_ref, k_ref, v_ref, o_ref = refs
    dq = q_ref.shape[-1]
    q = q_ref[0, 0].reshape(group * tq, dq)
    m = l = acc = None
    for start, size in chunks:
        k = k_ref[0, 0, start:start + size, :]
        v = v_ref[0, 0, start:start + size, :]
        s = lax.dot_general(q, k, (((1,), (1,)), ((), ())), preferred_element_type=F32)
        m_cur = jnp.max(s, axis=-1, keepdims=True)
        if m is None:
            m = m_cur
            p = jnp.exp(s - m)
            l = jnp.sum(p, axis=-1, keepdims=True)
            acc = jnp.dot(p.astype(BF16), v, preferred_element_type=F32)
        else:
            m_new = jnp.maximum(m, m_cur)
            alpha = jnp.exp(m - m_new)
            p = jnp.exp(s - m_new)
            l = alpha * l + jnp.sum(p, axis=-1, keepdims=True)
            acc = alpha * acc + jnp.dot(p.astype(BF16), v, preferred_element_type=F32)
            m = m_new
    out = acc / l
    if diff:
        lv = lamv_ref[0]
        lam = (jnp.exp(jnp.sum(lv[0:1] * lv[1:2], axis=-1, keepdims=True))
               - jnp.exp(jnp.sum(lv[2:3] * lv[3:4], axis=-1, keepdims=True)) + lam_init)
        o = out[:tq] - lam * out[tq:]
        o_ref[0] = (_rms(o, sub_ref[...], LANES) * (1.0 - lam_init)).astype(BF16)
    else:
        for g in range(group):
            o_ref[0, :, g * LANES:(g + 1) * LANES] = out[g * tq:(g + 1) * tq].astype(BF16)


def _flash_call(q, k, v, o_prev, *, tq, q_tile0, n_q_tiles, kv_tile0, chunks, n_tokens,
                diff=False, lamv=None, sub=None, lam_init=0.0, name="flash"):
    b, heads, group, n, dq = q.shape
    kv_len = sum(size for _, size in chunks)
    out_w = LANES if diff else group * LANES
    in_specs = [pl.BlockSpec((1, 1, group, tq, dq), lambda bi, h, i: (bi, h, 0, q_tile0 + i, 0)),
                pl.BlockSpec((1, 1, kv_len, dq), lambda bi, h, i: (bi, h, kv_tile0, 0)),
                pl.BlockSpec((1, 1, kv_len, LANES), lambda bi, h, i: (bi, h, kv_tile0, 0))]
    args = [q, k, v]
    if diff:
        in_specs += [pl.BlockSpec((1, 4, DIFF_QK), lambda bi, h, i: (0, 0, 0)),
                     pl.BlockSpec((1, LANES), lambda bi, h, i: (0, 0))]
        args += [lamv, sub]
    aliases = {}
    if o_prev is not None:
        in_specs.append(pl.BlockSpec(memory_space=pl.ANY))
        args.append(o_prev)
        aliases = {len(args) - 1: 0}
    kern = functools.partial(_flash_kernel, group=group, tq=tq, chunks=chunks, diff=diff,
                             lam_init=lam_init)
    if o_prev is not None:
        kern = functools.partial(_drop_alias, kern)
    return pl.pallas_call(
        kern,
        grid=(b, heads, n_q_tiles),
        in_specs=in_specs,
        out_specs=pl.BlockSpec((1, tq, out_w), lambda bi, h, i: (bi, q_tile0 + i, h)),
        out_shape=jax.ShapeDtypeStruct((b, n_tokens, BRANCH_WIDTH), BF16),
        input_output_aliases=aliases,
        compiler_params=_params(("parallel", "parallel", "parallel")),
        name=name,
    )(*args)


def _drop_alias(kern, *refs):
    kern(*refs[:-2], refs[-1])


def _merge_kernel(oa_ref, ob_ref, oc_ref, g_ref, ma_ref, mb_ref, mc_ref, w_ref, y_ref, og_ref):
    @pl.when(pl.program_id(2) == 0)
    def _():
        for r, o_ref in enumerate((oa_ref, ob_ref, oc_ref)):
            cols = slice(r * BRANCH_WIDTH, (r + 1) * BRANCH_WIDTH)
            og_ref[:, cols] = o_ref[0] * g_ref[0, :, cols]

    acc = None
    for r, m_ref in enumerate((ma_ref, mb_ref, mc_ref)):
        z = jnp.dot(og_ref[:, r * BRANCH_WIDTH:(r + 1) * BRANCH_WIDTH], w_ref[r],
                    preferred_element_type=F32)
        t = m_ref[0].astype(F32) * z
        acc = t if acc is None else acc + t
    y_ref[0] = acc.astype(BF16)


def _merge_call(oa, ob, oc, gm, w_br, rows, tm):
    b = oa.shape[0]
    d = w_br.shape[-1]
    tn = 512
    gate_w = 3 * BRANCH_WIDTH
    o_spec = pl.BlockSpec((1, tm, BRANCH_WIDTH), lambda bi, i, j: (bi, i, 0))

    def m_spec(r):
        off = (gate_w + r * d) // tn
        return pl.BlockSpec((1, tm, tn), lambda bi, i, j: (bi, i, off + j))

    return pl.pallas_call(
        _merge_kernel,
        grid=(b, rows // tm, d // tn),
        in_specs=[o_spec, o_spec, o_spec,
                  pl.BlockSpec((1, tm, gate_w), lambda bi, i, j: (bi, i, 0)),
                  m_spec(0), m_spec(1), m_spec(2),
                  pl.BlockSpec((3, BRANCH_WIDTH, tn), lambda bi, i, j: (0, 0, j))],
        out_specs=pl.BlockSpec((1, tm, tn), lambda bi, i, j: (bi, i, j)),
        out_shape=jax.ShapeDtypeStruct((b, rows, d), BF16),
        scratch_shapes=[pltpu.VMEM((tm, gate_w), BF16)],
        compiler_params=_params(("parallel", "parallel", "arbitrary")),
        name="branch_merge",
    )(oa, ob, oc, gm, gm, gm, gm, w_br)


def _out_kernel(y_ref, w_ref, x_ref, modb_ref, modc_ref, o_ref, *, tm, n_lat):
    out = jnp.dot(y_ref[0], w_ref[...], preferred_element_type=F32)
    row = pl.program_id(1) * tm + lax.broadcasted_iota(jnp.int32, (tm, 1), 0)
    gate = jnp.where(row < n_lat, modb_ref[0, 2:3, :], modc_ref[0, 2:3, :])
    o_ref[0] = x_ref[0] + gate * out


def _out_call(y, w_out, xc, mod, rows, tm, n_lat):
    b, _, d = xc.shape
    tn = 512
    return pl.pallas_call(
        functools.partial(_out_kernel, tm=tm, n_lat=n_lat),
        grid=(b, rows // tm, d // tn),
        in_specs=[pl.BlockSpec((1, tm, d), lambda bi, i, j: (bi, i, 0)),
                  pl.BlockSpec((d, tn), lambda bi, i, j: (0, j)),
                  pl.BlockSpec((1, tm, tn), lambda bi, i, j: (bi, i, j)),
                  pl.BlockSpec((1, 3, tn), lambda bi, i, j: (bi, 0, j)),
                  pl.BlockSpec((1, 3, tn), lambda bi, i, j: (b, 0, j))],
        out_specs=pl.BlockSpec((1, tm, tn), lambda bi, i, j: (bi, i, j)),
        out_shape=jax.ShapeDtypeStruct((b, rows, d), F32),
        compiler_params=_params(("parallel", "parallel", "parallel")),
        name="out_proj",
    )(y, w_out, xc, mod, mod)


def _rope_tables(n_lat, n_ctx, rot_dim):
    axis_dim = rot_dim // 2
    t = jnp.arange(n_lat, dtype=jnp.int32)
    pos_row = (t // GRID_W).astype(F32)
    pos_col = (t % GRID_W).astype(F32)
    inv_freq = ROPE_THETA ** (-jnp.arange(0, axis_dim, 2, dtype=F32) / axis_dim)
    ang_r = pos_row[:, None] * inv_freq
    ang_c = pos_col[:, None] * inv_freq
    ang = jnp.concatenate([ang_r, ang_r, ang_c, ang_c], axis=-1)
    cos, sin = jnp.cos(ang), jnp.sin(ang)
    lane = jnp.arange(rot_dim)
    sign = jnp.where((lane % axis_dim) < axis_dim // 2, -1.0, 1.0).astype(F32)
    reps = LANES // rot_dim
    cos = jnp.tile(cos, (1, reps))
    sin = jnp.tile(sin * sign, (1, reps))
    cos = jnp.concatenate([cos, jnp.ones((n_ctx, LANES), F32)], axis=0)
    sin = jnp.concatenate([sin, jnp.zeros((n_ctx, LANES), F32)], axis=0)
    return cos, sin


def kernel(x, c, ctx, c_ctx, norm_w, w_ada, b_ada, w_in, b_merge, gqa_q_norm, gqa_k_norm, mla_q_nope_norm, mla_q_rope_norm, mla_kv_norm, mla_w_uk, mla_w_uv, mla_k_nope_norm, mla_k_rope_norm, diff_q_norm, diff_k_norm, diff_lambda_q1, diff_lambda_k1, diff_lambda_q2, diff_lambda_k2, diff_subln, w_br_gqa, w_br_mla, w_br_diff, w_out):
    b, n_lat, d = x.shape
    n_ctx = ctx.shape[1]
    n = n_lat + n_ctx
    depth = w_in.shape[0]
    assert n_lat % 512 == 0 and n_ctx == 256 and n % 8 == 0 and b < 8

    c_all = jnp.concatenate([c, c_ctx[None], jnp.zeros((8 - b - 1, d), F32)], axis=0)
    mod = _ada_call(c_all, w_ada, b_ada).reshape(depth, 8, 3, d)

    o_aq = GQA_HEADS * HEAD_DIM + 2 * GQA_KV_HEADS * HEAD_DIM
    o_mq = o_aq + MLA_HEADS * (MLA_NOPE + MLA_ROPE)
    o_ckv = o_mq + MLA_KV_RANK
    o_kr = o_ckv + MLA_ROPE
    o_dend = o_kr + 3 * DIFF_HEADS * 2 * DIFF_QK
    w_a = w_in[:, :, :o_aq].astype(BF16)
    mq = w_in[:, :, o_aq:o_mq].reshape(depth, d, MLA_HEADS, MLA_NOPE + MLA_ROPE)
    mq = jnp.pad(mq, ((0, 0), (0, 0), (0, 0), (0, MLA_QK_PAD - MLA_NOPE - MLA_ROPE)))
    w_b = jnp.concatenate(
        [mq.reshape(depth, d, MLA_HEADS * MLA_QK_PAD), w_in[:, :, o_mq:o_ckv],
         jnp.pad(w_in[:, :, o_ckv:o_kr], ((0, 0), (0, 0), (0, LANES - MLA_ROPE)))], axis=-1).astype(BF16)
    w_c = w_in[:, :, o_kr:o_dend].astype(BF16)
    w_g = w_in[:, :, o_dend:].astype(BF16)
    w_ukv = jnp.concatenate([mla_w_uk, mla_w_uv], axis=-1).astype(BF16)
    w_br = jnp.stack([w_br_gqa, w_br_mla, w_br_diff], axis=1).astype(BF16)
    w_o = w_out.astype(BF16)
    bias_g = jnp.concatenate([jnp.zeros((depth, 3 * BRANCH_WIDTH), F32), b_merge], axis=-1)
    g_bq = jnp.concatenate([mla_q_nope_norm, mla_q_rope_norm,
                            jnp.zeros((depth, MLA_QK_PAD - MLA_NOPE - MLA_ROPE), F32)], axis=-1)
    g_kr = jnp.pad(mla_k_rope_norm, ((0, 0), (0, LANES - MLA_ROPE)))
    g_dq = jnp.tile(diff_q_norm, (1, 2))
    g_dk = jnp.tile(diff_k_norm, (1, 2))
    lamv = jnp.stack([diff_lambda_q1, diff_lambda_k1, diff_lambda_q2, diff_lambda_k2], axis=1)

    cos128, sin128 = _rope_tables(n_lat, n_ctx, HEAD_DIM)
    cos64, sin64 = _rope_tables(n_lat, n_ctx, MLA_ROPE)

    xc = jnp.concatenate([x, ctx], axis=1)

    lat_chunks = tuple((s, 512) for s in range(0, n_lat, 512)) + ((n_lat, n_ctx),)
    ctx_chunks = ((0, n_ctx),)
    ctx_tile = n_lat // n_ctx

    for l in range(depth):
        last = l == depth - 1
        lam_init = 0.8 - 0.6 * math.exp(-0.3 * l)
        h = _h_call(xc, norm_w[l][None], mod[l], n_lat)

        qa, ka, va = _proj_a_call(h, w_a[l], gqa_q_norm[l][None], gqa_k_norm[l][None], cos128, sin128)
        qb, ckv, kr = _proj_b_call(h, w_b[l], g_bq[l][None], mla_kv_norm[l][None], g_kr[l][None],
                                   cos64, sin64)
        qc, kc, vc = _proj_c_call(h, w_c[l], g_dq[l][None], g_dk[l][None], cos64, sin64)
        gm = _gate_call(h, w_g[l], bias_g[l][None])
        kb, vb = _mla_up_call(ckv, kr, w_ukv[l], mla_k_nope_norm[l][None])

        qa = qa.reshape(b, GQA_KV_HEADS, GQA_GROUP, n, HEAD_DIM)
        qb = qb.reshape(b, MLA_HEADS, 1, n, MLA_QK_PAD)
        rows = n_lat if last else n
        diff_kw = dict(diff=True, lamv=lamv[l][None], sub=diff_subln[l][None], lam_init=lam_init)

        oa = _flash_call(qa, ka, va, None, tq=256, q_tile0=0, n_q_tiles=n_lat // 256, kv_tile0=0,
                         chunks=lat_chunks, n_tokens=rows, name="flash_gqa")
        ob = _flash_call(qb, kb, vb, None, tq=512, q_tile0=0, n_q_tiles=n_lat // 512, kv_tile0=0,
                         chunks=lat_chunks, n_tokens=rows, name="flash_mla")
        oc = _flash_call(qc, kc, vc, None, tq=512, q_tile0=0, n_q_tiles=n_lat // 512, kv_tile0=0,
                         chunks=lat_chunks, n_tokens=rows, name="flash_diff", **diff_kw)
        if not last:
            ctx_kw = dict(tq=n_ctx, q_tile0=ctx_tile, n_q_tiles=1, kv_tile0=ctx_tile,
                          chunks=ctx_chunks, n_tokens=rows)
            oa = _flash_call(qa, ka, va, oa, name="flash_gqa_ctx", **ctx_kw)
            ob = _flash_call(qb, kb, vb, ob, name="flash_mla_ctx", **ctx_kw)
            oc = _flash_call(qc, kc, vc, oc, name="flash_diff_ctx", **ctx_kw, **diff_kw)

        tm = rows // 8
        y = _merge_call(oa, ob, oc, gm, w_br[l], rows, tm)
        xc = _out_call(y, w_o[l], xc, mod[l], rows, tm, n_lat)
    return xc
```

```python
import functools
import math

import jax
import jax.numpy as jnp
from jax import lax
from jax.experimental import pallas as pl
from jax.experimental.pallas import tpu as pltpu

F32 = jnp.float32
BF16 = jnp.bfloat16

GRID_W = 64
ROPE_THETA = 10000.0
EPS = 1e-6

HEAD_DIM = 128
BRANCH_WIDTH = 1024
GQA_HEADS = 8
GQA_KV_HEADS = 2
GQA_GROUP = GQA_HEADS // GQA_KV_HEADS
MLA_HEADS = 8
MLA_NOPE = 128
MLA_ROPE = 64
MLA_QK_PAD = 256
MLA_KV_RANK = 512
DIFF_HEADS = 8
DIFF_QK = 64

LANES = 128
MXU_N = 256
VMEM_LIMIT = 56 * 1024 * 1024
LOG2E = 1.4426950408889634
SAFE_BOUND_LOG2 = 57.0


def _params(sem, vmem=VMEM_LIMIT):
    return pltpu.CompilerParams(dimension_semantics=sem, vmem_limit_bytes=vmem)


def _resident(shape, index_map):
    return pl.BlockSpec(shape, index_map, pipeline_mode=pl.Buffered(1))


def _rms(y, gain, n):
    ms = jnp.sum(y * y, axis=-1, keepdims=True) * (1.0 / n)
    return y * lax.rsqrt(ms + EPS) * gain


def _seg_rms(y, gain, lo):
    ss = y * y
    s_lo = jnp.sum(jnp.where(lo, ss, 0.0), axis=-1, keepdims=True)
    s_hi = jnp.sum(jnp.where(lo, 0.0, ss), axis=-1, keepdims=True)
    ms = jnp.where(lo, s_lo, s_hi) * (1.0 / DIFF_QK)
    return y * lax.rsqrt(ms + EPS) * gain


def _rope(y, cos, sin_signed, half):
    lane = lax.broadcasted_iota(jnp.int32, (1, LANES), 1)
    lower = (lane & (2 * half - 1)) < half
    partner = jnp.where(lower, pltpu.roll(y, LANES - half, 1), pltpu.roll(y, half, 1))
    return y * cos + partner * sin_signed


def _sigmoid(z):
    return 1.0 / (1.0 + jnp.exp(-z))


def _ada_kernel(c_ref, w_ref, b_ref, o_ref):
    c = c_ref[...]
    a = (c * _sigmoid(c)).astype(BF16)
    o_ref[0] = jnp.dot(a, w_ref[0].astype(BF16), preferred_element_type=F32) + b_ref[0]


def _ada_call(c_all, w_ada, b_ada):
    depth, d, n3 = w_ada.shape
    tn = 512
    return pl.pallas_call(
        _ada_kernel,
        grid=(depth, n3 // tn),
        in_specs=[pl.BlockSpec((8, d), lambda l, j: (0, 0)),
                  pl.BlockSpec((1, d, tn), lambda l, j: (l, 0, j)),
                  pl.BlockSpec((1, 1, tn), lambda l, j: (l, 0, j))],
        out_specs=pl.BlockSpec((1, 8, tn), lambda l, j: (l, 0, j)),
        out_shape=jax.ShapeDtypeStruct((depth, 8, n3), F32),
        compiler_params=_params(("parallel", "parallel")),
        name="ada_mod",
    )(c_all, w_ada, b_ada.reshape(depth, 1, n3))


def _h_kernel(x_ref, nw_ref, mod_ref, h_ref):
    x = x_ref[0]
    ms = jnp.mean(x * x, axis=-1, keepdims=True)
    y = x * lax.rsqrt(ms + EPS) * nw_ref[...]
    shift = mod_ref[0, 0:1, :]
    scale = mod_ref[0, 1:2, :]
    h_ref[0] = (y * (1.0 + scale) + shift).astype(BF16)


def _h_call(xs, norm_w, mod, h_prev, *, n, tile0, ctx_stream, name):
    b, rows, d = xs.shape
    tr = 256
    in_specs = [pl.BlockSpec((1, tr, d), lambda bi, t: (bi, t, 0)),
                pl.BlockSpec((1, d), lambda bi, t: (0, 0)),
                pl.BlockSpec((1, 3, d), lambda bi, t: (b if ctx_stream else bi, 0, 0))]
    args = [xs, norm_w, mod]
    kern, aliases = _h_kernel, {}
    if h_prev is not None:
        in_specs.append(pl.BlockSpec(memory_space=pl.ANY))
        args.append(h_prev)
        kern, aliases = functools.partial(_drop_ref, _h_kernel, 3), {3: 0}
    return pl.pallas_call(
        kern,
        grid=(b, rows // tr),
        in_specs=in_specs,
        out_specs=pl.BlockSpec((1, tr, d), lambda bi, t: (bi, tile0 + t, 0)),
        out_shape=jax.ShapeDtypeStruct((b, n, d), BF16),
        input_output_aliases=aliases,
        compiler_params=_params(("parallel", "parallel")),
        name=name,
    )(*args)


def _proj_a_kernel(h_ref, w_ref, gq_ref, gk_ref, cos_ref, sin_ref, q_ref, k_ref, v_ref):
    h = h_ref[0]
    cos = cos_ref[...]
    sin = sin_ref[...]
    scale = HEAD_DIM ** -0.5 * LOG2E
    n_q = GQA_HEADS * HEAD_DIM // MXU_N
    for c in range(n_q + 2):
        y = jnp.dot(h, w_ref[:, c * MXU_N:(c + 1) * MXU_N], preferred_element_type=F32)
        for u in range(2):
            yu = y[:, u * LANES:(u + 1) * LANES]
            if c < n_q:
                z = _rope(_rms(yu, gq_ref[...], HEAD_DIM), cos, sin, HEAD_DIM // 4)
                q_ref[0, 2 * c + u] = (z * scale).astype(BF16)
            elif c == n_q:
                z = _rope(_rms(yu, gk_ref[...], HEAD_DIM), cos, sin, HEAD_DIM // 4)
                k_ref[0, u] = z.astype(BF16)
            else:
                v_ref[0, u] = yu.astype(BF16)


def _proj_b_kernel(h_ref, w_ref, gq_ref, gckv_ref, gkr_ref, cos_ref, sin_ref,
                   q_ref, ckv_ref, kr_ref):
    h = h_ref[0]
    cos = cos_ref[...]
    sin = sin_ref[...]
    scale = (MLA_NOPE + MLA_ROPE) ** -0.5 * LOG2E
    for hd in range(MLA_HEADS):
        y = jnp.dot(h, w_ref[:, hd * MLA_QK_PAD:(hd + 1) * MLA_QK_PAD], preferred_element_type=F32)
        nope = _rms(y[:, :LANES], gq_ref[:, :LANES], MLA_NOPE)
        rope = _rope(_rms(y[:, LANES:], gq_ref[:, LANES:], MLA_ROPE), cos, sin, MLA_ROPE // 4)
        q_ref[0, hd, :, 0:LANES] = (nope * scale).astype(BF16)
        q_ref[0, hd, :, LANES:2 * LANES] = (rope * scale).astype(BF16)
    base = MLA_HEADS * MLA_QK_PAD
    y0 = jnp.dot(h, w_ref[:, base:base + MXU_N], preferred_element_type=F32)
    y1 = jnp.dot(h, w_ref[:, base + MXU_N:base + 2 * MXU_N], preferred_element_type=F32)
    ms = (jnp.sum(y0 * y0, axis=-1, keepdims=True)
          + jnp.sum(y1 * y1, axis=-1, keepdims=True)) * (1.0 / MLA_KV_RANK)
    inv = lax.rsqrt(ms + EPS)
    ckv_ref[0, :, 0:MXU_N] = (y0 * inv * gckv_ref[:, 0:MXU_N]).astype(BF16)
    ckv_ref[0, :, MXU_N:2 * MXU_N] = (y1 * inv * gckv_ref[:, MXU_N:2 * MXU_N]).astype(BF16)
    base += MLA_KV_RANK
    yk = jnp.dot(h, w_ref[:, base:base + LANES], preferred_element_type=F32)
    kr = _rope(_rms(yk, gkr_ref[...], MLA_ROPE), cos, sin, MLA_ROPE // 4)
    kr_ref[0] = kr.astype(BF16)


def _proj_c_kernel(h_ref, w_ref, gq_ref, gk_ref, cos_ref, sin_ref, q_ref, k_ref, v_ref):
    h = h_ref[0]
    cos = cos_ref[...]
    sin = sin_ref[...]
    scale = DIFF_QK ** -0.5 * LOG2E
    lo = lax.broadcasted_iota(jnp.int32, (1, LANES), 1) < DIFF_QK
    per = DIFF_HEADS // 2
    for c in range(3 * per):
        y = jnp.dot(h, w_ref[:, c * MXU_N:(c + 1) * MXU_N], preferred_element_type=F32)
        for u in range(2):
            yu = y[:, u * LANES:(u + 1) * LANES]
            if c < per:
                z = _rope(_seg_rms(yu, gq_ref[...], lo), cos, sin, DIFF_QK // 4) * scale
                q_ref[0, 2 * c + u, 0] = jnp.where(lo, z, 0.0).astype(BF16)
                q_ref[0, 2 * c + u, 1] = jnp.where(lo, 0.0, z).astype(BF16)
            elif c < 2 * per:
                z = _rope(_seg_rms(yu, gk_ref[...], lo), cos, sin, DIFF_QK // 4)
                k_ref[0, 2 * (c - per) + u] = z.astype(BF16)
            else:
                v_ref[0, 2 * (c - 2 * per) + u] = yu.astype(BF16)


def _proj_tm(n):
    return n // 8


def _proj_a_call(h, w, gq, gk, cos, sin):
    b, n, d = h.shape
    tm = _proj_tm(n)
    row = lambda bi, i: (bi, i, 0)
    head = lambda bi, i: (bi, 0, i, 0)
    const = lambda bi, i: (0, 0)
    return pl.pallas_call(
        _proj_a_kernel,
        grid=(b, n // tm),
        in_specs=[pl.BlockSpec((1, tm, d), row),
                  _resident(w.shape, const),
                  pl.BlockSpec((1, LANES), const), pl.BlockSpec((1, LANES), const),
                  pl.BlockSpec((tm, LANES), lambda bi, i: (i, 0)),
                  pl.BlockSpec((tm, LANES), lambda bi, i: (i, 0))],
        out_specs=[pl.BlockSpec((1, GQA_HEADS, tm, HEAD_DIM), head),
                   pl.BlockSpec((1, GQA_KV_HEADS, tm, HEAD_DIM), head),
                   pl.BlockSpec((1, GQA_KV_HEADS, tm, HEAD_DIM), head)],
        out_shape=[jax.ShapeDtypeStruct((b, GQA_HEADS, n, HEAD_DIM), BF16),
                   jax.ShapeDtypeStruct((b, GQA_KV_HEADS, n, HEAD_DIM), BF16),
                   jax.ShapeDtypeStruct((b, GQA_KV_HEADS, n, HEAD_DIM), BF16)],
        compiler_params=_params(("parallel", "parallel")),
        name="proj_gqa",
    )(h, w, gq, gk, cos, sin)


def _proj_b_call(h, w, gq, gckv, gkr, cos, sin):
    b, n, d = h.shape
    tm = _proj_tm(n)
    row = lambda bi, i: (bi, i, 0)
    const = lambda bi, i: (0, 0)
    return pl.pallas_call(
        _proj_b_kernel,
        grid=(b, n // tm),
        in_specs=[pl.BlockSpec((1, tm, d), row),
                  _resident(w.shape, const),
                  pl.BlockSpec((1, MLA_QK_PAD), const),
                  pl.BlockSpec((1, MLA_KV_RANK), const),
                  pl.BlockSpec((1, LANES), const),
                  pl.BlockSpec((tm, LANES), lambda bi, i: (i, 0)),
                  pl.BlockSpec((tm, LANES), lambda bi, i: (i, 0))],
        out_specs=[pl.BlockSpec((1, MLA_HEADS, tm, MLA_QK_PAD), lambda bi, i: (bi, 0, i, 0)),
                   pl.BlockSpec((1, tm, MLA_KV_RANK), row),
                   pl.BlockSpec((1, tm, LANES), row)],
        out_shape=[jax.ShapeDtypeStruct((b, MLA_HEADS, n, MLA_QK_PAD), BF16),
                   jax.ShapeDtypeStruct((b, n, MLA_KV_RANK), BF16),
                   jax.ShapeDtypeStruct((b, n, LANES), BF16)],
        compiler_params=_params(("parallel", "parallel")),
        name="proj_mla",
    )(h, w, gq, gckv, gkr, cos, sin)


def _proj_c_call(h, w, gq, gk, cos, sin):
    b, n, d = h.shape
    tm = _proj_tm(n)
    row = lambda bi, i: (bi, i, 0)
    const = lambda bi, i: (0, 0)
    return pl.pallas_call(
        _proj_c_kernel,
        grid=(b, n // tm),
        in_specs=[pl.BlockSpec((1, tm, d), row),
                  _resident(w.shape, const),
                  pl.BlockSpec((1, LANES), const), pl.BlockSpec((1, LANES), const),
                  pl.BlockSpec((tm, LANES), lambda bi, i: (i, 0)),
                  pl.BlockSpec((tm, LANES), lambda bi, i: (i, 0))],
        out_specs=[pl.BlockSpec((1, DIFF_HEADS, 2, tm, LANES), lambda bi, i: (bi, 0, 0, i, 0)),
                   pl.BlockSpec((1, DIFF_HEADS, tm, LANES), lambda bi, i: (bi, 0, i, 0)),
                   pl.BlockSpec((1, DIFF_HEADS, tm, LANES), lambda bi, i: (bi, 0, i, 0))],
        out_shape=[jax.ShapeDtypeStruct((b, DIFF_HEADS, 2, n, LANES), BF16),
                   jax.ShapeDtypeStruct((b, DIFF_HEADS, n, LANES), BF16),
                   jax.ShapeDtypeStruct((b, DIFF_HEADS, n, LANES), BF16)],
        compiler_params=_params(("parallel", "parallel")),
        name="proj_diff",
    )(h, w, gq, gk, cos, sin)


def _mla_up_kernel(ckv_ref, kr_ref, w_ref, gk_ref, k_ref, v_ref):
    ckv = ckv_ref[0]
    kr = kr_ref[0]
    per = MLA_HEADS // 2
    for c in range(2 * per):
        y = jnp.dot(ckv, w_ref[:, c * MXU_N:(c + 1) * MXU_N], preferred_element_type=F32)
        for u in range(2):
            yu = y[:, u * LANES:(u + 1) * LANES]
            if c < per:
                k_ref[0, 2 * c + u, :, 0:LANES] = _rms(yu, gk_ref[...], MLA_NOPE).astype(BF16)
                k_ref[0, 2 * c + u, :, LANES:2 * LANES] = kr
            else:
                v_ref[0, 2 * (c - per) + u] = yu.astype(BF16)


def _mla_up_call(ckv, kr, w, gk):
    b, n, r = ckv.shape
    tm = _proj_tm(n)
    row = lambda bi, i: (bi, i, 0)
    const = lambda bi, i: (0, 0)
    return pl.pallas_call(
        _mla_up_kernel,
        grid=(b, n // tm),
        in_specs=[pl.BlockSpec((1, tm, r), row),
                  pl.BlockSpec((1, tm, LANES), row),
                  _resident(w.shape, const),
                  pl.BlockSpec((1, LANES), const)],
        out_specs=[pl.BlockSpec((1, MLA_HEADS, tm, MLA_QK_PAD), lambda bi, i: (bi, 0, i, 0)),
                   pl.BlockSpec((1, MLA_HEADS, tm, LANES), lambda bi, i: (bi, 0, i, 0))],
        out_shape=[jax.ShapeDtypeStruct((b, MLA_HEADS, n, MLA_QK_PAD), BF16),
                   jax.ShapeDtypeStruct((b, MLA_HEADS, n, LANES), BF16)],
        compiler_params=_params(("parallel", "parallel")),
        name="mla_up",
    )(ckv, kr, w, gk)


def _gate_kernel(h_ref, w_ref, b_ref, o_ref, *, n_sigmoid_tiles):
    j = pl.program_id(2)
    y = jnp.dot(h_ref[0], w_ref[...], preferred_element_type=F32)
    t = _sigmoid(y + b_ref[...])
    o_ref[0] = jnp.where(j < n_sigmoid_tiles, t, y * t).astype(BF16)


def _gate_call(h, w, bias):
    b, n, d = h.shape
    cols = w.shape[1]
    tm = n // 4
    tn = 1024
    return pl.pallas_call(
        functools.partial(_gate_kernel, n_sigmoid_tiles=(cols - 3 * BRANCH_WIDTH) // tn),
        grid=(b, n // tm, cols // tn),
        in_specs=[pl.BlockSpec((1, tm, d), lambda bi, i, j: (bi, i, 0)),
                  pl.BlockSpec((d, tn), lambda bi, i, j: (0, j)),
                  pl.BlockSpec((1, tn), lambda bi, i, j: (0, j))],
        out_specs=pl.BlockSpec((1, tm, tn), lambda bi, i, j: (bi, i, j)),
        out_shape=jax.ShapeDtypeStruct((b, n, cols), BF16),
        compiler_params=_params(("parallel", "parallel", "parallel")),
        name="gate_proj",
    )(h, w, bias)


def _flash_kernel(*refs, group, tq, chunks, diff, lam_init):
    if diff:
        q_ref, k_ref, v_ref, lamv_ref, sub_ref, o_ref, kmax_ref = refs
    else:
        q_ref, k_ref, v_ref, o_ref, kmax_ref = refs
    dq = q_ref.shape[-1]
    nt = (((1,), (1,)), ((), ()))

    @pl.when(pl.program_id(2) == 0)
    def _():
        best = None
        for start, size in chunks:
            kf = k_ref[0, 0, start:start + size, :].astype(F32)
            r = jnp.max(jnp.sum(kf * kf, axis=-1, keepdims=True), axis=0, keepdims=True)
            best = r if best is None else jnp.maximum(best, r)
        kmax_ref[...] = jnp.broadcast_to(best, kmax_ref.shape)

    q = q_ref[0, 0].reshape(group * tq, dq)
    half = group * tq // 2
    halves = (slice(0, half), slice(half, 2 * half))

    def q_dot_k(start, size):
        k = k_ref[0, 0, start:start + size, :]
        return [lax.dot_general(q[rows], k, nt, preferred_element_type=F32) for rows in halves]

    first_scores = q_dot_k(*chunks[0])

    qf = q.astype(F32)
    sq = jnp.sum(qf * qf, axis=-1, keepdims=True) * kmax_ref[0:1, 0:1]
    bound = sq * lax.rsqrt(jnp.maximum(sq, 1e-30))
    safe = jnp.max(sq) <= SAFE_BOUND_LOG2 * SAFE_BOUND_LOG2

    def finish(out):
        if diff:
            lv = lamv_ref[0]
            lam = (jnp.exp(jnp.sum(lv[0:1] * lv[1:2], axis=-1, keepdims=True))
                   - jnp.exp(jnp.sum(lv[2:3] * lv[3:4], axis=-1, keepdims=True)) + lam_init)
            o = out[:tq] - lam * out[tq:]
            o_ref[0] = (_rms(o, sub_ref[...], LANES) * (1.0 - lam_init)).astype(BF16)
        else:
            for g in range(group):
                o_ref[0, :, g * LANES:(g + 1) * LANES] = out[g * tq:(g + 1) * tq].astype(BF16)

    @pl.when(safe)
    def _():
        accs = [None, None]

        def p_dot_v(ps, start, size):
            v = v_ref[0, 0, start:start + size, :]
            v_ext = jnp.concatenate([v, jnp.ones_like(v)], axis=1)
            for i, p in enumerate(ps):
                t = jnp.dot(p, v_ext, preferred_element_type=F32)
                accs[i] = t if accs[i] is None else accs[i] + t

        pending = None
        for start, size in chunks:
            ss = first_scores if pending is None else q_dot_k(start, size)
            if pending is not None:
                p_dot_v(*pending)
            pending = ([jnp.exp2(s - bound[rows]).astype(BF16) for s, rows in zip(ss, halves)],
                       start, size)
        p_dot_v(*pending)
        acc = jnp.concatenate(accs, axis=0)
        finish(acc[:, :LANES] / acc[:, LANES:])

    @pl.when(jnp.logical_not(safe))
    def _():
        m = l = acc = None
        for start, size in chunks:
            k = k_ref[0, 0, start:start + size, :]
            v = v_ref[0, 0, start:start + size, :]
            s = lax.dot_general(q, k, nt, preferred_element_type=F32)
            m_cur = jnp.max(s, axis=-1, keepdims=True)
            if m is None:
                m = m_cur
                p = jnp.exp2(s - m)
                l = jnp.sum(p, axis=-1, keepdims=True)
                acc = jnp.dot(p.astype(BF16), v, preferred_element_type=F32)
            else:
                m_new = jnp.maximum(m, m_cur)
                alpha = jnp.exp2(m - m_new)
                p = jnp.exp2(s - m_new)
                l = alpha * l + jnp.sum(p, axis=-1, keepdims=True)
                acc = alpha * acc + jnp.dot(p.astype(BF16), v, preferred_element_type=F32)
                m = m_new
        finish(acc / l)


def _flash_call(q, k, v, o_prev, *, tq, q_tile0, n_q_tiles, kv_tile0, chunks, n_tokens,
                diff=False, lamv=None, sub=None, lam_init=0.0, name="flash"):
    b, heads, group, n, dq = q.shape
    kv_len = sum(size for _, size in chunks)
    out_w = LANES if diff else group * LANES
    in_specs = [pl.BlockSpec((1, 1, group, tq, dq), lambda bi, h, i: (bi, h, 0, q_tile0 + i, 0)),
                pl.BlockSpec((1, 1, kv_len, dq), lambda bi, h, i: (bi, h, kv_tile0, 0)),
                pl.BlockSpec((1, 1, kv_len, LANES), lambda bi, h, i: (bi, h, kv_tile0, 0))]
    args = [q, k, v]
    if diff:
        in_specs += [pl.BlockSpec((1, 4, DIFF_QK), lambda bi, h, i: (0, 0, 0)),
                     pl.BlockSpec((1, LANES), lambda bi, h, i: (0, 0))]
        args += [lamv, sub]
    aliases = {}
    if o_prev is not None:
        in_specs.append(pl.BlockSpec(memory_space=pl.ANY))
        args.append(o_prev)
        aliases = {len(args) - 1: 0}
    kern = functools.partial(_flash_kernel, group=group, tq=tq, chunks=chunks, diff=diff,
                             lam_init=lam_init)
    if o_prev is not None:
        kern = functools.partial(_drop_ref, kern, len(args) - 1)
    return pl.pallas_call(
        kern,
        grid=(b, heads, n_q_tiles),
        in_specs=in_specs,
        out_specs=pl.BlockSpec((1, tq, out_w), lambda bi, h, i: (bi, q_tile0 + i, h)),
        out_shape=jax.ShapeDtypeStruct((b, n_tokens, BRANCH_WIDTH), BF16),
        scratch_shapes=[pltpu.VMEM((8, LANES), F32)],
        input_output_aliases=aliases,
        compiler_params=_params(("parallel", "parallel", "arbitrary")),
        name=name,
    )(*args)


def _drop_ref(kern, idx, *refs):
    kern(*refs[:idx], *refs[idx + 1:])


N_CHUNK = 512


def _merge_kernel(oa_ref, ob_ref, oc_ref, g_ref, ma_ref, mb_ref, mc_ref, w_ref, y_ref):
    ogs = [o_ref[0] * g_ref[0, :, r * BRANCH_WIDTH:(r + 1) * BRANCH_WIDTH]
           for r, o_ref in enumerate((oa_ref, ob_ref, oc_ref))]
    for c in range(y_ref.shape[-1] // N_CHUNK):
        cols = slice(c * N_CHUNK, (c + 1) * N_CHUNK)
        acc = None
        for r, m_ref in enumerate((ma_ref, mb_ref, mc_ref)):
            z = jnp.dot(ogs[r], w_ref[r, :, cols], preferred_element_type=F32)
            t = m_ref[0, :, cols].astype(F32) * z
            acc = t if acc is None else acc + t
        y_ref[0, :, cols] = acc.astype(BF16)


def _merge_call(oa, ob, oc, gm, w_br, rows, tm):
    b = oa.shape[0]
    d = w_br.shape[-1]
    gate_w = 3 * BRANCH_WIDTH
    assert (3 * d) % gate_w == 0
    o_spec = pl.BlockSpec((1, tm, BRANCH_WIDTH), lambda bi, i: (bi, i, 0))
    m_specs = [pl.BlockSpec((1, tm, d), lambda bi, i, r=r: (bi, i, r)) for r in range(3)]
    return pl.pallas_call(
        _merge_kernel,
        grid=(b, rows // tm),
        in_specs=[o_spec, o_spec, o_spec,
                  pl.BlockSpec((1, tm, gate_w), lambda bi, i: (bi, i, 3 * d // gate_w)),
                  *m_specs,
                  _resident(w_br.shape, lambda bi, i: (0, 0, 0))],
        out_specs=pl.BlockSpec((1, tm, d), lambda bi, i: (bi, i, 0)),
        out_shape=jax.ShapeDtypeStruct((b, rows, d), BF16),
        compiler_params=_params(("parallel", "parallel")),
        name="branch_merge",
    )(oa, ob, oc, gm, gm, gm, gm, w_br)


def _out_kernel(y_ref, w_ref, x_ref, mod_ref, o_ref):
    y = y_ref[0]
    for c in range(o_ref.shape[-1] // N_CHUNK):
        cols = slice(c * N_CHUNK, (c + 1) * N_CHUNK)
        out = jnp.dot(y, w_ref[:, cols], preferred_element_type=F32)
        o_ref[0, :, cols] = x_ref[0, :, cols] + mod_ref[0, 2:3, cols] * out


def _out_call(y, w_out, xs, mod, *, tm, y_tile0, ctx_stream, name):
    b, rows, d = xs.shape
    return pl.pallas_call(
        _out_kernel,
        grid=(b, rows // tm),
        in_specs=[pl.BlockSpec((1, tm, d), lambda bi, i: (bi, y_tile0 + i, 0)),
                  _resident(w_out.shape, lambda bi, i: (0, 0)),
                  pl.BlockSpec((1, tm, d), lambda bi, i: (bi, i, 0)),
                  pl.BlockSpec((1, 3, d), lambda bi, i: (b if ctx_stream else bi, 0, 0))],
        out_specs=pl.BlockSpec((1, tm, d), lambda bi, i: (bi, i, 0)),
        out_shape=jax.ShapeDtypeStruct((b, rows, d), F32),
        compiler_params=_params(("parallel", "parallel")),
        name=name,
    )(y, w_out, xs, mod)


def _rope_tables(n_lat, n_ctx, rot_dim):
    axis_dim = rot_dim // 2
    t = jnp.arange(n_lat, dtype=jnp.int32)
    pos_row = (t // GRID_W).astype(F32)
    pos_col = (t % GRID_W).astype(F32)
    inv_freq = ROPE_THETA ** (-jnp.arange(0, axis_dim, 2, dtype=F32) / axis_dim)
    ang_r = pos_row[:, None] * inv_freq
    ang_c = pos_col[:, None] * inv_freq
    ang = jnp.concatenate([ang_r, ang_r, ang_c, ang_c], axis=-1)
    cos, sin = jnp.cos(ang), jnp.sin(ang)
    lane = jnp.arange(rot_dim)
    sign = jnp.where((lane % axis_dim) < axis_dim // 2, -1.0, 1.0).astype(F32)
    reps = LANES // rot_dim
    cos = jnp.tile(cos, (1, reps))
    sin = jnp.tile(sin * sign, (1, reps))
    cos = jnp.concatenate([cos, jnp.ones((n_ctx, LANES), F32)], axis=0)
    sin = jnp.concatenate([sin, jnp.zeros((n_ctx, LANES), F32)], axis=0)
    return cos, sin


def kernel(x, c, ctx, c_ctx, norm_w, w_ada, b_ada, w_in, b_merge, gqa_q_norm, gqa_k_norm, mla_q_nope_norm, mla_q_rope_norm, mla_kv_norm, mla_w_uk, mla_w_uv, mla_k_nope_norm, mla_k_rope_norm, diff_q_norm, diff_k_norm, diff_lambda_q1, diff_lambda_k1, diff_lambda_q2, diff_lambda_k2, diff_subln, w_br_gqa, w_br_mla, w_br_diff, w_out):
    b, n_lat, d = x.shape
    n_ctx = ctx.shape[1]
    n = n_lat + n_ctx
    depth = w_in.shape[0]
    assert n_lat % 2048 == 0 and n_ctx == 256 and n % 8 == 0 and b < 8

    c_all = jnp.concatenate([c, c_ctx[None], jnp.zeros((8 - b - 1, d), F32)], axis=0)
    mod = _ada_call(c_all, w_ada, b_ada).reshape(depth, 8, 3, d)

    o_aq = GQA_HEADS * HEAD_DIM + 2 * GQA_KV_HEADS * HEAD_DIM
    o_mq = o_aq + MLA_HEADS * (MLA_NOPE + MLA_ROPE)
    o_ckv = o_mq + MLA_KV_RANK
    o_kr = o_ckv + MLA_ROPE
    o_dend = o_kr + 3 * DIFF_HEADS * 2 * DIFF_QK
    w_a = w_in[:, :, :o_aq].astype(BF16)
    mq = w_in[:, :, o_aq:o_mq].reshape(depth, d, MLA_HEADS, MLA_NOPE + MLA_ROPE)
    mq = jnp.pad(mq, ((0, 0), (0, 0), (0, 0), (0, MLA_QK_PAD - MLA_NOPE - MLA_ROPE)))
    w_b = jnp.concatenate(
        [mq.reshape(depth, d, MLA_HEADS * MLA_QK_PAD), w_in[:, :, o_mq:o_ckv],
         jnp.pad(w_in[:, :, o_ckv:o_kr], ((0, 0), (0, 0), (0, LANES - MLA_ROPE)))], axis=-1).astype(BF16)
    w_c = w_in[:, :, o_kr:o_dend].astype(BF16)
    o_gend = o_dend + 3 * BRANCH_WIDTH
    w_g = jnp.concatenate([w_in[:, :, o_gend:], w_in[:, :, o_dend:o_gend]], axis=-1).astype(BF16)
    w_ukv = jnp.concatenate([mla_w_uk, mla_w_uv], axis=-1).astype(BF16)
    w_br = jnp.stack([w_br_gqa, w_br_mla, w_br_diff], axis=1).astype(BF16)
    w_o = w_out.astype(BF16)
    bias_g = jnp.concatenate([b_merge, jnp.zeros((depth, 3 * BRANCH_WIDTH), F32)], axis=-1)
    g_bq = jnp.concatenate([mla_q_nope_norm, mla_q_rope_norm,
                            jnp.zeros((depth, MLA_QK_PAD - MLA_NOPE - MLA_ROPE), F32)], axis=-1)
    g_kr = jnp.pad(mla_k_rope_norm, ((0, 0), (0, LANES - MLA_ROPE)))
    g_dq = jnp.tile(diff_q_norm, (1, 2))
    g_dk = jnp.tile(diff_k_norm, (1, 2))
    lamv = jnp.stack([diff_lambda_q1, diff_lambda_k1, diff_lambda_q2, diff_lambda_k2], axis=1)

    cos128, sin128 = _rope_tables(n_lat, n_ctx, HEAD_DIM)
    cos64, sin64 = _rope_tables(n_lat, n_ctx, MLA_ROPE)

    lat_chunks = tuple((s, 1024) for s in range(0, n_lat, 1024)) + ((n_lat, n_ctx),)
    assert sum(size for _, size in lat_chunks) == n
    ctx_chunks = ((0, n_ctx),)
    ctx_tile = n_lat // n_ctx

    for l in range(depth):
        last = l == depth - 1
        lam_init = 0.8 - 0.6 * math.exp(-0.3 * l)
        h = _h_call(x, norm_w[l][None], mod[l], None, n=n, tile0=0, ctx_stream=False, name="norm_mod")
        h = _h_call(ctx, norm_w[l][None], mod[l], h, n=n, tile0=ctx_tile, ctx_stream=True,
                    name="norm_mod_ctx")

        qa, ka, va = _proj_a_call(h, w_a[l], gqa_q_norm[l][None], gqa_k_norm[l][None], cos128, sin128)
        qb, ckv, kr = _proj_b_call(h, w_b[l], g_bq[l][None], mla_kv_norm[l][None], g_kr[l][None],
                                   cos64, sin64)
        qc, kc, vc = _proj_c_call(h, w_c[l], g_dq[l][None], g_dk[l][None], cos64, sin64)
        gm = _gate_call(h, w_g[l], bias_g[l][None])
        kb, vb = _mla_up_call(ckv, kr, w_ukv[l], mla_k_nope_norm[l][None])

        qa = qa.reshape(b, GQA_KV_HEADS, GQA_GROUP, n, HEAD_DIM)
        qb = qb.reshape(b, MLA_HEADS, 1, n, MLA_QK_PAD)
        rows = n_lat if last else n
        diff_kw = dict(diff=True, lamv=lamv[l][None], sub=diff_subln[l][None], lam_init=lam_init)

        oa = _flash_call(qa, ka, va, None, tq=256, q_tile0=0, n_q_tiles=n_lat // 256, kv_tile0=0,
                         chunks=lat_chunks, n_tokens=rows, name="flash_gqa")
        ob = _flash_call(qb, kb, vb, None, tq=1024, q_tile0=0, n_q_tiles=n_lat // 1024, kv_tile0=0,
                         chunks=lat_chunks, n_tokens=rows, name="flash_mla")
        oc = _flash_call(qc, kc, vc, None, tq=512, q_tile0=0, n_q_tiles=n_lat // 512, kv_tile0=0,
                         chunks=lat_chunks, n_tokens=rows, name="flash_diff", **diff_kw)
        if not last:
            ctx_kw = dict(tq=n_ctx, q_tile0=ctx_tile, n_q_tiles=1, kv_tile0=ctx_tile,
                          chunks=ctx_chunks, n_tokens=rows)
            oa = _flash_call(qa, ka, va, oa, name="flash_gqa_ctx", **ctx_kw)
            ob = _flash_call(qb, kb, vb, ob, name="flash_mla_ctx", **ctx_kw)
            oc = _flash_call(qc, kc, vc, oc, name="flash_diff_ctx", **ctx_kw, **diff_kw)

        y = _merge_call(oa, ob, oc, gm, w_br[l], rows, rows // 8)
        if not last:
            ctx = _out_call(y, w_o[l], ctx, mod[l], tm=n_ctx, y_tile0=ctx_tile, ctx_stream=True,
                            name="out_proj_ctx")
        x = _out_call(y, w_o[l], x, mod[l], tm=512, y_tile0=0, ctx_stream=False, name="out_proj")
    return x
```

```python
import functools
import math

import jax
import jax.numpy as jnp
from jax import lax
from jax.experimental import pallas as pl
from jax.experimental.pallas import tpu as pltpu

F32 = jnp.float32
BF16 = jnp.bfloat16

GRID_W = 64
ROPE_THETA = 10000.0
EPS = 1e-6

HEAD_DIM = 128
BRANCH_WIDTH = 1024
GQA_HEADS = 8
GQA_KV_HEADS = 2
GQA_GROUP = GQA_HEADS // GQA_KV_HEADS
MLA_HEADS = 8
MLA_NOPE = 128
MLA_ROPE = 64
MLA_QK_PAD = 256
MLA_KV_RANK = 512
DIFF_HEADS = 8
DIFF_QK = 64

LANES = 128
MXU_N = 256
VMEM_LIMIT = 56 * 1024 * 1024
LOG2E = 1.4426950408889634
SAFE_BOUND_LOG2 = 57.0


def _params(sem, vmem=VMEM_LIMIT):
    return pltpu.CompilerParams(dimension_semantics=sem, vmem_limit_bytes=vmem)


def _resident(shape, index_map):
    return pl.BlockSpec(shape, index_map, pipeline_mode=pl.Buffered(1))


def _rms(y, gain, n):
    ms = jnp.sum(y * y, axis=-1, keepdims=True) * (1.0 / n)
    return y * lax.rsqrt(ms + EPS) * gain


def _seg_rms(y, gain, lo):
    ss = y * y
    s_lo = jnp.sum(jnp.where(lo, ss, 0.0), axis=-1, keepdims=True)
    s_hi = jnp.sum(jnp.where(lo, 0.0, ss), axis=-1, keepdims=True)
    ms = jnp.where(lo, s_lo, s_hi) * (1.0 / DIFF_QK)
    return y * lax.rsqrt(ms + EPS) * gain


def _rope(y, cos, sin_signed, half):
    lane = lax.broadcasted_iota(jnp.int32, (1, LANES), 1)
    lower = (lane & (2 * half - 1)) < half
    partner = jnp.where(lower, pltpu.roll(y, LANES - half, 1), pltpu.roll(y, half, 1))
    return y * cos + partner * sin_signed


def _sigmoid(z):
    return 1.0 / (1.0 + jnp.exp(-z))


def _ada_kernel(c_ref, w_ref, b_ref, o_ref):
    c = c_ref[...]
    a = (c * _sigmoid(c)).astype(BF16)
    o_ref[0] = jnp.dot(a, w_ref[0].astype(BF16), preferred_element_type=F32) + b_ref[0]


def _ada_call(c_all, w_ada, b_ada):
    depth, d, n3 = w_ada.shape
    tn = 512
    return pl.pallas_call(
        _ada_kernel,
        grid=(depth, n3 // tn),
        in_specs=[pl.BlockSpec((8, d), lambda l, j: (0, 0)),
                  pl.BlockSpec((1, d, tn), lambda l, j: (l, 0, j)),
                  pl.BlockSpec((1, 1, tn), lambda l, j: (l, 0, j))],
        out_specs=pl.BlockSpec((1, 8, tn), lambda l, j: (l, 0, j)),
        out_shape=jax.ShapeDtypeStruct((depth, 8, n3), F32),
        compiler_params=_params(("parallel", "parallel")),
        name="ada_mod",
    )(c_all, w_ada, b_ada.reshape(depth, 1, n3))


def _h_kernel(x_ref, nw_ref, mod_ref, h_ref):
    x = x_ref[0]
    ms = jnp.mean(x * x, axis=-1, keepdims=True)
    y = x * lax.rsqrt(ms + EPS) * nw_ref[...]
    shift = mod_ref[0, 0:1, :]
    scale = mod_ref[0, 1:2, :]
    h_ref[0] = (y * (1.0 + scale) + shift).astype(BF16)


def _h_call(xs, norm_w, mod, h_prev, *, n, tile0, ctx_stream, name):
    b, rows, d = xs.shape
    tr = 256
    in_specs = [pl.BlockSpec((1, tr, d), lambda bi, t: (bi, t, 0)),
                pl.BlockSpec((1, d), lambda bi, t: (0, 0)),
                pl.BlockSpec((1, 3, d), lambda bi, t: (b if ctx_stream else bi, 0, 0))]
    args = [xs, norm_w, mod]
    kern, aliases = _h_kernel, {}
    if h_prev is not None:
        in_specs.append(pl.BlockSpec(memory_space=pl.ANY))
        args.append(h_prev)
        kern, aliases = functools.partial(_drop_ref, _h_kernel, 3), {3: 0}
    return pl.pallas_call(
        kern,
        grid=(b, rows // tr),
        in_specs=in_specs,
        out_specs=pl.BlockSpec((1, tr, d), lambda bi, t: (bi, tile0 + t, 0)),
        out_shape=jax.ShapeDtypeStruct((b, n, d), BF16),
        input_output_aliases=aliases,
        compiler_params=_params(("parallel", "parallel")),
        name=name,
    )(*args)


def _proj_a_kernel(h_ref, w_ref, gq_ref, gk_ref, cos_ref, sin_ref, q_ref, k_ref, v_ref):
    h = h_ref[0]
    cos = cos_ref[...]
    sin = sin_ref[...]
    scale = HEAD_DIM ** -0.5 * LOG2E
    n_q = GQA_HEADS * HEAD_DIM // MXU_N
    for c in range(n_q + 2):
        y = jnp.dot(h, w_ref[:, c * MXU_N:(c + 1) * MXU_N], preferred_element_type=F32)
        for u in range(2):
            yu = y[:, u * LANES:(u + 1) * LANES]
            if c < n_q:
                z = _rope(_rms(yu, gq_ref[...], HEAD_DIM), cos, sin, HEAD_DIM // 4)
                q_ref[0, 2 * c + u] = (z * scale).astype(BF16)
            elif c == n_q:
                z = _rope(_rms(yu, gk_ref[...], HEAD_DIM), cos, sin, HEAD_DIM // 4)
                k_ref[0, u] = z.astype(BF16)
            else:
                v_ref[0, u] = yu.astype(BF16)


def _proj_b_kernel(h_ref, w_ref, gq_ref, gckv_ref, gkr_ref, cos_ref, sin_ref,
                   q_ref, ckv_ref, kr_ref):
    h = h_ref[0]
    cos = cos_ref[...]
    sin = sin_ref[...]
    scale = (MLA_NOPE + MLA_ROPE) ** -0.5 * LOG2E
    for hd in range(MLA_HEADS):
        y = jnp.dot(h, w_ref[:, hd * MLA_QK_PAD:(hd + 1) * MLA_QK_PAD], preferred_element_type=F32)
        nope = _rms(y[:, :LANES], gq_ref[:, :LANES], MLA_NOPE)
        rope = _rope(_rms(y[:, LANES:], gq_ref[:, LANES:], MLA_ROPE), cos, sin, MLA_ROPE // 4)
        q_ref[0, hd, :, 0:LANES] = (nope * scale).astype(BF16)
        q_ref[0, hd, :, LANES:2 * LANES] = (rope * scale).astype(BF16)
    base = MLA_HEADS * MLA_QK_PAD
    y0 = jnp.dot(h, w_ref[:, base:base + MXU_N], preferred_element_type=F32)
    y1 = jnp.dot(h, w_ref[:, base + MXU_N:base + 2 * MXU_N], preferred_element_type=F32)
    ms = (jnp.sum(y0 * y0, axis=-1, keepdims=True)
          + jnp.sum(y1 * y1, axis=-1, keepdims=True)) * (1.0 / MLA_KV_RANK)
    inv = lax.rsqrt(ms + EPS)
    ckv_ref[0, :, 0:MXU_N] = (y0 * inv * gckv_ref[:, 0:MXU_N]).astype(BF16)
    ckv_ref[0, :, MXU_N:2 * MXU_N] = (y1 * inv * gckv_ref[:, MXU_N:2 * MXU_N]).astype(BF16)
    base += MLA_KV_RANK
    yk = jnp.dot(h, w_ref[:, base:base + LANES], preferred_element_type=F32)
    kr = _rope(_rms(yk, gkr_ref[...], MLA_ROPE), cos, sin, MLA_ROPE // 4)
    kr_ref[0] = kr.astype(BF16)


def _proj_c_kernel(h_ref, w_ref, gq_ref, gk_ref, cos_ref, sin_ref, q_ref, k_ref, v_ref):
    h = h_ref[0]
    cos = cos_ref[...]
    sin = sin_ref[...]
    scale = DIFF_QK ** -0.5 * LOG2E
    lo = lax.broadcasted_iota(jnp.int32, (1, LANES), 1) < DIFF_QK
    per = DIFF_HEADS // 2
    for c in range(3 * per):
        y = jnp.dot(h, w_ref[:, c * MXU_N:(c + 1) * MXU_N], preferred_element_type=F32)
        for u in range(2):
            yu = y[:, u * LANES:(u + 1) * LANES]
            if c < per:
                z = _rope(_seg_rms(yu, gq_ref[...], lo), cos, sin, DIFF_QK // 4) * scale
                q_ref[0, 2 * c + u, 0] = jnp.where(lo, z, 0.0).astype(BF16)
                q_ref[0, 2 * c + u, 1] = jnp.where(lo, 0.0, z).astype(BF16)
            elif c < 2 * per:
                z = _rope(_seg_rms(yu, gk_ref[...], lo), cos, sin, DIFF_QK // 4)
                k_ref[0, 2 * (c - per) + u] = z.astype(BF16)
            else:
                v_ref[0, 2 * (c - 2 * per) + u] = yu.astype(BF16)


def _proj_tm(n):
    return n // 8


def _proj_a_call(h, w, gq, gk, cos, sin):
    b, n, d = h.shape
    tm = _proj_tm(n)
    row = lambda bi, i: (bi, i, 0)
    head = lambda bi, i: (bi, 0, i, 0)
    const = lambda bi, i: (0, 0)
    return pl.pallas_call(
        _proj_a_kernel,
        grid=(b, n // tm),
        in_specs=[pl.BlockSpec((1, tm, d), row),
                  _resident(w.shape, const),
                  pl.BlockSpec((1, LANES), const), pl.BlockSpec((1, LANES), const),
                  pl.BlockSpec((tm, LANES), lambda bi, i: (i, 0)),
                  pl.BlockSpec((tm, LANES), lambda bi, i: (i, 0))],
        out_specs=[pl.BlockSpec((1, GQA_HEADS, tm, HEAD_DIM), head),
                   pl.BlockSpec((1, GQA_KV_HEADS, tm, HEAD_DIM), head),
                   pl.BlockSpec((1, GQA_KV_HEADS, tm, HEAD_DIM), head)],
        out_shape=[jax.ShapeDtypeStruct((b, GQA_HEADS, n, HEAD_DIM), BF16),
                   jax.ShapeDtypeStruct((b, GQA_KV_HEADS, n, HEAD_DIM), BF16),
                   jax.ShapeDtypeStruct((b, GQA_KV_HEADS, n, HEAD_DIM), BF16)],
        compiler_params=_params(("parallel", "parallel")),
        name="proj_gqa",
    )(h, w, gq, gk, cos, sin)


def _proj_b_call(h, w, gq, gckv, gkr, cos, sin):
    b, n, d = h.shape
    tm = _proj_tm(n)
    row = lambda bi, i: (bi, i, 0)
    const = lambda bi, i: (0, 0)
    return pl.pallas_call(
        _proj_b_kernel,
        grid=(b, n // tm),
        in_specs=[pl.BlockSpec((1, tm, d), row),
                  _resident(w.shape, const),
                  pl.BlockSpec((1, MLA_QK_PAD), const),
                  pl.BlockSpec((1, MLA_KV_RANK), const),
                  pl.BlockSpec((1, LANES), const),
                  pl.BlockSpec((tm, LANES), lambda bi, i: (i, 0)),
                  pl.BlockSpec((tm, LANES), lambda bi, i: (i, 0))],
        out_specs=[pl.BlockSpec((1, MLA_HEADS, tm, MLA_QK_PAD), lambda bi, i: (bi, 0, i, 0)),
                   pl.BlockSpec((1, tm, MLA_KV_RANK), row),
                   pl.BlockSpec((1, tm, LANES), row)],
        out_shape=[jax.ShapeDtypeStruct((b, MLA_HEADS, n, MLA_QK_PAD), BF16),
                   jax.ShapeDtypeStruct((b, n, MLA_KV_RANK), BF16),
                   jax.ShapeDtypeStruct((b, n, LANES), BF16)],
        compiler_params=_params(("parallel", "parallel")),
        name="proj_mla",
    )(h, w, gq, gckv, gkr, cos, sin)


def _proj_c_call(h, w, gq, gk, cos, sin):
    b, n, d = h.shape
    tm = _proj_tm(n)
    row = lambda bi, i: (bi, i, 0)
    const = lambda bi, i: (0, 0)
    return pl.pallas_call(
        _proj_c_kernel,
        grid=(b, n // tm),
        in_specs=[pl.BlockSpec((1, tm, d), row),
                  _resident(w.shape, const),
                  pl.BlockSpec((1, LANES), const), pl.BlockSpec((1, LANES), const),
                  pl.BlockSpec((tm, LANES), lambda bi, i: (i, 0)),
                  pl.BlockSpec((tm, LANES), lambda bi, i: (i, 0))],
        out_specs=[pl.BlockSpec((1, DIFF_HEADS, 2, tm, LANES), lambda bi, i: (bi, 0, 0, i, 0)),
                   pl.BlockSpec((1, DIFF_HEADS, tm, LANES), lambda bi, i: (bi, 0, i, 0)),
                   pl.BlockSpec((1, DIFF_HEADS, tm, LANES), lambda bi, i: (bi, 0, i, 0))],
        out_shape=[jax.ShapeDtypeStruct((b, DIFF_HEADS, 2, n, LANES), BF16),
                   jax.ShapeDtypeStruct((b, DIFF_HEADS, n, LANES), BF16),
                   jax.ShapeDtypeStruct((b, DIFF_HEADS, n, LANES), BF16)],
        compiler_params=_params(("parallel", "parallel")),
        name="proj_diff",
    )(h, w, gq, gk, cos, sin)


def _mla_up_kernel(ckv_ref, kr_ref, w_ref, gk_ref, k_ref, v_ref):
    ckv = ckv_ref[0]
    kr = kr_ref[0]
    per = MLA_HEADS // 2
    for c in range(2 * per):
        y = jnp.dot(ckv, w_ref[:, c * MXU_N:(c + 1) * MXU_N], preferred_element_type=F32)
        for u in range(2):
            yu = y[:, u * LANES:(u + 1) * LANES]
            if c < per:
                k_ref[0, 2 * c + u, :, 0:LANES] = _rms(yu, gk_ref[...], MLA_NOPE).astype(BF16)
                k_ref[0, 2 * c + u, :, LANES:2 * LANES] = kr
            else:
                v_ref[0, 2 * (c - per) + u] = yu.astype(BF16)


def _mla_up_call(ckv, kr, w, gk):
    b, n, r = ckv.shape
    tm = _proj_tm(n)
    row = lambda bi, i: (bi, i, 0)
    const = lambda bi, i: (0, 0)
    return pl.pallas_call(
        _mla_up_kernel,
        grid=(b, n // tm),
        in_specs=[pl.BlockSpec((1, tm, r), row),
                  pl.BlockSpec((1, tm, LANES), row),
                  _resident(w.shape, const),
                  pl.BlockSpec((1, LANES), const)],
        out_specs=[pl.BlockSpec((1, MLA_HEADS, tm, MLA_QK_PAD), lambda bi, i: (bi, 0, i, 0)),
                   pl.BlockSpec((1, MLA_HEADS, tm, LANES), lambda bi, i: (bi, 0, i, 0))],
        out_shape=[jax.ShapeDtypeStruct((b, MLA_HEADS, n, MLA_QK_PAD), BF16),
                   jax.ShapeDtypeStruct((b, MLA_HEADS, n, LANES), BF16)],
        compiler_params=_params(("parallel", "parallel")),
        name="mla_up",
    )(ckv, kr, w, gk)


def _gate_kernel(h_ref, w_ref, b_ref, o_ref, *, n_sigmoid_tiles):
    is_sigmoid = pl.program_id(2) < n_sigmoid_tiles
    h = h_ref[0]
    for c in range(o_ref.shape[-1] // MXU_N):
        cols = slice(c * MXU_N, (c + 1) * MXU_N)
        y = jnp.dot(h, w_ref[:, cols], preferred_element_type=F32)
        t = _sigmoid(y + b_ref[:, cols])
        o_ref[0, :, cols] = jnp.where(is_sigmoid, t, y * t).astype(BF16)


def _gate_call(h, w, bias):
    b, n, d = h.shape
    cols = w.shape[1]
    tm = n // 4
    tn = 1536
    assert (cols - 3 * BRANCH_WIDTH) % tn == 0 and cols % tn == 0
    return pl.pallas_call(
        functools.partial(_gate_kernel, n_sigmoid_tiles=(cols - 3 * BRANCH_WIDTH) // tn),
        grid=(b, n // tm, cols // tn),
        in_specs=[pl.BlockSpec((1, tm, d), lambda bi, i, j: (bi, i, 0)),
                  pl.BlockSpec((d, tn), lambda bi, i, j: (0, j)),
                  pl.BlockSpec((1, tn), lambda bi, i, j: (0, j))],
        out_specs=pl.BlockSpec((1, tm, tn), lambda bi, i, j: (bi, i, j)),
        out_shape=jax.ShapeDtypeStruct((b, n, cols), BF16),
        compiler_params=_params(("parallel", "parallel", "parallel")),
        name="gate_proj",
    )(h, w, bias)


def _flash_kernel(*refs, group, tq, chunks, diff, lam_init):
    if diff:
        bound_ref, q_ref, k_ref, v_ref, vt_ref, lamv_ref, sub_ref, o_ref = refs
    else:
        bound_ref, q_ref, k_ref, v_ref, vt_ref, o_ref = refs
    dq = q_ref.shape[-1]
    nt = (((1,), (1,)), ((), ()))

    q = q_ref[0, 0].reshape(group * tq, dq)
    half = group * tq // 2
    halves = (slice(0, half), slice(half, 2 * half))
    bound = bound_ref[0]
    safe = bound <= SAFE_BOUND_LOG2

    def finish(out):
        if diff:
            lv = lamv_ref[0]
            lam = (jnp.exp(jnp.sum(lv[0:1] * lv[1:2], axis=-1, keepdims=True))
                   - jnp.exp(jnp.sum(lv[2:3] * lv[3:4], axis=-1, keepdims=True)) + lam_init)
            o = out[:tq] - lam * out[tq:]
            o_ref[0] = (_rms(o, sub_ref[...], LANES) * (1.0 - lam_init)).astype(BF16)
        else:
            for g in range(group):
                o_ref[0, :, g * LANES:(g + 1) * LANES] = out[g * tq:(g + 1) * tq].astype(BF16)

    @pl.when(safe)
    def _():
        accs = [None, None]

        def k_dot_q(start, size):
            k = k_ref[0, 0, start:start + size, :]
            return [lax.dot_general(k, q[rows], nt, preferred_element_type=F32) for rows in halves]

        def vt_dot_pt(pts, start, size):
            vt = vt_ref[0, 0, :, start:start + size]
            for i, pt in enumerate(pts):
                t = jnp.dot(vt, pt, preferred_element_type=F32)
                accs[i] = t if accs[i] is None else accs[i] + t

        pending = None
        for start, size in chunks:
            ss = k_dot_q(start, size)
            if pending is not None:
                vt_dot_pt(*pending)
            pending = ([jnp.exp2(s - bound).astype(BF16) for s in ss], start, size)
        vt_dot_pt(*pending)
        acc = jnp.concatenate(accs, axis=1)
        finish((acc[:LANES] / acc[LANES:LANES + 1]).T)

    @pl.when(jnp.logical_not(safe))
    def _():
        m = l = acc = None
        for start, size in chunks:
            k = k_ref[0, 0, start:start + size, :]
            v = v_ref[0, 0, start:start + size, :]
            s = lax.dot_general(q, k, nt, preferred_element_type=F32)
            m_cur = jnp.max(s, axis=-1, keepdims=True)
            if m is None:
                m = m_cur
                p = jnp.exp2(s - m)
                l = jnp.sum(p, axis=-1, keepdims=True)
                acc = jnp.dot(p.astype(BF16), v, preferred_element_type=F32)
            else:
                m_new = jnp.maximum(m, m_cur)
                alpha = jnp.exp2(m - m_new)
                p = jnp.exp2(s - m_new)
                l = alpha * l + jnp.sum(p, axis=-1, keepdims=True)
                acc = alpha * acc + jnp.dot(p.astype(BF16), v, preferred_element_type=F32)
                m = m_new
        finish(acc / l)


VT_ROWS = LANES + 16


def _vt_kernel(v_ref, o_ref):
    n = v_ref.shape[2]
    for j in range(n // LANES):
        cols = slice(j * LANES, (j + 1) * LANES)
        o_ref[0, 0, 0:LANES, cols] = v_ref[0, 0, cols, :].astype(F32).T.astype(BF16)
    o_ref[0, 0, LANES:VT_ROWS, :] = jnp.ones((VT_ROWS - LANES, n), BF16)


def _vt_call(v, name):
    b, heads, n, dv = v.shape
    return pl.pallas_call(
        _vt_kernel,
        grid=(b, heads),
        in_specs=[pl.BlockSpec((1, 1, n, dv), lambda bi, h: (bi, h, 0, 0))],
        out_specs=pl.BlockSpec((1, 1, VT_ROWS, n), lambda bi, h: (bi, h, 0, 0)),
        out_shape=jax.ShapeDtypeStruct((b, heads, VT_ROWS, n), BF16),
        compiler_params=_params(("parallel", "parallel")),
        name=name,
    )(v)


def _flash_call(bound, q, k, v, vt, o_prev, *, tq, q_tile0, n_q_tiles, kv_tile0, chunks, n_tokens,
                diff=False, lamv=None, sub=None, lam_init=0.0, name="flash"):
    b, heads, group, n, dq = q.shape
    kv_len = sum(size for _, size in chunks)
    out_w = LANES if diff else group * LANES
    in_specs = [pl.BlockSpec(memory_space=pltpu.SMEM),
                pl.BlockSpec((1, 1, group, tq, dq), lambda bi, h, i: (bi, h, 0, q_tile0 + i, 0)),
                pl.BlockSpec((1, 1, kv_len, dq), lambda bi, h, i: (bi, h, kv_tile0, 0)),
                pl.BlockSpec((1, 1, kv_len, LANES), lambda bi, h, i: (bi, h, kv_tile0, 0)),
                pl.BlockSpec((1, 1, VT_ROWS, kv_len), lambda bi, h, i: (bi, h, 0, kv_tile0))]
    args = [bound, q, k, v, vt]
    if diff:
        in_specs += [pl.BlockSpec((1, 4, DIFF_QK), lambda bi, h, i: (0, 0, 0)),
                     pl.BlockSpec((1, LANES), lambda bi, h, i: (0, 0))]
        args += [lamv, sub]
    aliases = {}
    if o_prev is not None:
        in_specs.append(pl.BlockSpec(memory_space=pl.ANY))
        args.append(o_prev)
        aliases = {len(args) - 1: 0}
    kern = functools.partial(_flash_kernel, group=group, tq=tq, chunks=chunks, diff=diff,
                             lam_init=lam_init)
    if o_prev is not None:
        kern = functools.partial(_drop_ref, kern, len(args) - 1)
    return pl.pallas_call(
        kern,
        grid=(b, heads, n_q_tiles),
        in_specs=in_specs,
        out_specs=pl.BlockSpec((1, tq, out_w), lambda bi, h, i: (bi, q_tile0 + i, h)),
        out_shape=jax.ShapeDtypeStruct((b, n_tokens, BRANCH_WIDTH), BF16),
        input_output_aliases=aliases,
        compiler_params=_params(("parallel", "parallel", "parallel")),
        name=name,
    )(*args)


def _score_bound(q_terms, k_terms, scale):
    q2 = sum(d * jnp.max(jnp.square(g.astype(F32))) for g, d in q_terms)
    k2 = sum(d * jnp.max(jnp.square(g.astype(F32))) for g, d in k_terms)
    return (1.01 * scale * LOG2E * jnp.sqrt(q2 * k2)).reshape(1)


def _drop_ref(kern, idx, *refs):
    kern(*refs[:idx], *refs[idx + 1:])


N_CHUNK = 512


def _merge_kernel(oa_ref, ob_ref, oc_ref, g_ref, ma_ref, mb_ref, mc_ref, w_ref, y_ref):
    ogs = [o_ref[0] * g_ref[0, :, r * BRANCH_WIDTH:(r + 1) * BRANCH_WIDTH]
           for r, o_ref in enumerate((oa_ref, ob_ref, oc_ref))]
    for c in range(y_ref.shape[-1] // N_CHUNK):
        cols = slice(c * N_CHUNK, (c + 1) * N_CHUNK)
        acc = None
        for r, m_ref in enumerate((ma_ref, mb_ref, mc_ref)):
            z = jnp.dot(ogs[r], w_ref[r, :, cols], preferred_element_type=F32)
            t = m_ref[0, :, cols].astype(F32) * z
            acc = t if acc is None else acc + t
        y_ref[0, :, cols] = acc.astype(BF16)


def _merge_call(oa, ob, oc, gm, w_br, rows, tm):
    b = oa.shape[0]
    d = w_br.shape[-1]
    gate_w = 3 * BRANCH_WIDTH
    assert (3 * d) % gate_w == 0
    o_spec = pl.BlockSpec((1, tm, BRANCH_WIDTH), lambda bi, i: (bi, i, 0))
    m_specs = [pl.BlockSpec((1, tm, d), lambda bi, i, r=r: (bi, i, r)) for r in range(3)]
    return pl.pallas_call(
        _merge_kernel,
        grid=(b, rows // tm),
        in_specs=[o_spec, o_spec, o_spec,
                  pl.BlockSpec((1, tm, gate_w), lambda bi, i: (bi, i, 3 * d // gate_w)),
                  *m_specs,
                  _resident(w_br.shape, lambda bi, i: (0, 0, 0))],
        out_specs=pl.BlockSpec((1, tm, d), lambda bi, i: (bi, i, 0)),
        out_shape=jax.ShapeDtypeStruct((b, rows, d), BF16),
        compiler_params=_params(("parallel", "parallel")),
        name="branch_merge",
    )(oa, ob, oc, gm, gm, gm, gm, w_br)


def _out_kernel(y_ref, w_ref, x_ref, mod_ref, o_ref):
    y = y_ref[0]
    for c in range(o_ref.shape[-1] // N_CHUNK):
        cols = slice(c * N_CHUNK, (c + 1) * N_CHUNK)
        out = jnp.dot(y, w_ref[:, cols], preferred_element_type=F32)
        o_ref[0, :, cols] = x_ref[0, :, cols] + mod_ref[0, 2:3, cols] * out


def _out_call(y, w_out, xs, mod, *, tm, y_tile0, ctx_stream, name):
    b, rows, d = xs.shape
    return pl.pallas_call(
        _out_kernel,
        grid=(b, rows // tm),
        in_specs=[pl.BlockSpec((1, tm, d), lambda bi, i: (bi, y_tile0 + i, 0)),
                  _resident(w_out.shape, lambda bi, i: (0, 0)),
                  pl.BlockSpec((1, tm, d), lambda bi, i: (bi, i, 0)),
                  pl.BlockSpec((1, 3, d), lambda bi, i: (b if ctx_stream else bi, 0, 0))],
        out_specs=pl.BlockSpec((1, tm, d), lambda bi, i: (bi, i, 0)),
        out_shape=jax.ShapeDtypeStruct((b, rows, d), F32),
        compiler_params=_params(("parallel", "parallel")),
        name=name,
    )(y, w_out, xs, mod)


def _rope_tables(n_lat, n_ctx, rot_dim):
    axis_dim = rot_dim // 2
    t = jnp.arange(n_lat, dtype=jnp.int32)
    pos_row = (t // GRID_W).astype(F32)
    pos_col = (t % GRID_W).astype(F32)
    inv_freq = ROPE_THETA ** (-jnp.arange(0, axis_dim, 2, dtype=F32) / axis_dim)
    ang_r = pos_row[:, None] * inv_freq
    ang_c = pos_col[:, None] * inv_freq
    ang = jnp.concatenate([ang_r, ang_r, ang_c, ang_c], axis=-1)
    cos, sin = jnp.cos(ang), jnp.sin(ang)
    lane = jnp.arange(rot_dim)
    sign = jnp.where((lane % axis_dim) < axis_dim // 2, -1.0, 1.0).astype(F32)
    reps = LANES // rot_dim
    cos = jnp.tile(cos, (1, reps))
    sin = jnp.tile(sin * sign, (1, reps))
    cos = jnp.concatenate([cos, jnp.ones((n_ctx, LANES), F32)], axis=0)
    sin = jnp.concatenate([sin, jnp.zeros((n_ctx, LANES), F32)], axis=0)
    return cos, sin


def kernel(x, c, ctx, c_ctx, norm_w, w_ada, b_ada, w_in, b_merge, gqa_q_norm, gqa_k_norm, mla_q_nope_norm, mla_q_rope_norm, mla_kv_norm, mla_w_uk, mla_w_uv, mla_k_nope_norm, mla_k_rope_norm, diff_q_norm, diff_k_norm, diff_lambda_q1, diff_lambda_k1, diff_lambda_q2, diff_lambda_k2, diff_subln, w_br_gqa, w_br_mla, w_br_diff, w_out):
    b, n_lat, d = x.shape
    n_ctx = ctx.shape[1]
    n = n_lat + n_ctx
    depth = w_in.shape[0]
    assert n_lat % 2048 == 0 and n_ctx == 256 and n % 8 == 0 and b < 8

    c_all = jnp.concatenate([c, c_ctx[None], jnp.zeros((8 - b - 1, d), F32)], axis=0)
    mod = _ada_call(c_all, w_ada, b_ada).reshape(depth, 8, 3, d)

    o_aq = GQA_HEADS * HEAD_DIM + 2 * GQA_KV_HEADS * HEAD_DIM
    o_mq = o_aq + MLA_HEADS * (MLA_NOPE + MLA_ROPE)
    o_ckv = o_mq + MLA_KV_RANK
    o_kr = o_ckv + MLA_ROPE
    o_dend = o_kr + 3 * DIFF_HEADS * 2 * DIFF_QK
    w_a = w_in[:, :, :o_aq].astype(BF16)
    mq = w_in[:, :, o_aq:o_mq].reshape(depth, d, MLA_HEADS, MLA_NOPE + MLA_ROPE)
    mq = jnp.pad(mq, ((0, 0), (0, 0), (0, 0), (0, MLA_QK_PAD - MLA_NOPE - MLA_ROPE)))
    w_b = jnp.concatenate(
        [mq.reshape(depth, d, MLA_HEADS * MLA_QK_PAD), w_in[:, :, o_mq:o_ckv],
         jnp.pad(w_in[:, :, o_ckv:o_kr], ((0, 0), (0, 0), (0, LANES - MLA_ROPE)))], axis=-1).astype(BF16)
    w_c = w_in[:, :, o_kr:o_dend].astype(BF16)
    o_gend = o_dend + 3 * BRANCH_WIDTH
    w_g = jnp.concatenate([w_in[:, :, o_gend:], w_in[:, :, o_dend:o_gend]], axis=-1).astype(BF16)
    w_ukv = jnp.concatenate([mla_w_uk, mla_w_uv], axis=-1).astype(BF16)
    w_br = jnp.stack([w_br_gqa, w_br_mla, w_br_diff], axis=1).astype(BF16)
    w_o = w_out.astype(BF16)
    bias_g = jnp.concatenate([b_merge, jnp.zeros((depth, 3 * BRANCH_WIDTH), F32)], axis=-1)
    g_bq = jnp.concatenate([mla_q_nope_norm, mla_q_rope_norm,
                            jnp.zeros((depth, MLA_QK_PAD - MLA_NOPE - MLA_ROPE), F32)], axis=-1)
    g_kr = jnp.pad(mla_k_rope_norm, ((0, 0), (0, LANES - MLA_ROPE)))
    g_dq = jnp.tile(diff_q_norm, (1, 2))
    g_dk = jnp.tile(diff_k_norm, (1, 2))
    lamv = jnp.stack([diff_lambda_q1, diff_lambda_k1, diff_lambda_q2, diff_lambda_k2], axis=1)

    cos128, sin128 = _rope_tables(n_lat, n_ctx, HEAD_DIM)
    cos64, sin64 = _rope_tables(n_lat, n_ctx, MLA_ROPE)

    lat_chunks = tuple((s, 1024) for s in range(0, n_lat, 1024)) + ((n_lat, n_ctx),)
    assert sum(size for _, size in lat_chunks) == n
    ctx_chunks = ((0, n_ctx),)
    ctx_tile = n_lat // n_ctx

    for l in range(depth):
        last = l == depth - 1
        lam_init = 0.8 - 0.6 * math.exp(-0.3 * l)
        h = _h_call(x, norm_w[l][None], mod[l], None, n=n, tile0=0, ctx_stream=False, name="norm_mod")
        h = _h_call(ctx, norm_w[l][None], mod[l], h, n=n, tile0=ctx_tile, ctx_stream=True,
                    name="norm_mod_ctx")

        qa, ka, va = _proj_a_call(h, w_a[l], gqa_q_norm[l][None], gqa_k_norm[l][None], cos128, sin128)
        qb, ckv, kr = _proj_b_call(h, w_b[l], g_bq[l][None], mla_kv_norm[l][None], g_kr[l][None],
                                   cos64, sin64)
        qc, kc, vc = _proj_c_call(h, w_c[l], g_dq[l][None], g_dk[l][None], cos64, sin64)
        gm = _gate_call(h, w_g[l], bias_g[l][None])
        kb, vb = _mla_up_call(ckv, kr, w_ukv[l], mla_k_nope_norm[l][None])

        qa = qa.reshape(b, GQA_KV_HEADS, GQA_GROUP, n, HEAD_DIM)
        qb = qb.reshape(b, MLA_HEADS, 1, n, MLA_QK_PAD)
        rows = n_lat if last else n
        diff_kw = dict(diff=True, lamv=lamv[l][None], sub=diff_subln[l][None], lam_init=lam_init)

        ba = _score_bound([(gqa_q_norm[l], HEAD_DIM)], [(gqa_k_norm[l], HEAD_DIM)], HEAD_DIM ** -0.5)
        bb = _score_bound([(mla_q_nope_norm[l], MLA_NOPE), (mla_q_rope_norm[l], MLA_ROPE)],
                          [(mla_k_nope_norm[l], MLA_NOPE), (mla_k_rope_norm[l], MLA_ROPE)],
                          (MLA_NOPE + MLA_ROPE) ** -0.5)
        bc = _score_bound([(diff_q_norm[l], DIFF_QK)], [(diff_k_norm[l], DIFF_QK)], DIFF_QK ** -0.5)
        vta = _vt_call(va, "vt_gqa")
        vtb = _vt_call(vb, "vt_mla")
        vtc = _vt_call(vc, "vt_diff")
        oa = _flash_call(ba, qa, ka, va, vta, None, tq=256, q_tile0=0, n_q_tiles=n_lat // 256,
                         kv_tile0=0, chunks=lat_chunks, n_tokens=rows, name="flash_gqa")
        ob = _flash_call(bb, qb, kb, vb, vtb, None, tq=1024, q_tile0=0, n_q_tiles=n_lat // 1024,
                         kv_tile0=0, chunks=lat_chunks, n_tokens=rows, name="flash_mla")
        oc = _flash_call(bc, qc, kc, vc, vtc, None, tq=512, q_tile0=0, n_q_tiles=n_lat // 512,
                         kv_tile0=0, chunks=lat_chunks, n_tokens=rows, name="flash_diff", **diff_kw)
        if not last:
            ctx_kw = dict(tq=n_ctx, q_tile0=ctx_tile, n_q_tiles=1, kv_tile0=ctx_tile,
                          chunks=ctx_chunks, n_tokens=rows)
            oa = _flash_call(ba, qa, ka, va, vta, oa, name="flash_gqa_ctx", **ctx_kw)
            ob = _flash_call(bb, qb, kb, vb, vtb, ob, name="flash_mla_ctx", **ctx_kw)
            oc = _flash_call(bc, qc, kc, vc, vtc, oc, name="flash_diff_ctx", **ctx_kw, **diff_kw)

        y = _merge_call(oa, ob, oc, gm, w_br[l], rows, rows // 8)
        if not last:
            ctx = _out_call(y, w_o[l], ctx, mod[l], tm=n_ctx, y_tile0=ctx_tile, ctx_stream=True,
                            name="out_proj_ctx")
        x = _out_call(y, w_o[l], x, mod[l], tm=512, y_tile0=0, ctx_stream=False, name="out_proj")
    return x
```

```python
import functools
import math

import jax
import jax.numpy as jnp
from jax import lax
from jax.experimental import pallas as pl
from jax.experimental.pallas import tpu as pltpu

F32 = jnp.float32
BF16 = jnp.bfloat16

GRID_W = 64
ROPE_THETA = 10000.0
EPS = 1e-6

HEAD_DIM = 128
BRANCH_WIDTH = 1024
GQA_HEADS = 8
GQA_KV_HEADS = 2
GQA_GROUP = GQA_HEADS // GQA_KV_HEADS
MLA_HEADS = 8
MLA_NOPE = 128
MLA_ROPE = 64
MLA_QK_PAD = 256
MLA_KV_RANK = 512
DIFF_HEADS = 8
DIFF_QK = 64

LANES = 128
MXU_N = 256
VMEM_LIMIT = 56 * 1024 * 1024
LOG2E = 1.4426950408889634
SAFE_BOUND_LOG2 = 57.0


def _params(sem, vmem=VMEM_LIMIT):
    return pltpu.CompilerParams(dimension_semantics=sem, vmem_limit_bytes=vmem)


def _resident(shape, index_map):
    return pl.BlockSpec(shape, index_map, pipeline_mode=pl.Buffered(1))


def _rms(y, gain, n):
    ms = jnp.sum(y * y, axis=-1, keepdims=True) * (1.0 / n)
    return y * lax.rsqrt(ms + EPS) * gain


def _seg_rms(y, gain, lo):
    ss = y * y
    s_lo = jnp.sum(jnp.where(lo, ss, 0.0), axis=-1, keepdims=True)
    s_hi = jnp.sum(jnp.where(lo, 0.0, ss), axis=-1, keepdims=True)
    ms = jnp.where(lo, s_lo, s_hi) * (1.0 / DIFF_QK)
    return y * lax.rsqrt(ms + EPS) * gain


def _rope(y, cos, sin_signed, half):
    lane = lax.broadcasted_iota(jnp.int32, (1, LANES), 1)
    lower = (lane & (2 * half - 1)) < half
    partner = jnp.where(lower, pltpu.roll(y, LANES - half, 1), pltpu.roll(y, half, 1))
    return y * cos + partner * sin_signed


def _sigmoid(z):
    return 1.0 / (1.0 + jnp.exp(-z))


def _ada_kernel(c_ref, w_ref, b_ref, o_ref):
    c = c_ref[...]
    a = (c * _sigmoid(c)).astype(BF16)
    o_ref[0] = jnp.dot(a, w_ref[0].astype(BF16), preferred_element_type=F32) + b_ref[0]


def _ada_call(c_all, w_ada, b_ada):
    depth, d, n3 = w_ada.shape
    tn = 512
    return pl.pallas_call(
        _ada_kernel,
        grid=(depth, n3 // tn),
        in_specs=[pl.BlockSpec((8, d), lambda l, j: (0, 0)),
                  pl.BlockSpec((1, d, tn), lambda l, j: (l, 0, j)),
                  pl.BlockSpec((1, 1, tn), lambda l, j: (l, 0, j))],
        out_specs=pl.BlockSpec((1, 8, tn), lambda l, j: (l, 0, j)),
        out_shape=jax.ShapeDtypeStruct((depth, 8, n3), F32),
        compiler_params=_params(("parallel", "parallel")),
        name="ada_mod",
    )(c_all, w_ada, b_ada.reshape(depth, 1, n3))


def _h_kernel(x_ref, ctx_ref, nw_ref, mod_ref, modc_ref, h_ref, *, lat_tiles):
    def norm_mod(src_ref, m_ref):
        x = src_ref[0]
        ms = jnp.mean(x * x, axis=-1, keepdims=True)
        y = x * lax.rsqrt(ms + EPS) * nw_ref[...]
        h_ref[0] = (y * (1.0 + m_ref[0, 1:2, :]) + m_ref[0, 0:1, :]).astype(BF16)

    t = pl.program_id(1)

    @pl.when(t < lat_tiles)
    def _():
        norm_mod(x_ref, mod_ref)

    @pl.when(t >= lat_tiles)
    def _():
        norm_mod(ctx_ref, modc_ref)


def _h_call(x, ctx, norm_w, mod):
    b, n_lat, d = x.shape
    tr = ctx.shape[1]
    lat_tiles = n_lat // tr
    return pl.pallas_call(
        functools.partial(_h_kernel, lat_tiles=lat_tiles),
        grid=(b, lat_tiles + 1),
        in_specs=[pl.BlockSpec((1, tr, d), lambda bi, t: (bi, jnp.minimum(t, lat_tiles - 1), 0)),
                  pl.BlockSpec((1, tr, d), lambda bi, t: (bi, 0, 0)),
                  pl.BlockSpec((1, d), lambda bi, t: (0, 0)),
                  pl.BlockSpec((1, 3, d), lambda bi, t: (bi, 0, 0)),
                  pl.BlockSpec((1, 3, d), lambda bi, t: (b, 0, 0))],
        out_specs=pl.BlockSpec((1, tr, d), lambda bi, t: (bi, t, 0)),
        out_shape=jax.ShapeDtypeStruct((b, n_lat + tr, d), BF16),
        compiler_params=_params(("parallel", "parallel")),
        name="norm_mod",
    )(x, ctx, norm_w, mod, mod)


def _proj_a_kernel(h_ref, w_ref, gq_ref, gk_ref, cos_ref, sin_ref, q_ref, k_ref, v_ref):
    h = h_ref[0]
    cos = cos_ref[...]
    sin = sin_ref[...]
    scale = HEAD_DIM ** -0.5 * LOG2E
    n_q = GQA_HEADS * HEAD_DIM // MXU_N
    for c in range(n_q + 2):
        y = jnp.dot(h, w_ref[:, c * MXU_N:(c + 1) * MXU_N], preferred_element_type=F32)
        for u in range(2):
            yu = y[:, u * LANES:(u + 1) * LANES]
            if c < n_q:
                z = _rope(_rms(yu, gq_ref[...], HEAD_DIM), cos, sin, HEAD_DIM // 4)
                q_ref[0, 2 * c + u] = (z * scale).astype(BF16)
            elif c == n_q:
                z = _rope(_rms(yu, gk_ref[...], HEAD_DIM), cos, sin, HEAD_DIM // 4)
                k_ref[0, u] = z.astype(BF16)
            else:
                v_ref[0, u] = yu.astype(BF16)


def _proj_b_kernel(h_ref, w_ref, gq_ref, gckv_ref, gkr_ref, cos_ref, sin_ref,
                   q_ref, ckv_ref, kr_ref):
    h = h_ref[0]
    cos = cos_ref[...]
    sin = sin_ref[...]
    scale = (MLA_NOPE + MLA_ROPE) ** -0.5 * LOG2E
    for hd in range(MLA_HEADS):
        y = jnp.dot(h, w_ref[:, hd * MLA_QK_PAD:(hd + 1) * MLA_QK_PAD], preferred_element_type=F32)
        nope = _rms(y[:, :LANES], gq_ref[:, :LANES], MLA_NOPE)
        rope = _rope(_rms(y[:, LANES:], gq_ref[:, LANES:], MLA_ROPE), cos, sin, MLA_ROPE // 4)
        q_ref[0, hd, :, 0:LANES] = (nope * scale).astype(BF16)
        q_ref[0, hd, :, LANES:2 * LANES] = (rope * scale).astype(BF16)
    base = MLA_HEADS * MLA_QK_PAD
    y0 = jnp.dot(h, w_ref[:, base:base + MXU_N], preferred_element_type=F32)
    y1 = jnp.dot(h, w_ref[:, base + MXU_N:base + 2 * MXU_N], preferred_element_type=F32)
    ms = (jnp.sum(y0 * y0, axis=-1, keepdims=True)
          + jnp.sum(y1 * y1, axis=-1, keepdims=True)) * (1.0 / MLA_KV_RANK)
    inv = lax.rsqrt(ms + EPS)
    ckv_ref[0, :, 0:MXU_N] = (y0 * inv * gckv_ref[:, 0:MXU_N]).astype(BF16)
    ckv_ref[0, :, MXU_N:2 * MXU_N] = (y1 * inv * gckv_ref[:, MXU_N:2 * MXU_N]).astype(BF16)
    base += MLA_KV_RANK
    yk = jnp.dot(h, w_ref[:, base:base + LANES], preferred_element_type=F32)
    kr = _rope(_rms(yk, gkr_ref[...], MLA_ROPE), cos, sin, MLA_ROPE // 4)
    kr_ref[0] = kr.astype(BF16)


def _proj_c_kernel(h_ref, w_ref, gq_ref, gk_ref, cos_ref, sin_ref, q_ref, k_ref, v_ref):
    h = h_ref[0]
    cos = cos_ref[...]
    sin = sin_ref[...]
    scale = DIFF_QK ** -0.5 * LOG2E
    lo = lax.broadcasted_iota(jnp.int32, (1, LANES), 1) < DIFF_QK
    per = DIFF_HEADS // 2
    for c in range(3 * per):
        y = jnp.dot(h, w_ref[:, c * MXU_N:(c + 1) * MXU_N], preferred_element_type=F32)
        for u in range(2):
            yu = y[:, u * LANES:(u + 1) * LANES]
            if c < per:
                z = _rope(_seg_rms(yu, gq_ref[...], lo), cos, sin, DIFF_QK // 4) * scale
                q_ref[0, 2 * c + u, 0] = jnp.where(lo, z, 0.0).astype(BF16)
                q_ref[0, 2 * c + u, 1] = jnp.where(lo, 0.0, z).astype(BF16)
            elif c < 2 * per:
                z = _rope(_seg_rms(yu, gk_ref[...], lo), cos, sin, DIFF_QK // 4)
                k_ref[0, 2 * (c - per) + u] = z.astype(BF16)
            else:
                v_ref[0, 2 * (c - 2 * per) + u] = yu.astype(BF16)


def _proj_tm(n):
    return n // 8


def _proj_a_call(h, w, gq, gk, cos, sin):
    b, n, d = h.shape
    tm = _proj_tm(n)
    row = lambda bi, i: (bi, i, 0)
    head = lambda bi, i: (bi, 0, i, 0)
    const = lambda bi, i: (0, 0)
    return pl.pallas_call(
        _proj_a_kernel,
        grid=(b, n // tm),
        in_specs=[pl.BlockSpec((1, tm, d), row),
                  _resident(w.shape, const),
                  pl.BlockSpec((1, LANES), const), pl.BlockSpec((1, LANES), const),
                  pl.BlockSpec((tm, LANES), lambda bi, i: (i, 0)),
                  pl.BlockSpec((tm, LANES), lambda bi, i: (i, 0))],
        out_specs=[pl.BlockSpec((1, GQA_HEADS, tm, HEAD_DIM), head),
                   pl.BlockSpec((1, GQA_KV_HEADS, tm, HEAD_DIM), head),
                   pl.BlockSpec((1, GQA_KV_HEADS, tm, HEAD_DIM), head)],
        out_shape=[jax.ShapeDtypeStruct((b, GQA_HEADS, n, HEAD_DIM), BF16),
                   jax.ShapeDtypeStruct((b, GQA_KV_HEADS, n, HEAD_DIM), BF16),
                   jax.ShapeDtypeStruct((b, GQA_KV_HEADS, n, HEAD_DIM), BF16)],
        compiler_params=_params(("parallel", "parallel")),
        name="proj_gqa",
    )(h, w, gq, gk, cos, sin)


def _proj_b_call(h, w, gq, gckv, gkr, cos, sin):
    b, n, d = h.shape
    tm = _proj_tm(n)
    row = lambda bi, i: (bi, i, 0)
    const = lambda bi, i: (0, 0)
    return pl.pallas_call(
        _proj_b_kernel,
        grid=(b, n // tm),
        in_specs=[pl.BlockSpec((1, tm, d), row),
                  _resident(w.shape, const),
                  pl.BlockSpec((1, MLA_QK_PAD), const),
                  pl.BlockSpec((1, MLA_KV_RANK), const),
                  pl.BlockSpec((1, LANES), const),
                  pl.BlockSpec((tm, LANES), lambda bi, i: (i, 0)),
                  pl.BlockSpec((tm, LANES), lambda bi, i: (i, 0))],
        out_specs=[pl.BlockSpec((1, MLA_HEADS, tm, MLA_QK_PAD), lambda bi, i: (bi, 0, i, 0)),
                   pl.BlockSpec((1, tm, MLA_KV_RANK), row),
                   pl.BlockSpec((1, tm, LANES), row)],
        out_shape=[jax.ShapeDtypeStruct((b, MLA_HEADS, n, MLA_QK_PAD), BF16),
                   jax.ShapeDtypeStruct((b, n, MLA_KV_RANK), BF16),
                   jax.ShapeDtypeStruct((b, n, LANES), BF16)],
        compiler_params=_params(("parallel", "parallel")),
        name="proj_mla",
    )(h, w, gq, gckv, gkr, cos, sin)


def _proj_c_call(h, w, gq, gk, cos, sin):
    b, n, d = h.shape
    tm = _proj_tm(n)
    row = lambda bi, i: (bi, i, 0)
    const = lambda bi, i: (0, 0)
    return pl.pallas_call(
        _proj_c_kernel,
        grid=(b, n // tm),
        in_specs=[pl.BlockSpec((1, tm, d), row),
                  _resident(w.shape, const),
                  pl.BlockSpec((1, LANES), const), pl.BlockSpec((1, LANES), const),
                  pl.BlockSpec((tm, LANES), lambda bi, i: (i, 0)),
                  pl.BlockSpec((tm, LANES), lambda bi, i: (i, 0))],
        out_specs=[pl.BlockSpec((1, DIFF_HEADS, 2, tm, LANES), lambda bi, i: (bi, 0, 0, i, 0)),
                   pl.BlockSpec((1, DIFF_HEADS, tm, LANES), lambda bi, i: (bi, 0, i, 0)),
                   pl.BlockSpec((1, DIFF_HEADS, tm, LANES), lambda bi, i: (bi, 0, i, 0))],
        out_shape=[jax.ShapeDtypeStruct((b, DIFF_HEADS, 2, n, LANES), BF16),
                   jax.ShapeDtypeStruct((b, DIFF_HEADS, n, LANES), BF16),
                   jax.ShapeDtypeStruct((b, DIFF_HEADS, n, LANES), BF16)],
        compiler_params=_params(("parallel", "parallel")),
        name="proj_diff",
    )(h, w, gq, gk, cos, sin)


def _mla_up_kernel(ckv_ref, kr_ref, w_ref, gk_ref, k_ref, v_ref):
    ckv = ckv_ref[0]
    kr = kr_ref[0]
    per = MLA_HEADS // 2
    for c in range(2 * per):
        y = jnp.dot(ckv, w_ref[:, c * MXU_N:(c + 1) * MXU_N], preferred_element_type=F32)
        for u in range(2):
            yu = y[:, u * LANES:(u + 1) * LANES]
            if c < per:
                k_ref[0, 2 * c + u, :, 0:LANES] = _rms(yu, gk_ref[...], MLA_NOPE).astype(BF16)
                k_ref[0, 2 * c + u, :, LANES:2 * LANES] = kr
            else:
                v_ref[0, 2 * (c - per) + u] = yu.astype(BF16)


def _mla_up_call(ckv, kr, w, gk):
    b, n, r = ckv.shape
    tm = _proj_tm(n)
    row = lambda bi, i: (bi, i, 0)
    const = lambda bi, i: (0, 0)
    return pl.pallas_call(
        _mla_up_kernel,
        grid=(b, n // tm),
        in_specs=[pl.BlockSpec((1, tm, r), row),
                  pl.BlockSpec((1, tm, LANES), row),
                  _resident(w.shape, const),
                  pl.BlockSpec((1, LANES), const)],
        out_specs=[pl.BlockSpec((1, MLA_HEADS, tm, MLA_QK_PAD), lambda bi, i: (bi, 0, i, 0)),
                   pl.BlockSpec((1, MLA_HEADS, tm, LANES), lambda bi, i: (bi, 0, i, 0))],
        out_shape=[jax.ShapeDtypeStruct((b, MLA_HEADS, n, MLA_QK_PAD), BF16),
                   jax.ShapeDtypeStruct((b, MLA_HEADS, n, LANES), BF16)],
        compiler_params=_params(("parallel", "parallel")),
        name="mla_up",
    )(ckv, kr, w, gk)


def _gate_kernel(h_ref, w_ref, b_ref, o_ref, *, n_sigmoid_tiles):
    is_sigmoid = pl.program_id(2) < n_sigmoid_tiles
    h = h_ref[0]
    for c in range(o_ref.shape[-1] // MXU_N):
        cols = slice(c * MXU_N, (c + 1) * MXU_N)
        y = jnp.dot(h, w_ref[:, cols], preferred_element_type=F32)
        t = _sigmoid(y + b_ref[:, cols])
        o_ref[0, :, cols] = jnp.where(is_sigmoid, t, y * t).astype(BF16)


def _gate_call(h, w, bias):
    b, n, d = h.shape
    cols = w.shape[1]
    tm = n // 4
    tn = 3072
    assert (cols - 3 * BRANCH_WIDTH) % tn == 0 and cols % tn == 0
    return pl.pallas_call(
        functools.partial(_gate_kernel, n_sigmoid_tiles=(cols - 3 * BRANCH_WIDTH) // tn),
        grid=(b, n // tm, cols // tn),
        in_specs=[pl.BlockSpec((1, tm, d), lambda bi, i, j: (bi, i, 0)),
                  pl.BlockSpec((d, tn), lambda bi, i, j: (0, j)),
                  pl.BlockSpec((1, tn), lambda bi, i, j: (0, j))],
        out_specs=pl.BlockSpec((1, tm, tn), lambda bi, i, j: (bi, i, j)),
        out_shape=jax.ShapeDtypeStruct((b, n, cols), BF16),
        compiler_params=_params(("parallel", "parallel", "parallel")),
        name="gate_proj",
    )(h, w, bias)


def _flash_kernel(*refs, group, tq, chunks, diff, lam_init):
    if diff:
        bound_ref, q_ref, k_ref, v_ref, vt_ref, lamv_ref, sub_ref, o_ref = refs
    else:
        bound_ref, q_ref, k_ref, v_ref, vt_ref, o_ref = refs
    dq = q_ref.shape[-1]
    nt = (((1,), (1,)), ((), ()))

    q = q_ref[0, 0].reshape(group * tq, dq)
    half = group * tq // 2
    halves = (slice(0, half), slice(half, 2 * half))
    bound = bound_ref[0]
    safe = bound <= SAFE_BOUND_LOG2

    def finish(out):
        if diff:
            lv = lamv_ref[0]
            lam = (jnp.exp(jnp.sum(lv[0:1] * lv[1:2], axis=-1, keepdims=True))
                   - jnp.exp(jnp.sum(lv[2:3] * lv[3:4], axis=-1, keepdims=True)) + lam_init)
            o = out[:tq] - lam * out[tq:]
            o_ref[0] = (_rms(o, sub_ref[...], LANES) * (1.0 - lam_init)).astype(BF16)
        else:
            for g in range(group):
                o_ref[0, :, g * LANES:(g + 1) * LANES] = out[g * tq:(g + 1) * tq].astype(BF16)

    @pl.when(safe)
    def _():
        accs = [None, None]

        def k_dot_q(start, size):
            k = k_ref[0, 0, start:start + size, :]
            return [lax.dot_general(k, q[rows], nt, preferred_element_type=F32) for rows in halves]

        def vt_dot_pt(pts, start, size):
            vt = vt_ref[0, 0, :, start:start + size]
            for i, pt in enumerate(pts):
                t = jnp.dot(vt, pt, preferred_element_type=F32)
                accs[i] = t if accs[i] is None else accs[i] + t

        pending = None
        for start, size in chunks:
            ss = k_dot_q(start, size)
            if pending is not None:
                vt_dot_pt(*pending)
            pending = ([jnp.exp2(s - bound).astype(BF16) for s in ss], start, size)
        vt_dot_pt(*pending)
        acc = jnp.concatenate(accs, axis=1)
        finish((acc[:LANES] / acc[LANES:LANES + 1]).T)

    @pl.when(jnp.logical_not(safe))
    def _():
        m = l = acc = None
        for start, size in chunks:
            k = k_ref[0, 0, start:start + size, :]
            v = v_ref[0, 0, start:start + size, :]
            s = lax.dot_general(q, k, nt, preferred_element_type=F32)
            m_cur = jnp.max(s, axis=-1, keepdims=True)
            if m is None:
                m = m_cur
                p = jnp.exp2(s - m)
                l = jnp.sum(p, axis=-1, keepdims=True)
                acc = jnp.dot(p.astype(BF16), v, preferred_element_type=F32)
            else:
                m_new = jnp.maximum(m, m_cur)
                alpha = jnp.exp2(m - m_new)
                p = jnp.exp2(s - m_new)
                l = alpha * l + jnp.sum(p, axis=-1, keepdims=True)
                acc = alpha * acc + jnp.dot(p.astype(BF16), v, preferred_element_type=F32)
                m = m_new
        finish(acc / l)


VT_ROWS = LANES + 16


def _vt_kernel(v_ref, o_ref):
    n = v_ref.shape[2]
    for j in range(n // LANES):
        cols = slice(j * LANES, (j + 1) * LANES)
        o_ref[0, 0, 0:LANES, cols] = v_ref[0, 0, cols, :].astype(F32).T.astype(BF16)
    o_ref[0, 0, LANES:VT_ROWS, :] = jnp.ones((VT_ROWS - LANES, n), BF16)


def _vt_call(v, name):
    b, heads, n, dv = v.shape
    return pl.pallas_call(
        _vt_kernel,
        grid=(b, heads),
        in_specs=[pl.BlockSpec((1, 1, n, dv), lambda bi, h: (bi, h, 0, 0))],
        out_specs=pl.BlockSpec((1, 1, VT_ROWS, n), lambda bi, h: (bi, h, 0, 0)),
        out_shape=jax.ShapeDtypeStruct((b, heads, VT_ROWS, n), BF16),
        compiler_params=_params(("parallel", "parallel")),
        name=name,
    )(v)


def _flash_call(bound, q, k, v, vt, *, tq, q_tile0, n_q_tiles, kv_tile0, chunks,
                diff=False, lamv=None, sub=None, lam_init=0.0, name="flash"):
    b, heads, group, n, dq = q.shape
    kv_len = sum(size for _, size in chunks)
    out_w = LANES if diff else group * LANES
    in_specs = [pl.BlockSpec(memory_space=pltpu.SMEM),
                pl.BlockSpec((1, 1, group, tq, dq), lambda bi, h, i: (bi, h, 0, q_tile0 + i, 0)),
                pl.BlockSpec((1, 1, kv_len, dq), lambda bi, h, i: (bi, h, kv_tile0, 0)),
                pl.BlockSpec((1, 1, kv_len, LANES), lambda bi, h, i: (bi, h, kv_tile0, 0)),
                pl.BlockSpec((1, 1, VT_ROWS, kv_len), lambda bi, h, i: (bi, h, 0, kv_tile0))]
    args = [bound, q, k, v, vt]
    if diff:
        in_specs += [pl.BlockSpec((1, 4, DIFF_QK), lambda bi, h, i: (0, 0, 0)),
                     pl.BlockSpec((1, LANES), lambda bi, h, i: (0, 0))]
        args += [lamv, sub]
    kern = functools.partial(_flash_kernel, group=group, tq=tq, chunks=chunks, diff=diff,
                             lam_init=lam_init)
    return pl.pallas_call(
        kern,
        grid=(b, heads, n_q_tiles),
        in_specs=in_specs,
        out_specs=pl.BlockSpec((1, tq, out_w), lambda bi, h, i: (bi, i, h)),
        out_shape=jax.ShapeDtypeStruct((b, n_q_tiles * tq, BRANCH_WIDTH), BF16),
        compiler_params=_params(("parallel", "parallel", "parallel")),
        name=name,
    )(*args)


def _score_bound(q_terms, k_terms, scale):
    q2 = sum(d * jnp.max(jnp.square(g.astype(F32))) for g, d in q_terms)
    k2 = sum(d * jnp.max(jnp.square(g.astype(F32))) for g, d in k_terms)
    return (1.01 * scale * LOG2E * jnp.sqrt(q2 * k2)).reshape(1)


N_CHUNK = 512


def _merge_kernel(oa_ref, ob_ref, oc_ref, g_ref, ma_ref, mb_ref, mc_ref, w_ref, y_ref):
    ogs = [o_ref[0] * g_ref[0, :, r * BRANCH_WIDTH:(r + 1) * BRANCH_WIDTH]
           for r, o_ref in enumerate((oa_ref, ob_ref, oc_ref))]
    for c in range(y_ref.shape[-1] // N_CHUNK):
        cols = slice(c * N_CHUNK, (c + 1) * N_CHUNK)
        acc = None
        for r, m_ref in enumerate((ma_ref, mb_ref, mc_ref)):
            z = jnp.dot(ogs[r], w_ref[r, :, cols], preferred_element_type=F32)
            t = m_ref[0, :, cols].astype(F32) * z
            acc = t if acc is None else acc + t
        y_ref[0, :, cols] = acc.astype(BF16)


def _merge_call(oa, ob, oc, gm, w_br, *, tm, gm_tile0, name):
    b, rows, _ = oa.shape
    d = w_br.shape[-1]
    gate_w = 3 * BRANCH_WIDTH
    assert (3 * d) % gate_w == 0
    o_spec = pl.BlockSpec((1, tm, BRANCH_WIDTH), lambda bi, i: (bi, i, 0))
    m_specs = [pl.BlockSpec((1, tm, d), lambda bi, i, r=r: (bi, gm_tile0 + i, r)) for r in range(3)]
    return pl.pallas_call(
        _merge_kernel,
        grid=(b, rows // tm),
        in_specs=[o_spec, o_spec, o_spec,
                  pl.BlockSpec((1, tm, gate_w), lambda bi, i: (bi, gm_tile0 + i, 3 * d // gate_w)),
                  *m_specs,
                  _resident(w_br.shape, lambda bi, i: (0, 0, 0))],
        out_specs=pl.BlockSpec((1, tm, d), lambda bi, i: (bi, i, 0)),
        out_shape=jax.ShapeDtypeStruct((b, rows, d), BF16),
        compiler_params=_params(("parallel", "parallel")),
        name=name,
    )(oa, ob, oc, gm, gm, gm, gm, w_br)


def _out_kernel(y_ref, w_ref, x_ref, mod_ref, o_ref):
    y = y_ref[0]
    for c in range(o_ref.shape[-1] // N_CHUNK):
        cols = slice(c * N_CHUNK, (c + 1) * N_CHUNK)
        out = jnp.dot(y, w_ref[:, cols], preferred_element_type=F32)
        o_ref[0, :, cols] = x_ref[0, :, cols] + mod_ref[0, 2:3, cols] * out


def _out_call(y, w_out, xs, mod, *, tm, ctx_stream, name):
    b, rows, d = xs.shape
    return pl.pallas_call(
        _out_kernel,
        grid=(b, rows // tm),
        in_specs=[pl.BlockSpec((1, tm, d), lambda bi, i: (bi, i, 0)),
                  _resident(w_out.shape, lambda bi, i: (0, 0)),
                  pl.BlockSpec((1, tm, d), lambda bi, i: (bi, i, 0)),
                  pl.BlockSpec((1, 3, d), lambda bi, i: (b if ctx_stream else bi, 0, 0))],
        out_specs=pl.BlockSpec((1, tm, d), lambda bi, i: (bi, i, 0)),
        out_shape=jax.ShapeDtypeStruct((b, rows, d), F32),
        compiler_params=_params(("parallel", "parallel")),
        name=name,
    )(y, w_out, xs, mod)


def _rope_tables(n_lat, n_ctx, rot_dim):
    axis_dim = rot_dim // 2
    t = jnp.arange(n_lat, dtype=jnp.int32)
    pos_row = (t // GRID_W).astype(F32)
    pos_col = (t % GRID_W).astype(F32)
    inv_freq = ROPE_THETA ** (-jnp.arange(0, axis_dim, 2, dtype=F32) / axis_dim)
    ang_r = pos_row[:, None] * inv_freq
    ang_c = pos_col[:, None] * inv_freq
    ang = jnp.concatenate([ang_r, ang_r, ang_c, ang_c], axis=-1)
    cos, sin = jnp.cos(ang), jnp.sin(ang)
    lane = jnp.arange(rot_dim)
    sign = jnp.where((lane % axis_dim) < axis_dim // 2, -1.0, 1.0).astype(F32)
    reps = LANES // rot_dim
    cos = jnp.tile(cos, (1, reps))
    sin = jnp.tile(sin * sign, (1, reps))
    cos = jnp.concatenate([cos, jnp.ones((n_ctx, LANES), F32)], axis=0)
    sin = jnp.concatenate([sin, jnp.zeros((n_ctx, LANES), F32)], axis=0)
    return cos, sin


def kernel(x, c, ctx, c_ctx, norm_w, w_ada, b_ada, w_in, b_merge, gqa_q_norm, gqa_k_norm, mla_q_nope_norm, mla_q_rope_norm, mla_kv_norm, mla_w_uk, mla_w_uv, mla_k_nope_norm, mla_k_rope_norm, diff_q_norm, diff_k_norm, diff_lambda_q1, diff_lambda_k1, diff_lambda_q2, diff_lambda_k2, diff_subln, w_br_gqa, w_br_mla, w_br_diff, w_out):
    b, n_lat, d = x.shape
    n_ctx = ctx.shape[1]
    n = n_lat + n_ctx
    depth = w_in.shape[0]
    assert n_lat % 2048 == 0 and n_ctx == 256 and n % 8 == 0 and b < 8

    c_all = jnp.concatenate([c, c_ctx[None], jnp.zeros((8 - b - 1, d), F32)], axis=0)
    mod = _ada_call(c_all, w_ada, b_ada).reshape(depth, 8, 3, d)

    o_aq = GQA_HEADS * HEAD_DIM + 2 * GQA_KV_HEADS * HEAD_DIM
    o_mq = o_aq + MLA_HEADS * (MLA_NOPE + MLA_ROPE)
    o_ckv = o_mq + MLA_KV_RANK
    o_kr = o_ckv + MLA_ROPE
    o_dend = o_kr + 3 * DIFF_HEADS * 2 * DIFF_QK
    w_a = w_in[:, :, :o_aq].astype(BF16)
    mq = w_in[:, :, o_aq:o_mq].reshape(depth, d, MLA_HEADS, MLA_NOPE + MLA_ROPE)
    mq = jnp.pad(mq, ((0, 0), (0, 0), (0, 0), (0, MLA_QK_PAD - MLA_NOPE - MLA_ROPE)))
    w_b = jnp.concatenate(
        [mq.reshape(depth, d, MLA_HEADS * MLA_QK_PAD), w_in[:, :, o_mq:o_ckv],
         jnp.pad(w_in[:, :, o_ckv:o_kr], ((0, 0), (0, 0), (0, LANES - MLA_ROPE)))], axis=-1).astype(BF16)
    w_c = w_in[:, :, o_kr:o_dend].astype(BF16)
    o_gend = o_dend + 3 * BRANCH_WIDTH
    w_g = jnp.concatenate([w_in[:, :, o_gend:], w_in[:, :, o_dend:o_gend]], axis=-1).astype(BF16)
    w_ukv = jnp.concatenate([mla_w_uk, mla_w_uv], axis=-1).astype(BF16)
    w_br = jnp.stack([w_br_gqa, w_br_mla, w_br_diff], axis=1).astype(BF16)
    w_o = w_out.astype(BF16)
    bias_g = jnp.concatenate([b_merge, jnp.zeros((depth, 3 * BRANCH_WIDTH), F32)], axis=-1)
    g_bq = jnp.concatenate([mla_q_nope_norm, mla_q_rope_norm,
                            jnp.zeros((depth, MLA_QK_PAD - MLA_NOPE - MLA_ROPE), F32)], axis=-1)
    g_kr = jnp.pad(mla_k_rope_norm, ((0, 0), (0, LANES - MLA_ROPE)))
    g_dq = jnp.tile(diff_q_norm, (1, 2))
    g_dk = jnp.tile(diff_k_norm, (1, 2))
    lamv = jnp.stack([diff_lambda_q1, diff_lambda_k1, diff_lambda_q2, diff_lambda_k2], axis=1)

    cos128, sin128 = _rope_tables(n_lat, n_ctx, HEAD_DIM)
    cos64, sin64 = _rope_tables(n_lat, n_ctx, MLA_ROPE)

    lat_chunks = tuple((s, 1024) for s in range(0, n_lat, 1024)) + ((n_lat, n_ctx),)
    assert sum(size for _, size in lat_chunks) == n
    ctx_chunks = ((0, n_ctx),)
    ctx_tile = n_lat // n_ctx

    for l in range(depth):
        last = l == depth - 1
        lam_init = 0.8 - 0.6 * math.exp(-0.3 * l)
        h = _h_call(x, ctx, norm_w[l][None], mod[l])

        qa, ka, va = _proj_a_call(h, w_a[l], gqa_q_norm[l][None], gqa_k_norm[l][None], cos128, sin128)
        qb, ckv, kr = _proj_b_call(h, w_b[l], g_bq[l][None], mla_kv_norm[l][None], g_kr[l][None],
                                   cos64, sin64)
        qc, kc, vc = _proj_c_call(h, w_c[l], g_dq[l][None], g_dk[l][None], cos64, sin64)
        gm = _gate_call(h, w_g[l], bias_g[l][None])
        kb, vb = _mla_up_call(ckv, kr, w_ukv[l], mla_k_nope_norm[l][None])

        qa = qa.reshape(b, GQA_KV_HEADS, GQA_GROUP, n, HEAD_DIM)
        qb = qb.reshape(b, MLA_HEADS, 1, n, MLA_QK_PAD)
        diff_kw = dict(diff=True, lamv=lamv[l][None], sub=diff_subln[l][None], lam_init=lam_init)

        ba = _score_bound([(gqa_q_norm[l], HEAD_DIM)], [(gqa_k_norm[l], HEAD_DIM)], HEAD_DIM ** -0.5)
        bb = _score_bound([(mla_q_nope_norm[l], MLA_NOPE), (mla_q_rope_norm[l], MLA_ROPE)],
                          [(mla_k_nope_norm[l], MLA_NOPE), (mla_k_rope_norm[l], MLA_ROPE)],
                          (MLA_NOPE + MLA_ROPE) ** -0.5)
        bc = _score_bound([(diff_q_norm[l], DIFF_QK)], [(diff_k_norm[l], DIFF_QK)], DIFF_QK ** -0.5)
        vta = _vt_call(va, "vt_gqa")
        vtb = _vt_call(vb, "vt_mla")
        vtc = _vt_call(vc, "vt_diff")
        if not last:
            ctx_kw = dict(tq=n_ctx, q_tile0=ctx_tile, n_q_tiles=1, kv_tile0=ctx_tile, chunks=ctx_chunks)
            oa = _flash_call(ba, qa, ka, va, vta, name="flash_gqa_ctx", **ctx_kw)
            ob = _flash_call(bb, qb, kb, vb, vtb, name="flash_mla_ctx", **ctx_kw)
            oc = _flash_call(bc, qc, kc, vc, vtc, name="flash_diff_ctx", **ctx_kw, **diff_kw)
            y = _merge_call(oa, ob, oc, gm, w_br[l], tm=n_ctx, gm_tile0=ctx_tile, name="branch_merge_ctx")
            ctx = _out_call(y, w_o[l], ctx, mod[l], tm=n_ctx, ctx_stream=True, name="out_proj_ctx")

        lat_kw = dict(q_tile0=0, kv_tile0=0, chunks=lat_chunks)
        oa = _flash_call(ba, qa, ka, va, vta, tq=512, n_q_tiles=n_lat // 512, name="flash_gqa", **lat_kw)
        ob = _flash_call(bb, qb, kb, vb, vtb, tq=2048, n_q_tiles=n_lat // 2048, name="flash_mla", **lat_kw)
        oc = _flash_call(bc, qc, kc, vc, vtc, tq=1024, n_q_tiles=n_lat // 1024, name="flash_diff",
                         **lat_kw, **diff_kw)
        y = _merge_call(oa, ob, oc, gm, w_br[l], tm=512, gm_tile0=0, name="branch_merge")
        x = _out_call(y, w_o[l], x, mod[l], tm=512, ctx_stream=False, name="out_proj")
    return x
```

```python
import functools
import math

import jax
import jax.numpy as jnp
import numpy as np
from jax import lax
from jax.experimental import pallas as pl
from jax.experimental.pallas import tpu as pltpu

F32 = jnp.float32
BF16 = jnp.bfloat16

GRID_W = 64
ROPE_THETA = 10000.0
EPS = 1e-6

D_MODEL = 2048
HEAD_DIM = 128
BRANCH_WIDTH = 1024
GQA_HEADS = 8
GQA_KV_HEADS = 2
GQA_GROUP = GQA_HEADS // GQA_KV_HEADS
MLA_HEADS = 8
MLA_NOPE = 128
MLA_ROPE = 64
MLA_QK_PAD = 256
MLA_KV_RANK = 512
DIFF_HEADS = 8
DIFF_QK = 64

LANES = 128
MXU_N = 256
VMEM_LIMIT = 56 * 1024 * 1024
LOG2E = 1.4426950408889634
SAFE_BOUND_LOG2 = 57.0


def _params(sem, vmem=VMEM_LIMIT):
    return pltpu.CompilerParams(dimension_semantics=sem, vmem_limit_bytes=vmem)


def _resident(shape, index_map):
    return pl.BlockSpec(shape, index_map, pipeline_mode=pl.Buffered(1))


def _rms(y, gain, n):
    ms = jnp.sum(y * y, axis=-1, keepdims=True) * (1.0 / n)
    return y * lax.rsqrt(ms + EPS) * gain


def _seg_rms(y, gain, lo):
    ss = y * y
    s_lo = jnp.sum(jnp.where(lo, ss, 0.0), axis=-1, keepdims=True)
    s_hi = jnp.sum(jnp.where(lo, 0.0, ss), axis=-1, keepdims=True)
    ms = jnp.where(lo, s_lo, s_hi) * (1.0 / DIFF_QK)
    return y * lax.rsqrt(ms + EPS) * gain


def _rope(y, cos, sin_signed):
    return y * cos + pltpu.roll(y, LANES // 2, 1) * sin_signed


def _lane_src_a():
    e = np.arange(HEAD_DIM // 4)
    return np.concatenate([e, 64 + e, 32 + e, 96 + e])


def _lane_src_b():
    e = np.arange(MLA_ROPE // 4)
    pad = -np.ones(32, np.int64)
    return np.concatenate([e, 32 + e, pad, 16 + e, 48 + e, pad])


def _lane_src_c():
    e = np.arange(DIFF_QK // 4)
    x1 = [m * DIFF_QK + a * 32 + e for m in (0, 1) for a in (0, 1)]
    x2 = [m * DIFF_QK + a * 32 + 16 + e for m in (0, 1) for a in (0, 1)]
    return np.concatenate(x1 + x2)


def _permute_lanes(vec, src):
    out = jnp.take(vec, jnp.asarray(np.maximum(src, 0)), axis=-1)
    return jnp.where(jnp.asarray(src >= 0), out, 0.0)


def _sigmoid(z):
    return 1.0 / (1.0 + jnp.exp(-z))


def _ada_kernel(c_ref, w_ref, b_ref, o_ref):
    c = c_ref[...]
    a = (c * _sigmoid(c)).astype(BF16)
    o_ref[0] = jnp.dot(a, w_ref[0].astype(BF16), preferred_element_type=F32) + b_ref[0]


def _ada_call(c_all, w_ada, b_ada):
    depth, d, n3 = w_ada.shape
    tn = 512
    return pl.pallas_call(
        _ada_kernel,
        grid=(depth, n3 // tn),
        in_specs=[pl.BlockSpec((8, d), lambda l, j: (0, 0)),
                  pl.BlockSpec((1, d, tn), lambda l, j: (l, 0, j)),
                  pl.BlockSpec((1, 1, tn), lambda l, j: (l, 0, j))],
        out_specs=pl.BlockSpec((1, 8, tn), lambda l, j: (l, 0, j)),
        out_shape=jax.ShapeDtypeStruct((depth, 8, n3), F32),
        compiler_params=_params(("parallel", "parallel")),
        name="ada_mod",
    )(c_all, w_ada, b_ada.reshape(depth, 1, n3))


def _h_kernel(x_ref, ctx_ref, nw_ref, mod_ref, modc_ref, h_ref, *, lat_tiles):
    def norm_mod(src_ref, m_ref):
        x = src_ref[0]
        ms = jnp.mean(x * x, axis=-1, keepdims=True)
        y = x * lax.rsqrt(ms + EPS) * nw_ref[...]
        h_ref[0] = (y * (1.0 + m_ref[0, 1:2, :]) + m_ref[0, 0:1, :]).astype(BF16)

    t = pl.program_id(1)

    @pl.when(t < lat_tiles)
    def _():
        norm_mod(x_ref, mod_ref)

    @pl.when(t >= lat_tiles)
    def _():
        norm_mod(ctx_ref, modc_ref)


def _h_call(x, ctx, norm_w, mod):
    b, n_lat, d = x.shape
    tr = ctx.shape[1]
    lat_tiles = n_lat // tr
    return pl.pallas_call(
        functools.partial(_h_kernel, lat_tiles=lat_tiles),
        grid=(b, lat_tiles + 1),
        in_specs=[pl.BlockSpec((1, tr, d), lambda bi, t: (bi, jnp.minimum(t, lat_tiles - 1), 0)),
                  pl.BlockSpec((1, tr, d), lambda bi, t: (bi, 0, 0)),
                  pl.BlockSpec((1, d), lambda bi, t: (0, 0)),
                  pl.BlockSpec((1, 3, d), lambda bi, t: (bi, 0, 0)),
                  pl.BlockSpec((1, 3, d), lambda bi, t: (b, 0, 0))],
        out_specs=pl.BlockSpec((1, tr, d), lambda bi, t: (bi, t, 0)),
        out_shape=jax.ShapeDtypeStruct((b, n_lat + tr, d), BF16),
        compiler_params=_params(("parallel", "parallel")),
        name="norm_mod",
    )(x, ctx, norm_w, mod, mod)


def _proj_a_kernel(h_ref, w_ref, gq_ref, gk_ref, cos_ref, sin_ref, q_ref, k_ref, v_ref):
    h = h_ref[0]
    cos = cos_ref[...]
    sin = sin_ref[...]
    scale = HEAD_DIM ** -0.5 * LOG2E
    n_q = GQA_HEADS * HEAD_DIM // MXU_N
    for c in range(n_q + 2):
        y = jnp.dot(h, w_ref[:, c * MXU_N:(c + 1) * MXU_N], preferred_element_type=F32)
        for u in range(2):
            yu = y[:, u * LANES:(u + 1) * LANES]
            if c < n_q:
                z = _rope(_rms(yu, gq_ref[...], HEAD_DIM), cos, sin)
                q_ref[0, 2 * c + u] = (z * scale).astype(BF16)
            elif c == n_q:
                z = _rope(_rms(yu, gk_ref[...], HEAD_DIM), cos, sin)
                k_ref[0, u] = z.astype(BF16)
            else:
                v_ref[0, u] = yu.astype(BF16)


def _proj_b_kernel(h_ref, w_ref, gq_ref, gckv_ref, gkr_ref, cos_ref, sin_ref,
                   q_ref, ckv_ref, kr_ref):
    h = h_ref[0]
    cos = cos_ref[...]
    sin = sin_ref[...]
    scale = (MLA_NOPE + MLA_ROPE) ** -0.5 * LOG2E
    for hd in range(MLA_HEADS):
        y = jnp.dot(h, w_ref[:, hd * MLA_QK_PAD:(hd + 1) * MLA_QK_PAD], preferred_element_type=F32)
        nope = _rms(y[:, :LANES], gq_ref[:, :LANES], MLA_NOPE)
        rope = _rope(_rms(y[:, LANES:], gq_ref[:, LANES:], MLA_ROPE), cos, sin)
        q_ref[0, hd, :, 0:LANES] = (nope * scale).astype(BF16)
        q_ref[0, hd, :, LANES:2 * LANES] = (rope * scale).astype(BF16)
    base = MLA_HEADS * MLA_QK_PAD
    y0 = jnp.dot(h, w_ref[:, base:base + MXU_N], preferred_element_type=F32)
    y1 = jnp.dot(h, w_ref[:, base + MXU_N:base + 2 * MXU_N], preferred_element_type=F32)
    ms = (jnp.sum(y0 * y0, axis=-1, keepdims=True)
          + jnp.sum(y1 * y1, axis=-1, keepdims=True)) * (1.0 / MLA_KV_RANK)
    inv = lax.rsqrt(ms + EPS)
    ckv_ref[0, :, 0:MXU_N] = (y0 * inv * gckv_ref[:, 0:MXU_N]).astype(BF16)
    ckv_ref[0, :, MXU_N:2 * MXU_N] = (y1 * inv * gckv_ref[:, MXU_N:2 * MXU_N]).astype(BF16)
    base += MLA_KV_RANK
    yk = jnp.dot(h, w_ref[:, base:base + LANES], preferred_element_type=F32)
    kr = _rope(_rms(yk, gkr_ref[...], MLA_ROPE), cos, sin)
    kr_ref[0] = kr.astype(BF16)


def _proj_c_kernel(h_ref, w_ref, gq_ref, gk_ref, cos_ref, sin_ref, q_ref, k_ref, v_ref):
    h = h_ref[0]
    cos = cos_ref[...]
    sin = sin_ref[...]
    scale = DIFF_QK ** -0.5 * LOG2E
    lo = (lax.broadcasted_iota(jnp.int32, (1, LANES), 1) & 32) == 0
    per = DIFF_HEADS // 2
    for c in range(3 * per):
        y = jnp.dot(h, w_ref[:, c * MXU_N:(c + 1) * MXU_N], preferred_element_type=F32)
        for u in range(2):
            yu = y[:, u * LANES:(u + 1) * LANES]
            if c < per:
                z = _rope(_seg_rms(yu, gq_ref[...], lo), cos, sin) * scale
                q_ref[0, 2 * c + u, 0] = jnp.where(lo, z, 0.0).astype(BF16)
                q_ref[0, 2 * c + u, 1] = jnp.where(lo, 0.0, z).astype(BF16)
            elif c < 2 * per:
                z = _rope(_seg_rms(yu, gk_ref[...], lo), cos, sin)
                k_ref[0, 2 * (c - per) + u] = z.astype(BF16)
            else:
                v_ref[0, 2 * (c - 2 * per) + u] = yu.astype(BF16)


def _proj_tm(n):
    return n // 8


def _layer_weight(w, layer):
    return _resident((None,) + w.shape[1:], lambda bi, i: (layer, 0, 0))


def _proj_a_call(h, w, layer, gq, gk, cos, sin):
    b, n, d = h.shape
    tm = _proj_tm(n)
    row = lambda bi, i: (bi, i, 0)
    head = lambda bi, i: (bi, 0, i, 0)
    const = lambda bi, i: (0, 0)
    return pl.pallas_call(
        _proj_a_kernel,
        grid=(b, n // tm),
        in_specs=[pl.BlockSpec((1, tm, d), row),
                  _layer_weight(w, layer),
                  pl.BlockSpec((1, LANES), const), pl.BlockSpec((1, LANES), const),
                  pl.BlockSpec((tm, LANES), lambda bi, i: (i, 0)),
                  pl.BlockSpec((tm, LANES), lambda bi, i: (i, 0))],
        out_specs=[pl.BlockSpec((1, GQA_HEADS, tm, HEAD_DIM), head),
                   pl.BlockSpec((1, GQA_KV_HEADS, tm, HEAD_DIM), head),
                   pl.BlockSpec((1, GQA_KV_HEADS, tm, HEAD_DIM), head)],
        out_shape=[jax.ShapeDtypeStruct((b, GQA_HEADS, n, HEAD_DIM), BF16),
                   jax.ShapeDtypeStruct((b, GQA_KV_HEADS, n, HEAD_DIM), BF16),
                   jax.ShapeDtypeStruct((b, GQA_KV_HEADS, n, HEAD_DIM), BF16)],
        compiler_params=_params(("parallel", "parallel")),
        name="proj_gqa",
    )(h, w, gq, gk, cos, sin)


def _proj_b_call(h, w, layer, gq, gckv, gkr, cos, sin):
    b, n, d = h.shape
    tm = _proj_tm(n)
    row = lambda bi, i: (bi, i, 0)
    const = lambda bi, i: (0, 0)
    return pl.pallas_call(
        _proj_b_kernel,
        grid=(b, n // tm),
        in_specs=[pl.BlockSpec((1, tm, d), row),
                  _layer_weight(w, layer),
                  pl.BlockSpec((1, MLA_QK_PAD), const),
                  pl.BlockSpec((1, MLA_KV_RANK), const),
                  pl.BlockSpec((1, LANES), const),
                  pl.BlockSpec((tm, LANES), lambda bi, i: (i, 0)),
                  pl.BlockSpec((tm, LANES), lambda bi, i: (i, 0))],
        out_specs=[pl.BlockSpec((1, MLA_HEADS, tm, MLA_QK_PAD), lambda bi, i: (bi, 0, i, 0)),
                   pl.BlockSpec((1, tm, MLA_KV_RANK), row),
                   pl.BlockSpec((1, tm, LANES), row)],
        out_shape=[jax.ShapeDtypeStruct((b, MLA_HEADS, n, MLA_QK_PAD), BF16),
                   jax.ShapeDtypeStruct((b, n, MLA_KV_RANK), BF16),
                   jax.ShapeDtypeStruct((b, n, LANES), BF16)],
        compiler_params=_params(("parallel", "parallel")),
        name="proj_mla",
    )(h, w, gq, gckv, gkr, cos, sin)


def _proj_c_call(h, w, layer, gq, gk, cos, sin):
    b, n, d = h.shape
    tm = _proj_tm(n)
    row = lambda bi, i: (bi, i, 0)
    const = lambda bi, i: (0, 0)
    return pl.pallas_call(
        _proj_c_kernel,
        grid=(b, n // tm),
        in_specs=[pl.BlockSpec((1, tm, d), row),
                  _layer_weight(w, layer),
                  pl.BlockSpec((1, LANES), const), pl.BlockSpec((1, LANES), const),
                  pl.BlockSpec((tm, LANES), lambda bi, i: (i, 0)),
                  pl.BlockSpec((tm, LANES), lambda bi, i: (i, 0))],
        out_specs=[pl.BlockSpec((1, DIFF_HEADS, 2, tm, LANES), lambda bi, i: (bi, 0, 0, i, 0)),
                   pl.BlockSpec((1, DIFF_HEADS, tm, LANES), lambda bi, i: (bi, 0, i, 0)),
                   pl.BlockSpec((1, DIFF_HEADS, tm, LANES), lambda bi, i: (bi, 0, i, 0))],
        out_shape=[jax.ShapeDtypeStruct((b, DIFF_HEADS, 2, n, LANES), BF16),
                   jax.ShapeDtypeStruct((b, DIFF_HEADS, n, LANES), BF16),
                   jax.ShapeDtypeStruct((b, DIFF_HEADS, n, LANES), BF16)],
        compiler_params=_params(("parallel", "parallel")),
        name="proj_diff",
    )(h, w, gq, gk, cos, sin)


def _mla_up_kernel(ckv_ref, kr_ref, w_ref, gk_ref, k_ref, v_ref):
    ckv = ckv_ref[0]
    kr = kr_ref[0]
    per = MLA_HEADS // 2
    for c in range(2 * per):
        y = jnp.dot(ckv, w_ref[:, c * MXU_N:(c + 1) * MXU_N], preferred_element_type=F32)
        for u in range(2):
            yu = y[:, u * LANES:(u + 1) * LANES]
            if c < per:
                k_ref[0, 2 * c + u, :, 0:LANES] = _rms(yu, gk_ref[...], MLA_NOPE).astype(BF16)
                k_ref[0, 2 * c + u, :, LANES:2 * LANES] = kr
            else:
                v_ref[0, 2 * (c - per) + u] = yu.astype(BF16)


def _mla_up_call(ckv, kr, w, gk):
    b, n, r = ckv.shape
    tm = _proj_tm(n)
    row = lambda bi, i: (bi, i, 0)
    const = lambda bi, i: (0, 0)
    return pl.pallas_call(
        _mla_up_kernel,
        grid=(b, n // tm),
        in_specs=[pl.BlockSpec((1, tm, r), row),
                  pl.BlockSpec((1, tm, LANES), row),
                  _resident(w.shape, const),
                  pl.BlockSpec((1, LANES), const)],
        out_specs=[pl.BlockSpec((1, MLA_HEADS, tm, MLA_QK_PAD), lambda bi, i: (bi, 0, i, 0)),
                   pl.BlockSpec((1, MLA_HEADS, tm, LANES), lambda bi, i: (bi, 0, i, 0))],
        out_shape=[jax.ShapeDtypeStruct((b, MLA_HEADS, n, MLA_QK_PAD), BF16),
                   jax.ShapeDtypeStruct((b, MLA_HEADS, n, LANES), BF16)],
        compiler_params=_params(("parallel", "parallel")),
        name="mla_up",
    )(ckv, kr, w, gk)


def _gate_kernel(h_ref, w_ref, b_ref, o_ref, *, n_sigmoid_tiles):
    is_sigmoid = pl.program_id(2) < n_sigmoid_tiles
    h = h_ref[0]
    for c in range(o_ref.shape[-1] // MXU_N):
        cols = slice(c * MXU_N, (c + 1) * MXU_N)
        y = jnp.dot(h, w_ref[:, cols], preferred_element_type=F32)
        t = _sigmoid(y + b_ref[:, cols])
        o_ref[0, :, cols] = jnp.where(is_sigmoid, t, y * t).astype(BF16)


def _gate_call(h, w, layer, bias):
    b, n, d = h.shape
    cols = w.shape[-1]
    tm = n // 4
    tn = 3072
    assert (cols - 3 * BRANCH_WIDTH) % tn == 0 and cols % tn == 0
    return pl.pallas_call(
        functools.partial(_gate_kernel, n_sigmoid_tiles=(cols - 3 * BRANCH_WIDTH) // tn),
        grid=(b, n // tm, cols // tn),
        in_specs=[pl.BlockSpec((1, tm, d), lambda bi, i, j: (bi, i, 0)),
                  pl.BlockSpec((None, d, tn), lambda bi, i, j: (layer, 0, j)),
                  pl.BlockSpec((1, tn), lambda bi, i, j: (0, j))],
        out_specs=pl.BlockSpec((1, tm, tn), lambda bi, i, j: (bi, i, j)),
        out_shape=jax.ShapeDtypeStruct((b, n, cols), BF16),
        compiler_params=_params(("parallel", "parallel", "parallel")),
        name="gate_proj",
    )(h, w, bias)


def _flash_kernel(*refs, group, tq, chunks, diff, lam_init):
    if diff:
        bound_ref, q_ref, k_ref, v_ref, vt_ref, lamv_ref, sub_ref, o_ref = refs
    else:
        bound_ref, q_ref, k_ref, v_ref, vt_ref, o_ref = refs
    dq = q_ref.shape[-1]
    nt = (((1,), (1,)), ((), ()))

    q = q_ref[0, 0].reshape(group * tq, dq)
    half = group * tq // 2
    halves = (slice(0, half), slice(half, 2 * half))
    bound = bound_ref[0]
    safe = bound <= SAFE_BOUND_LOG2

    def finish(out):
        if diff:
            lv = lamv_ref[0]
            lam = (jnp.exp(jnp.sum(lv[0:1] * lv[1:2], axis=-1, keepdims=True))
                   - jnp.exp(jnp.sum(lv[2:3] * lv[3:4], axis=-1, keepdims=True)) + lam_init)
            o = out[:tq] - lam * out[tq:]
            o_ref[0] = (_rms(o, sub_ref[...], LANES) * (1.0 - lam_init)).astype(BF16)
        else:
            for g in range(group):
                o_ref[0, :, g * LANES:(g + 1) * LANES] = out[g * tq:(g + 1) * tq].astype(BF16)

    @pl.when(safe)
    def _():
        accs = [None, None]

        def k_dot_q(start, size):
            k = k_ref[0, 0, start:start + size, :]
            return [lax.dot_general(k, q[rows], nt, preferred_element_type=F32) for rows in halves]

        def vt_dot_pt(pts, start, size):
            vt = vt_ref[0, 0, :, start:start + size]
            for i, pt in enumerate(pts):
                t = jnp.dot(vt, pt, preferred_element_type=F32)
                accs[i] = t if accs[i] is None else accs[i] + t

        pending = None
        for start, size in chunks:
            ss = k_dot_q(start, size)
            if pending is not None:
                vt_dot_pt(*pending)
            pending = ([jnp.exp2(s - bound).astype(BF16) for s in ss], start, size)
        vt_dot_pt(*pending)
        acc = jnp.concatenate(accs, axis=1)
        finish((acc[:LANES] / acc[LANES:LANES + 1]).T)

    @pl.when(jnp.logical_not(safe))
    def _():
        m = l = acc = None
        for start, size in chunks:
            k = k_ref[0, 0, start:start + size, :]
            v = v_ref[0, 0, start:start + size, :]
            s = lax.dot_general(q, k, nt, preferred_element_type=F32)
            m_cur = jnp.max(s, axis=-1, keepdims=True)
            if m is None:
                m = m_cur
                p = jnp.exp2(s - m)
                l = jnp.sum(p, axis=-1, keepdims=True)
                acc = jnp.dot(p.astype(BF16), v, preferred_element_type=F32)
            else:
                m_new = jnp.maximum(m, m_cur)
                alpha = jnp.exp2(m - m_new)
                p = jnp.exp2(s - m_new)
                l = alpha * l + jnp.sum(p, axis=-1, keepdims=True)
                acc = alpha * acc + jnp.dot(p.astype(BF16), v, preferred_element_type=F32)
                m = m_new
        finish(acc / l)


VT_ROWS = LANES + 16


def _vt_kernel(v_ref, o_ref):
    n = v_ref.shape[2]
    for j in range(n // LANES):
        cols = slice(j * LANES, (j + 1) * LANES)
        o_ref[0, 0, 0:LANES, cols] = v_ref[0, 0, cols, :].astype(F32).T.astype(BF16)
    o_ref[0, 0, LANES:VT_ROWS, :] = jnp.ones((VT_ROWS - LANES, n), BF16)


def _vt_call(v, name):
    b, heads, n, dv = v.shape
    return pl.pallas_call(
        _vt_kernel,
        grid=(b, heads),
        in_specs=[pl.BlockSpec((1, 1, n, dv), lambda bi, h: (bi, h, 0, 0))],
        out_specs=pl.BlockSpec((1, 1, VT_ROWS, n), lambda bi, h: (bi, h, 0, 0)),
        out_shape=jax.ShapeDtypeStruct((b, heads, VT_ROWS, n), BF16),
        compiler_params=_params(("parallel", "parallel")),
        name=name,
    )(v)


def _flash_call(bound, q, k, v, vt, *, tq, q_tile0, n_q_tiles, kv_tile0, chunks,
                diff=False, lamv=None, sub=None, lam_init=0.0, name="flash"):
    b, heads, group, n, dq = q.shape
    kv_len = sum(size for _, size in chunks)
    out_w = LANES if diff else group * LANES
    in_specs = [pl.BlockSpec(memory_space=pltpu.SMEM),
                pl.BlockSpec((1, 1, group, tq, dq), lambda bi, h, i: (bi, h, 0, q_tile0 + i, 0)),
                pl.BlockSpec((1, 1, kv_len, dq), lambda bi, h, i: (bi, h, kv_tile0, 0)),
                pl.BlockSpec((1, 1, kv_len, LANES), lambda bi, h, i: (bi, h, kv_tile0, 0)),
                pl.BlockSpec((1, 1, VT_ROWS, kv_len), lambda bi, h, i: (bi, h, 0, kv_tile0))]
    args = [bound, q, k, v, vt]
    if diff:
        in_specs += [pl.BlockSpec((1, 4, DIFF_QK), lambda bi, h, i: (0, 0, 0)),
                     pl.BlockSpec((1, LANES), lambda bi, h, i: (0, 0))]
        args += [lamv, sub]
    kern = functools.partial(_flash_kernel, group=group, tq=tq, chunks=chunks, diff=diff,
                             lam_init=lam_init)
    return pl.pallas_call(
        kern,
        grid=(b, heads, n_q_tiles),
        in_specs=in_specs,
        out_specs=pl.BlockSpec((1, tq, out_w), lambda bi, h, i: (bi, i, h)),
        out_shape=jax.ShapeDtypeStruct((b, n_q_tiles * tq, BRANCH_WIDTH), BF16),
        compiler_params=_params(("parallel", "parallel", "parallel")),
        name=name,
    )(*args)


def _score_bound(q_terms, k_terms, scale):
    q2 = sum(d * jnp.max(jnp.square(g.astype(F32))) for g, d in q_terms)
    k2 = sum(d * jnp.max(jnp.square(g.astype(F32))) for g, d in k_terms)
    return (1.01 * scale * LOG2E * jnp.sqrt(q2 * k2)).reshape(1)


N_CHUNK = 512


def _merge_kernel(oa_ref, ob_ref, oc_ref, g_ref, ma_ref, mb_ref, mc_ref, w_ref, y_ref):
    ogs = [o_ref[0] * g_ref[0, :, r * BRANCH_WIDTH:(r + 1) * BRANCH_WIDTH]
           for r, o_ref in enumerate((oa_ref, ob_ref, oc_ref))]
    for c in range(y_ref.shape[-1] // N_CHUNK):
        cols = slice(c * N_CHUNK, (c + 1) * N_CHUNK)
        acc = None
        for r, m_ref in enumerate((ma_ref, mb_ref, mc_ref)):
            z = jnp.dot(ogs[r], w_ref[r, :, cols], preferred_element_type=F32)
            t = m_ref[0, :, cols].astype(F32) * z
            acc = t if acc is None else acc + t
        y_ref[0, :, cols] = acc.astype(BF16)


def _merge_call(oa, ob, oc, gm, w_br, *, tm, gm_tile0, name):
    b, rows, _ = oa.shape
    d = w_br.shape[-1]
    gate_w = 3 * BRANCH_WIDTH
    assert (3 * d) % gate_w == 0
    o_spec = pl.BlockSpec((1, tm, BRANCH_WIDTH), lambda bi, i: (bi, i, 0))
    m_specs = [pl.BlockSpec((1, tm, d), lambda bi, i, r=r: (bi, gm_tile0 + i, r)) for r in range(3)]
    return pl.pallas_call(
        _merge_kernel,
        grid=(b, rows // tm),
        in_specs=[o_spec, o_spec, o_spec,
                  pl.BlockSpec((1, tm, gate_w), lambda bi, i: (bi, gm_tile0 + i, 3 * d // gate_w)),
                  *m_specs,
                  _resident(w_br.shape, lambda bi, i: (0, 0, 0))],
        out_specs=pl.BlockSpec((1, tm, d), lambda bi, i: (bi, i, 0)),
        out_shape=jax.ShapeDtypeStruct((b, rows, d), BF16),
        compiler_params=_params(("parallel", "parallel")),
        name=name,
    )(oa, ob, oc, gm, gm, gm, gm, w_br)


def _out_kernel(y_ref, w_ref, x_ref, mod_ref, o_ref):
    y = y_ref[0]
    for c in range(o_ref.shape[-1] // N_CHUNK):
        cols = slice(c * N_CHUNK, (c + 1) * N_CHUNK)
        out = jnp.dot(y, w_ref[:, cols], preferred_element_type=F32)
        o_ref[0, :, cols] = x_ref[0, :, cols] + mod_ref[0, 2:3, cols] * out


def _out_call(y, w_out, xs, mod, *, tm, ctx_stream, name):
    b, rows, d = xs.shape
    return pl.pallas_call(
        _out_kernel,
        grid=(b, rows // tm),
        in_specs=[pl.BlockSpec((1, tm, d), lambda bi, i: (bi, i, 0)),
                  _resident(w_out.shape, lambda bi, i: (0, 0)),
                  pl.BlockSpec((1, tm, d), lambda bi, i: (bi, i, 0)),
                  pl.BlockSpec((1, 3, d), lambda bi, i: (b if ctx_stream else bi, 0, 0))],
        out_specs=pl.BlockSpec((1, tm, d), lambda bi, i: (bi, i, 0)),
        out_shape=jax.ShapeDtypeStruct((b, rows, d), F32),
        compiler_params=_params(("parallel", "parallel")),
        name=name,
    )(y, w_out, xs, mod)


def _rope_tables(n_lat, n_ctx, rot_dim, src):
    axis_dim = rot_dim // 2
    t = jnp.arange(n_lat, dtype=jnp.int32)
    pos_row = (t // GRID_W).astype(F32)
    pos_col = (t % GRID_W).astype(F32)
    inv_freq = ROPE_THETA ** (-jnp.arange(0, axis_dim, 2, dtype=F32) / axis_dim)
    ang_r = pos_row[:, None] * inv_freq
    ang_c = pos_col[:, None] * inv_freq
    ang = jnp.concatenate([ang_r, ang_r, ang_c, ang_c], axis=-1)
    cos, sin = jnp.cos(ang), jnp.sin(ang)
    lane = jnp.arange(rot_dim)
    sign = jnp.where((lane % axis_dim) < axis_dim // 2, -1.0, 1.0).astype(F32)
    pad = jnp.asarray(src < 0)
    lanes = jnp.asarray(np.maximum(src, 0) % rot_dim)
    cos = jnp.where(pad, 1.0, jnp.take(cos, lanes, axis=1))
    sin = jnp.where(pad, 0.0, jnp.take(sin * sign, lanes, axis=1))
    cos = jnp.concatenate([cos, jnp.ones((n_ctx, LANES), F32)], axis=0)
    sin = jnp.concatenate([sin, jnp.zeros((n_ctx, LANES), F32)], axis=0)
    return cos, sin


PREP_WINDOW = 384
_IDENT = np.arange(LANES)


def _prep_patterns():
    def pattern(offset, units):
        s = np.zeros((PREP_WINDOW, MXU_N), np.float32)
        for u, (start, src) in enumerate(units):
            for t, idx in enumerate(src):
                if idx >= 0:
                    s[offset + start + idx, u * LANES + t] = 1.0
        return s

    two = lambda src: [(0, src), (LANES, src)]
    mla_head = [(0, _IDENT), (LANES, _lane_src_b())]
    pats = {
        "copy0": pattern(0, two(_IDENT)), "copy64": pattern(64, two(_IDENT)),
        "rot128": pattern(0, two(_lane_src_a())), "diff64": pattern(64, two(_lane_src_c())),
        "mla0": pattern(0, mla_head), "mla64": pattern(64, mla_head),
        "kr": pattern(0, [(0, _lane_src_b())]),
    }
    names = sorted(pats)
    info = {k: (i, -(-(int(np.nonzero(pats[k].any(axis=1))[0].max()) + 1) // LANES) * LANES)
            for i, k in enumerate(names)}
    return np.stack([pats[k] for k in names]), info


def _prep_table():
    o_gk = GQA_HEADS * HEAD_DIM
    o_gv = o_gk + GQA_KV_HEADS * HEAD_DIM
    o_mq = o_gv + GQA_KV_HEADS * HEAD_DIM
    o_ckv = o_mq + MLA_HEADS * (MLA_NOPE + MLA_ROPE)
    o_kr = o_ckv + MLA_KV_RANK
    o_dq = o_kr + MLA_ROPE
    o_dv = o_dq + 2 * DIFF_HEADS * 2 * DIFF_QK
    o_gate = o_dv + DIFF_HEADS * LANES
    o_merge = o_gate + 3 * BRANCH_WIDTH
    table = []

    def add(out, col, width, src, kind):
        offset = src % LANES
        name = kind if kind in ("rot128", "kr") else kind + str(offset)
        assert offset in (0, 64) and (kind not in ("rot128", "kr") or offset == 0)
        table.append((out, col, width, src, name))

    for c in range(o_gv // MXU_N):
        add(0, c * MXU_N, MXU_N, c * MXU_N, "rot128")
    add(0, o_gv, MXU_N, o_gv, "copy")
    for hd in range(MLA_HEADS):
        add(1, hd * MLA_QK_PAD, MXU_N, o_mq + hd * (MLA_NOPE + MLA_ROPE), "mla")
    for c in range(MLA_KV_RANK // MXU_N):
        add(1, MLA_HEADS * MLA_QK_PAD + c * MXU_N, MXU_N, o_ckv + c * MXU_N, "copy")
    add(1, MLA_HEADS * MLA_QK_PAD + MLA_KV_RANK, LANES, o_kr, "kr")
    for c in range((o_dv - o_dq) // MXU_N):
        add(2, c * MXU_N, MXU_N, o_dq + c * MXU_N, "diff")
    for c in range((o_gate - o_dv) // MXU_N):
        add(2, o_dv - o_dq + c * MXU_N, MXU_N, o_dv + c * MXU_N, "copy")
    n_merge = 3 * D_MODEL // MXU_N
    for c in range(n_merge):
        add(3, c * MXU_N, MXU_N, o_merge + c * MXU_N, "copy")
    for c in range(3 * BRANCH_WIDTH // MXU_N):
        add(3, (n_merge + c) * MXU_N, MXU_N, o_gate + c * MXU_N, "copy")
    return table


def _prep_kernel(w_ref, s_ref, *o_refs, table, pat_index):
    total = w_ref.shape[-1]
    for out, col, width, src, name in table:
        start = src - src % LANES
        pat, window = pat_index[name]
        window = min(window, total - start)
        acc = None
        for k0 in range(0, window, MXU_N):
            k1 = min(k0 + MXU_N, window)
            t = jnp.dot(w_ref[0, :, start + k0:start + k1].astype(BF16), s_ref[pat, k0:k1, 0:width],
                        preferred_element_type=F32)
            acc = t if acc is None else acc + t
        o_refs[out][0, :, col:col + width] = acc.astype(BF16)


def _prep_call(w_in):
    depth, d, total = w_in.shape
    table = _prep_table()
    pats, pat_index = _prep_patterns()
    widths = [max(col + width for out, col, width, _, _ in table if out == o) for o in range(4)]
    tk = 128
    return pl.pallas_call(
        functools.partial(_prep_kernel, table=table, pat_index=pat_index),
        grid=(depth, d // tk),
        in_specs=[pl.BlockSpec((1, tk, total), lambda l, i: (l, i, 0)),
                  _resident(pats.shape, lambda l, i: (0, 0, 0))],
        out_specs=[pl.BlockSpec((1, tk, wd), lambda l, i: (l, i, 0)) for wd in widths],
        out_shape=[jax.ShapeDtypeStruct((depth, d, wd), BF16) for wd in widths],
        compiler_params=_params(("parallel", "parallel")),
        name="weight_layout",
    )(w_in, jnp.asarray(pats, BF16))


def kernel(x, c, ctx, c_ctx, norm_w, w_ada, b_ada, w_in, b_merge, gqa_q_norm, gqa_k_norm, mla_q_nope_norm, mla_q_rope_norm, mla_kv_norm, mla_w_uk, mla_w_uv, mla_k_nope_norm, mla_k_rope_norm, diff_q_norm, diff_k_norm, diff_lambda_q1, diff_lambda_k1, diff_lambda_q2, diff_lambda_k2, diff_subln, w_br_gqa, w_br_mla, w_br_diff, w_out):
    b, n_lat, d = x.shape
    n_ctx = ctx.shape[1]
    n = n_lat + n_ctx
    depth = w_in.shape[0]
    assert n_lat % 2048 == 0 and n_ctx == 256 and n % 8 == 0 and b < 8 and d == D_MODEL

    c_all = jnp.concatenate([c, c_ctx[None], jnp.zeros((8 - b - 1, d), F32)], axis=0)
    mod = _ada_call(c_all, w_ada, b_ada).reshape(depth, 8, 3, d)

    w_a, w_b, w_c, w_g = _prep_call(w_in)
    w_ukv = jnp.concatenate([mla_w_uk, mla_w_uv], axis=-1).astype(BF16)
    w_br = jnp.stack([w_br_gqa, w_br_mla, w_br_diff], axis=1).astype(BF16)
    w_o = w_out.astype(BF16)
    bias_g = jnp.concatenate([b_merge, jnp.zeros((depth, 3 * BRANCH_WIDTH), F32)], axis=-1)
    src_a, src_b, src_c = _lane_src_a(), _lane_src_b(), _lane_src_c()
    g_aq = _permute_lanes(gqa_q_norm, src_a)
    g_ak = _permute_lanes(gqa_k_norm, src_a)
    g_bq = jnp.concatenate([mla_q_nope_norm, _permute_lanes(mla_q_rope_norm, src_b)], axis=-1)
    g_kr = _permute_lanes(mla_k_rope_norm, src_b)
    g_dq = _permute_lanes(jnp.tile(diff_q_norm, (1, 2)), src_c)
    g_dk = _permute_lanes(jnp.tile(diff_k_norm, (1, 2)), src_c)
    lamv = jnp.stack([diff_lambda_q1, diff_lambda_k1, diff_lambda_q2, diff_lambda_k2], axis=1)

    cos_a, sin_a = _rope_tables(n_lat, n_ctx, HEAD_DIM, src_a)
    cos_b, sin_b = _rope_tables(n_lat, n_ctx, MLA_ROPE, src_b)
    cos_c, sin_c = _rope_tables(n_lat, n_ctx, DIFF_QK, src_c)

    lat_chunks = tuple((s, 1024) for s in range(0, n_lat, 1024)) + ((n_lat, n_ctx),)
    assert sum(size for _, size in lat_chunks) == n
    ctx_chunks = ((0, n_ctx),)
    ctx_tile = n_lat // n_ctx

    for l in range(depth):
        last = l == depth - 1
        lam_init = 0.8 - 0.6 * math.exp(-0.3 * l)
        h = _h_call(x, ctx, norm_w[l][None], mod[l])

        qa, ka, va = _proj_a_call(h, w_a, l, g_aq[l][None], g_ak[l][None], cos_a, sin_a)
        qb, ckv, kr = _proj_b_call(h, w_b, l, g_bq[l][None], mla_kv_norm[l][None], g_kr[l][None],
                                   cos_b, sin_b)
        qc, kc, vc = _proj_c_call(h, w_c, l, g_dq[l][None], g_dk[l][None], cos_c, sin_c)
        gm = _gate_call(h, w_g, l, bias_g[l][None])
        kb, vb = _mla_up_call(ckv, kr, w_ukv[l], mla_k_nope_norm[l][None])

        qa = qa.reshape(b, GQA_KV_HEADS, GQA_GROUP, n, HEAD_DIM)
        qb = qb.reshape(b, MLA_HEADS, 1, n, MLA_QK_PAD)
        diff_kw = dict(diff=True, lamv=lamv[l][None], sub=diff_subln[l][None], lam_init=lam_init)

        ba = _score_bound([(gqa_q_norm[l], HEAD_DIM)], [(gqa_k_norm[l], HEAD_DIM)], HEAD_DIM ** -0.5)
        bb = _score_bound([(mla_q_nope_norm[l], MLA_NOPE), (mla_q_rope_norm[l], MLA_ROPE)],
                          [(mla_k_nope_norm[l], MLA_NOPE), (mla_k_rope_norm[l], MLA_ROPE)],
                          (MLA_NOPE + MLA_ROPE) ** -0.5)
        bc = _score_bound([(diff_q_norm[l], DIFF_QK)], [(diff_k_norm[l], DIFF_QK)], DIFF_QK ** -0.5)
        vta = _vt_call(va, "vt_gqa")
        vtb = _vt_call(vb, "vt_mla")
        vtc = _vt_call(vc, "vt_diff")
        if not last:
            ctx_kw = dict(tq=n_ctx, q_tile0=ctx_tile, n_q_tiles=1, kv_tile0=ctx_tile, chunks=ctx_chunks)
            oa = _flash_call(ba, qa, ka, va, vta, name="flash_gqa_ctx", **ctx_kw)
            ob = _flash_call(bb, qb, kb, vb, vtb, name="flash_mla_ctx", **ctx_kw)
            oc = _flash_call(bc, qc, kc, vc, vtc, name="flash_diff_ctx", **ctx_kw, **diff_kw)
            y = _merge_call(oa, ob, oc, gm, w_br[l], tm=n_ctx, gm_tile0=ctx_tile, name="branch_merge_ctx")
            ctx = _out_call(y, w_o[l], ctx, mod[l], tm=n_ctx, ctx_stream=True, name="out_proj_ctx")

        lat_kw = dict(q_tile0=0, kv_tile0=0, chunks=lat_chunks)
        oa = _flash_call(ba, qa, ka, va, vta, tq=512, n_q_tiles=n_lat // 512, name="flash_gqa", **lat_kw)
        ob = _flash_call(bb, qb, kb, vb, vtb, tq=2048, n_q_tiles=n_lat // 2048, name="flash_mla", **lat_kw)
        oc = _flash_call(bc, qc, kc, vc, vtc, tq=1024, n_q_tiles=n_lat // 1024, name="flash_diff",
                         **lat_kw, **diff_kw)
        y = _merge_call(oa, ob, oc, gm, w_br[l], tm=512, gm_tile0=0, name="branch_merge")
        x = _out_call(y, w_o[l], x, mod[l], tm=512, ctx_stream=False, name="out_proj")
    return x
```

```python
import functools
import math

import jax
import jax.numpy as jnp
import numpy as np
from jax import lax
from jax.experimental import pallas as pl
from jax.experimental.pallas import tpu as pltpu

F32 = jnp.float32
BF16 = jnp.bfloat16

GRID_W = 64
ROPE_THETA = 10000.0
EPS = 1e-6

D_MODEL = 2048
HEAD_DIM = 128
BRANCH_WIDTH = 1024
GQA_HEADS = 8
GQA_KV_HEADS = 2
GQA_GROUP = GQA_HEADS // GQA_KV_HEADS
MLA_HEADS = 8
MLA_NOPE = 128
MLA_ROPE = 64
MLA_QK_PAD = 256
MLA_KV_RANK = 512
DIFF_HEADS = 8
DIFF_QK = 64

LANES = 128
MXU_N = 256
VMEM_LIMIT = 56 * 1024 * 1024
LOG2E = 1.4426950408889634
SAFE_BOUND_LOG2 = 57.0


def _params(sem, vmem=VMEM_LIMIT):
    return pltpu.CompilerParams(dimension_semantics=sem, vmem_limit_bytes=vmem)


def _resident(shape, index_map):
    return pl.BlockSpec(shape, index_map, pipeline_mode=pl.Buffered(1))


def _rms(y, gain, n):
    ms = jnp.sum(y * y, axis=-1, keepdims=True) * (1.0 / n)
    return y * lax.rsqrt(ms + EPS) * gain


def _seg_rms(y, gain, lo):
    ss = y * y
    s_lo = jnp.sum(jnp.where(lo, ss, 0.0), axis=-1, keepdims=True)
    s_hi = jnp.sum(jnp.where(lo, 0.0, ss), axis=-1, keepdims=True)
    ms = jnp.where(lo, s_lo, s_hi) * (1.0 / DIFF_QK)
    return y * lax.rsqrt(ms + EPS) * gain


def _rope(y, cos, sin_signed):
    return y * cos + pltpu.roll(y, LANES // 2, 1) * sin_signed


def _lane_src_a():
    e = np.arange(HEAD_DIM // 4)
    return np.concatenate([e, 64 + e, 32 + e, 96 + e])


def _lane_src_b():
    e = np.arange(MLA_ROPE // 4)
    pad = -np.ones(32, np.int64)
    return np.concatenate([e, 32 + e, pad, 16 + e, 48 + e, pad])


def _lane_src_c():
    e = np.arange(DIFF_QK // 4)
    x1 = [m * DIFF_QK + a * 32 + e for m in (0, 1) for a in (0, 1)]
    x2 = [m * DIFF_QK + a * 32 + 16 + e for m in (0, 1) for a in (0, 1)]
    return np.concatenate(x1 + x2)


def _permute_lanes(vec, src):
    out = jnp.take(vec, jnp.asarray(np.maximum(src, 0)), axis=-1)
    return jnp.where(jnp.asarray(src >= 0), out, 0.0)


def _sigmoid(z):
    return 1.0 / (1.0 + jnp.exp(-z))


def _dot_wt(x, wt):
    return lax.dot_general(x, wt, (((1,), (1,)), ((), ())), preferred_element_type=F32)


def _ada_kernel(c_ref, w_ref, b_ref, o_ref):
    c = c_ref[...]
    a = (c * _sigmoid(c)).astype(BF16)
    o_ref[0] = jnp.dot(a, w_ref[0].astype(BF16), preferred_element_type=F32) + b_ref[0]


def _ada_call(c_all, w_ada, b_ada):
    depth, d, n3 = w_ada.shape
    tn = 512
    return pl.pallas_call(
        _ada_kernel,
        grid=(depth, n3 // tn),
        in_specs=[pl.BlockSpec((8, d), lambda l, j: (0, 0)),
                  pl.BlockSpec((1, d, tn), lambda l, j: (l, 0, j)),
                  pl.BlockSpec((1, 1, tn), lambda l, j: (l, 0, j))],
        out_specs=pl.BlockSpec((1, 8, tn), lambda l, j: (l, 0, j)),
        out_shape=jax.ShapeDtypeStruct((depth, 8, n3), F32),
        compiler_params=_params(("parallel", "parallel")),
        name="ada_mod",
    )(c_all, w_ada, b_ada.reshape(depth, 1, n3))


def _h_kernel(x_ref, ctx_ref, nw_ref, mod_ref, modc_ref, h_ref, *, lat_tiles):
    def norm_mod(src_ref, m_ref):
        x = src_ref[0]
        ms = jnp.mean(x * x, axis=-1, keepdims=True)
        y = x * lax.rsqrt(ms + EPS) * nw_ref[...]
        h_ref[0] = (y * (1.0 + m_ref[0, 1:2, :]) + m_ref[0, 0:1, :]).astype(BF16)

    t = pl.program_id(1)

    @pl.when(t < lat_tiles)
    def _():
        norm_mod(x_ref, mod_ref)

    @pl.when(t >= lat_tiles)
    def _():
        norm_mod(ctx_ref, modc_ref)


def _h_call(x, ctx, norm_w, mod):
    b, n_lat, d = x.shape
    tr = ctx.shape[1]
    lat_tiles = n_lat // tr
    return pl.pallas_call(
        functools.partial(_h_kernel, lat_tiles=lat_tiles),
        grid=(b, lat_tiles + 1),
        in_specs=[pl.BlockSpec((1, tr, d), lambda bi, t: (bi, jnp.minimum(t, lat_tiles - 1), 0)),
                  pl.BlockSpec((1, tr, d), lambda bi, t: (bi, 0, 0)),
                  pl.BlockSpec((1, d), lambda bi, t: (0, 0)),
                  pl.BlockSpec((1, 3, d), lambda bi, t: (bi, 0, 0)),
                  pl.BlockSpec((1, 3, d), lambda bi, t: (b, 0, 0))],
        out_specs=pl.BlockSpec((1, tr, d), lambda bi, t: (bi, t, 0)),
        out_shape=jax.ShapeDtypeStruct((b, n_lat + tr, d), BF16),
        compiler_params=_params(("parallel", "parallel")),
        name="norm_mod",
    )(x, ctx, norm_w, mod, mod)


def _proj_a_kernel(h_ref, w_ref, gq_ref, gk_ref, cos_ref, sin_ref, q_ref, k_ref, v_ref):
    h = h_ref[0]
    cos = cos_ref[...]
    sin = sin_ref[...]
    scale = HEAD_DIM ** -0.5 * LOG2E
    n_q = GQA_HEADS * HEAD_DIM // MXU_N
    for c in range(n_q + 2):
        y = _dot_wt(h, w_ref[c * MXU_N:(c + 1) * MXU_N, :])
        for u in range(2):
            yu = y[:, u * LANES:(u + 1) * LANES]
            if c < n_q:
                z = _rope(_rms(yu, gq_ref[...], HEAD_DIM), cos, sin)
                q_ref[0, 2 * c + u] = (z * scale).astype(BF16)
            elif c == n_q:
                z = _rope(_rms(yu, gk_ref[...], HEAD_DIM), cos, sin)
                k_ref[0, u] = z.astype(BF16)
            else:
                v_ref[0, u] = yu.astype(BF16)


def _proj_b_kernel(h_ref, w_ref, gq_ref, gckv_ref, gkr_ref, cos_ref, sin_ref,
                   q_ref, ckv_ref, kr_ref):
    h = h_ref[0]
    cos = cos_ref[...]
    sin = sin_ref[...]
    scale = (MLA_NOPE + MLA_ROPE) ** -0.5 * LOG2E
    for hd in range(MLA_HEADS):
        y = _dot_wt(h, w_ref[hd * MLA_QK_PAD:(hd + 1) * MLA_QK_PAD, :])
        nope = _rms(y[:, :LANES], gq_ref[:, :LANES], MLA_NOPE)
        rope = _rope(_rms(y[:, LANES:], gq_ref[:, LANES:], MLA_ROPE), cos, sin)
        q_ref[0, hd, :, 0:LANES] = (nope * scale).astype(BF16)
        q_ref[0, hd, :, LANES:2 * LANES] = (rope * scale).astype(BF16)
    base = MLA_HEADS * MLA_QK_PAD
    y0 = _dot_wt(h, w_ref[base:base + MXU_N, :])
    y1 = _dot_wt(h, w_ref[base + MXU_N:base + 2 * MXU_N, :])
    ms = (jnp.sum(y0 * y0, axis=-1, keepdims=True)
          + jnp.sum(y1 * y1, axis=-1, keepdims=True)) * (1.0 / MLA_KV_RANK)
    inv = lax.rsqrt(ms + EPS)
    ckv_ref[0, :, 0:MXU_N] = (y0 * inv * gckv_ref[:, 0:MXU_N]).astype(BF16)
    ckv_ref[0, :, MXU_N:2 * MXU_N] = (y1 * inv * gckv_ref[:, MXU_N:2 * MXU_N]).astype(BF16)
    base += MLA_KV_RANK
    yk = _dot_wt(h, w_ref[base:base + LANES, :])
    kr = _rope(_rms(yk, gkr_ref[...], MLA_ROPE), cos, sin)
    kr_ref[0] = kr.astype(BF16)


def _proj_c_kernel(h_ref, w_ref, gq_ref, gk_ref, cos_ref, sin_ref, q_ref, k_ref, v_ref):
    h = h_ref[0]
    cos = cos_ref[...]
    sin = sin_ref[...]
    scale = DIFF_QK ** -0.5 * LOG2E
    lo = (lax.broadcasted_iota(jnp.int32, (1, LANES), 1) & 32) == 0
    per = DIFF_HEADS // 2
    for c in range(3 * per):
        y = _dot_wt(h, w_ref[c * MXU_N:(c + 1) * MXU_N, :])
        for u in range(2):
            yu = y[:, u * LANES:(u + 1) * LANES]
            if c < per:
                z = _rope(_seg_rms(yu, gq_ref[...], lo), cos, sin) * scale
                q_ref[0, 2 * c + u, 0] = jnp.where(lo, z, 0.0).astype(BF16)
                q_ref[0, 2 * c + u, 1] = jnp.where(lo, 0.0, z).astype(BF16)
            elif c < 2 * per:
                z = _rope(_seg_rms(yu, gk_ref[...], lo), cos, sin)
                k_ref[0, 2 * (c - per) + u] = z.astype(BF16)
            else:
                v_ref[0, 2 * (c - 2 * per) + u] = yu.astype(BF16)


def _proj_tm(n):
    return n // 8


def _layer_weight(w, layer):
    return _resident((None,) + w.shape[1:], lambda bi, i: (layer, 0, 0))


def _proj_a_call(h, w, layer, gq, gk, cos, sin):
    b, n, d = h.shape
    tm = _proj_tm(n)
    row = lambda bi, i: (bi, i, 0)
    head = lambda bi, i: (bi, 0, i, 0)
    const = lambda bi, i: (0, 0)
    return pl.pallas_call(
        _proj_a_kernel,
        grid=(b, n // tm),
        in_specs=[pl.BlockSpec((1, tm, d), row),
                  _layer_weight(w, layer),
                  pl.BlockSpec((1, LANES), const), pl.BlockSpec((1, LANES), const),
                  pl.BlockSpec((tm, LANES), lambda bi, i: (i, 0)),
                  pl.BlockSpec((tm, LANES), lambda bi, i: (i, 0))],
        out_specs=[pl.BlockSpec((1, GQA_HEADS, tm, HEAD_DIM), head),
                   pl.BlockSpec((1, GQA_KV_HEADS, tm, HEAD_DIM), head),
                   pl.BlockSpec((1, GQA_KV_HEADS, tm, HEAD_DIM), head)],
        out_shape=[jax.ShapeDtypeStruct((b, GQA_HEADS, n, HEAD_DIM), BF16),
                   jax.ShapeDtypeStruct((b, GQA_KV_HEADS, n, HEAD_DIM), BF16),
                   jax.ShapeDtypeStruct((b, GQA_KV_HEADS, n, HEAD_DIM), BF16)],
        compiler_params=_params(("parallel", "parallel")),
        name="proj_gqa",
    )(h, w, gq, gk, cos, sin)


def _proj_b_call(h, w, layer, gq, gckv, gkr, cos, sin):
    b, n, d = h.shape
    tm = _proj_tm(n)
    row = lambda bi, i: (bi, i, 0)
    const = lambda bi, i: (0, 0)
    return pl.pallas_call(
        _proj_b_kernel,
        grid=(b, n // tm),
        in_specs=[pl.BlockSpec((1, tm, d), row),
                  _layer_weight(w, layer),
                  pl.BlockSpec((1, MLA_QK_PAD), const),
                  pl.BlockSpec((1, MLA_KV_RANK), const),
                  pl.BlockSpec((1, LANES), const),
                  pl.BlockSpec((tm, LANES), lambda bi, i: (i, 0)),
                  pl.BlockSpec((tm, LANES), lambda bi, i: (i, 0))],
        out_specs=[pl.BlockSpec((1, MLA_HEADS, tm, MLA_QK_PAD), lambda bi, i: (bi, 0, i, 0)),
                   pl.BlockSpec((1, tm, MLA_KV_RANK), row),
                   pl.BlockSpec((1, tm, LANES), row)],
        out_shape=[jax.ShapeDtypeStruct((b, MLA_HEADS, n, MLA_QK_PAD), BF16),
                   jax.ShapeDtypeStruct((b, n, MLA_KV_RANK), BF16),
                   jax.ShapeDtypeStruct((b, n, LANES), BF16)],
        compiler_params=_params(("parallel", "parallel")),
        name="proj_mla",
    )(h, w, gq, gckv, gkr, cos, sin)


def _proj_c_call(h, w, layer, gq, gk, cos, sin):
    b, n, d = h.shape
    tm = _proj_tm(n)
    row = lambda bi, i: (bi, i, 0)
    const = lambda bi, i: (0, 0)
    return pl.pallas_call(
        _proj_c_kernel,
        grid=(b, n // tm),
        in_specs=[pl.BlockSpec((1, tm, d), row),
                  _layer_weight(w, layer),
                  pl.BlockSpec((1, LANES), const), pl.BlockSpec((1, LANES), const),
                  pl.BlockSpec((tm, LANES), lambda bi, i: (i, 0)),
                  pl.BlockSpec((tm, LANES), lambda bi, i: (i, 0))],
        out_specs=[pl.BlockSpec((1, DIFF_HEADS, 2, tm, LANES), lambda bi, i: (bi, 0, 0, i, 0)),
                   pl.BlockSpec((1, DIFF_HEADS, tm, LANES), lambda bi, i: (bi, 0, i, 0)),
                   pl.BlockSpec((1, DIFF_HEADS, tm, LANES), lambda bi, i: (bi, 0, i, 0))],
        out_shape=[jax.ShapeDtypeStruct((b, DIFF_HEADS, 2, n, LANES), BF16),
                   jax.ShapeDtypeStruct((b, DIFF_HEADS, n, LANES), BF16),
                   jax.ShapeDtypeStruct((b, DIFF_HEADS, n, LANES), BF16)],
        compiler_params=_params(("parallel", "parallel")),
        name="proj_diff",
    )(h, w, gq, gk, cos, sin)


def _mla_up_kernel(ckv_ref, kr_ref, w_ref, gk_ref, k_ref, v_ref):
    ckv = ckv_ref[0]
    kr = kr_ref[0]
    per = MLA_HEADS // 2
    for c in range(2 * per):
        y = jnp.dot(ckv, w_ref[:, c * MXU_N:(c + 1) * MXU_N], preferred_element_type=F32)
        for u in range(2):
            yu = y[:, u * LANES:(u + 1) * LANES]
            if c < per:
                k_ref[0, 2 * c + u, :, 0:LANES] = _rms(yu, gk_ref[...], MLA_NOPE).astype(BF16)
                k_ref[0, 2 * c + u, :, LANES:2 * LANES] = kr
            else:
                v_ref[0, 2 * (c - per) + u] = yu.astype(BF16)


def _mla_up_call(ckv, kr, w, gk):
    b, n, r = ckv.shape
    tm = _proj_tm(n)
    row = lambda bi, i: (bi, i, 0)
    const = lambda bi, i: (0, 0)
    return pl.pallas_call(
        _mla_up_kernel,
        grid=(b, n // tm),
        in_specs=[pl.BlockSpec((1, tm, r), row),
                  pl.BlockSpec((1, tm, LANES), row),
                  _resident(w.shape, const),
                  pl.BlockSpec((1, LANES), const)],
        out_specs=[pl.BlockSpec((1, MLA_HEADS, tm, MLA_QK_PAD), lambda bi, i: (bi, 0, i, 0)),
                   pl.BlockSpec((1, MLA_HEADS, tm, LANES), lambda bi, i: (bi, 0, i, 0))],
        out_shape=[jax.ShapeDtypeStruct((b, MLA_HEADS, n, MLA_QK_PAD), BF16),
                   jax.ShapeDtypeStruct((b, MLA_HEADS, n, LANES), BF16)],
        compiler_params=_params(("parallel", "parallel")),
        name="mla_up",
    )(ckv, kr, w, gk)


def _gate_kernel(h_ref, w_ref, b_ref, o_ref, *, n_sigmoid_tiles):
    is_sigmoid = pl.program_id(2) < n_sigmoid_tiles
    h = h_ref[0]
    for c in range(o_ref.shape[-1] // MXU_N):
        cols = slice(c * MXU_N, (c + 1) * MXU_N)
        y = _dot_wt(h, w_ref[cols, :])
        t = _sigmoid(y + b_ref[:, cols])
        o_ref[0, :, cols] = jnp.where(is_sigmoid, t, y * t).astype(BF16)


def _gate_call(h, w, layer, bias):
    b, n, d = h.shape
    cols = w.shape[1]
    tm = n // 4
    tn = 3072
    assert (cols - 3 * BRANCH_WIDTH) % tn == 0 and cols % tn == 0
    return pl.pallas_call(
        functools.partial(_gate_kernel, n_sigmoid_tiles=(cols - 3 * BRANCH_WIDTH) // tn),
        grid=(b, n // tm, cols // tn),
        in_specs=[pl.BlockSpec((1, tm, d), lambda bi, i, j: (bi, i, 0)),
                  pl.BlockSpec((None, tn, d), lambda bi, i, j: (layer, j, 0)),
                  pl.BlockSpec((1, tn), lambda bi, i, j: (0, j))],
        out_specs=pl.BlockSpec((1, tm, tn), lambda bi, i, j: (bi, i, j)),
        out_shape=jax.ShapeDtypeStruct((b, n, cols), BF16),
        compiler_params=_params(("parallel", "parallel", "parallel")),
        name="gate_proj",
    )(h, w, bias)


def _flash_kernel(*refs, group, tq, chunks, diff, lam_init):
    if diff:
        bound_ref, q_ref, k_ref, v_ref, vt_ref, lamv_ref, sub_ref, o_ref = refs
    else:
        bound_ref, q_ref, k_ref, v_ref, vt_ref, o_ref = refs
    dq = q_ref.shape[-1]
    nt = (((1,), (1,)), ((), ()))

    q = q_ref[0, 0].reshape(group * tq, dq)
    half = group * tq // 2
    halves = (slice(0, half), slice(half, 2 * half))
    bound = bound_ref[0]
    safe = bound <= SAFE_BOUND_LOG2

    def finish(out):
        if diff:
            lv = lamv_ref[0]
            lam = (jnp.exp(jnp.sum(lv[0:1] * lv[1:2], axis=-1, keepdims=True))
                   - jnp.exp(jnp.sum(lv[2:3] * lv[3:4], axis=-1, keepdims=True)) + lam_init)
            o = out[:tq] - lam * out[tq:]
            o_ref[0] = (_rms(o, sub_ref[...], LANES) * (1.0 - lam_init)).astype(BF16)
        else:
            for g in range(group):
                o_ref[0, :, g * LANES:(g + 1) * LANES] = out[g * tq:(g + 1) * tq].astype(BF16)

    @pl.when(safe)
    def _():
        accs = [None, None]

        def k_dot_q(start, size):
            k = k_ref[0, 0, start:start + size, :]
            return [lax.dot_general(k, q[rows], nt, preferred_element_type=F32) for rows in halves]

        def vt_dot_pt(pts, start, size):
            vt = vt_ref[0, 0, :, start:start + size]
            for i, pt in enumerate(pts):
                t = jnp.dot(vt, pt, preferred_element_type=F32)
                accs[i] = t if accs[i] is None else accs[i] + t

        pending = None
        for start, size in chunks:
            ss = k_dot_q(start, size)
            if pending is not None:
                vt_dot_pt(*pending)
            pending = ([jnp.exp2(s - bound).astype(BF16) for s in ss], start, size)
        vt_dot_pt(*pending)
        acc = jnp.concatenate(accs, axis=1)
        finish((acc[:LANES] / acc[LANES:LANES + 1]).T)

    @pl.when(jnp.logical_not(safe))
    def _():
        m = l = acc = None
        for start, size in chunks:
            k = k_ref[0, 0, start:start + size, :]
            v = v_ref[0, 0, start:start + size, :]
            s = lax.dot_general(q, k, nt, preferred_element_type=F32)
            m_cur = jnp.max(s, axis=-1, keepdims=True)
            if m is None:
                m = m_cur
                p = jnp.exp2(s - m)
                l = jnp.sum(p, axis=-1, keepdims=True)
                acc = jnp.dot(p.astype(BF16), v, preferred_element_type=F32)
            else:
                m_new = jnp.maximum(m, m_cur)
                alpha = jnp.exp2(m - m_new)
                p = jnp.exp2(s - m_new)
                l = alpha * l + jnp.sum(p, axis=-1, keepdims=True)
                acc = alpha * acc + jnp.dot(p.astype(BF16), v, preferred_element_type=F32)
                m = m_new
        finish(acc / l)


VT_ROWS = LANES + 16


def _vt_kernel(v_ref, o_ref):
    n = v_ref.shape[2]
    for j in range(n // LANES):
        cols = slice(j * LANES, (j + 1) * LANES)
        o_ref[0, 0, 0:LANES, cols] = v_ref[0, 0, cols, :].astype(F32).T.astype(BF16)
    o_ref[0, 0, LANES:VT_ROWS, :] = jnp.ones((VT_ROWS - LANES, n), BF16)


def _vt_call(v, name):
    b, heads, n, dv = v.shape
    return pl.pallas_call(
        _vt_kernel,
        grid=(b, heads),
        in_specs=[pl.BlockSpec((1, 1, n, dv), lambda bi, h: (bi, h, 0, 0))],
        out_specs=pl.BlockSpec((1, 1, VT_ROWS, n), lambda bi, h: (bi, h, 0, 0)),
        out_shape=jax.ShapeDtypeStruct((b, heads, VT_ROWS, n), BF16),
        compiler_params=_params(("parallel", "parallel")),
        name=name,
    )(v)


def _flash_call(bound, q, k, v, vt, *, tq, q_tile0, n_q_tiles, kv_tile0, chunks,
                diff=False, lamv=None, sub=None, lam_init=0.0, name="flash"):
    b, heads, group, n, dq = q.shape
    kv_len = sum(size for _, size in chunks)
    out_w = LANES if diff else group * LANES
    in_specs = [pl.BlockSpec(memory_space=pltpu.SMEM),
                pl.BlockSpec((1, 1, group, tq, dq), lambda bi, h, i: (bi, h, 0, q_tile0 + i, 0)),
                pl.BlockSpec((1, 1, kv_len, dq), lambda bi, h, i: (bi, h, kv_tile0, 0)),
                pl.BlockSpec((1, 1, kv_len, LANES), lambda bi, h, i: (bi, h, kv_tile0, 0)),
                pl.BlockSpec((1, 1, VT_ROWS, kv_len), lambda bi, h, i: (bi, h, 0, kv_tile0))]
    args = [bound, q, k, v, vt]
    if diff:
        in_specs += [pl.BlockSpec((1, 4, DIFF_QK), lambda bi, h, i: (0, 0, 0)),
                     pl.BlockSpec((1, LANES), lambda bi, h, i: (0, 0))]
        args += [lamv, sub]
    kern = functools.partial(_flash_kernel, group=group, tq=tq, chunks=chunks, diff=diff,
                             lam_init=lam_init)
    return pl.pallas_call(
        kern,
        grid=(b, heads, n_q_tiles),
        in_specs=in_specs,
        out_specs=pl.BlockSpec((1, tq, out_w), lambda bi, h, i: (bi, i, h)),
        out_shape=jax.ShapeDtypeStruct((b, n_q_tiles * tq, BRANCH_WIDTH), BF16),
        compiler_params=_params(("parallel", "parallel", "parallel")),
        name=name,
    )(*args)


def _score_bound(q_terms, k_terms, scale):
    q2 = sum(d * jnp.max(jnp.square(g.astype(F32))) for g, d in q_terms)
    k2 = sum(d * jnp.max(jnp.square(g.astype(F32))) for g, d in k_terms)
    return (1.01 * scale * LOG2E * jnp.sqrt(q2 * k2)).reshape(1)


N_CHUNK = 512


def _merge_kernel(oa_ref, ob_ref, oc_ref, g_ref, ma_ref, mb_ref, mc_ref, w_ref, y_ref):
    ogs = [o_ref[0] * g_ref[0, :, r * BRANCH_WIDTH:(r + 1) * BRANCH_WIDTH]
           for r, o_ref in enumerate((oa_ref, ob_ref, oc_ref))]
    for c in range(y_ref.shape[-1] // N_CHUNK):
        cols = slice(c * N_CHUNK, (c + 1) * N_CHUNK)
        acc = None
        for r, m_ref in enumerate((ma_ref, mb_ref, mc_ref)):
            z = jnp.dot(ogs[r], w_ref[r, :, cols], preferred_element_type=F32)
            t = m_ref[0, :, cols].astype(F32) * z
            acc = t if acc is None else acc + t
        y_ref[0, :, cols] = acc.astype(BF16)


def _merge_call(oa, ob, oc, gm, w_br, *, tm, gm_tile0, name):
    b, rows, _ = oa.shape
    d = w_br.shape[-1]
    gate_w = 3 * BRANCH_WIDTH
    assert (3 * d) % gate_w == 0
    o_spec = pl.BlockSpec((1, tm, BRANCH_WIDTH), lambda bi, i: (bi, i, 0))
    m_specs = [pl.BlockSpec((1, tm, d), lambda bi, i, r=r: (bi, gm_tile0 + i, r)) for r in range(3)]
    return pl.pallas_call(
        _merge_kernel,
        grid=(b, rows // tm),
        in_specs=[o_spec, o_spec, o_spec,
                  pl.BlockSpec((1, tm, gate_w), lambda bi, i: (bi, gm_tile0 + i, 3 * d // gate_w)),
                  *m_specs,
                  _resident(w_br.shape, lambda bi, i: (0, 0, 0))],
        out_specs=pl.BlockSpec((1, tm, d), lambda bi, i: (bi, i, 0)),
        out_shape=jax.ShapeDtypeStruct((b, rows, d), BF16),
        compiler_params=_params(("parallel", "parallel")),
        name=name,
    )(oa, ob, oc, gm, gm, gm, gm, w_br)


def _out_kernel(y_ref, w_ref, x_ref, mod_ref, o_ref):
    y = y_ref[0]
    for c in range(o_ref.shape[-1] // N_CHUNK):
        cols = slice(c * N_CHUNK, (c + 1) * N_CHUNK)
        out = jnp.dot(y, w_ref[:, cols], preferred_element_type=F32)
        o_ref[0, :, cols] = x_ref[0, :, cols] + mod_ref[0, 2:3, cols] * out


def _out_call(y, w_out, xs, mod, *, tm, ctx_stream, name):
    b, rows, d = xs.shape
    return pl.pallas_call(
        _out_kernel,
        grid=(b, rows // tm),
        in_specs=[pl.BlockSpec((1, tm, d), lambda bi, i: (bi, i, 0)),
                  _resident(w_out.shape, lambda bi, i: (0, 0)),
                  pl.BlockSpec((1, tm, d), lambda bi, i: (bi, i, 0)),
                  pl.BlockSpec((1, 3, d), lambda bi, i: (b if ctx_stream else bi, 0, 0))],
        out_specs=pl.BlockSpec((1, tm, d), lambda bi, i: (bi, i, 0)),
        out_shape=jax.ShapeDtypeStruct((b, rows, d), F32),
        compiler_params=_params(("parallel", "parallel")),
        name=name,
    )(y, w_out, xs, mod)


def _rope_tables(n_lat, n_ctx, rot_dim, src):
    axis_dim = rot_dim // 2
    t = jnp.arange(n_lat, dtype=jnp.int32)
    pos_row = (t // GRID_W).astype(F32)
    pos_col = (t % GRID_W).astype(F32)
    inv_freq = ROPE_THETA ** (-jnp.arange(0, axis_dim, 2, dtype=F32) / axis_dim)
    ang_r = pos_row[:, None] * inv_freq
    ang_c = pos_col[:, None] * inv_freq
    ang = jnp.concatenate([ang_r, ang_r, ang_c, ang_c], axis=-1)
    cos, sin = jnp.cos(ang), jnp.sin(ang)
    lane = jnp.arange(rot_dim)
    sign = jnp.where((lane % axis_dim) < axis_dim // 2, -1.0, 1.0).astype(F32)
    pad = jnp.asarray(src < 0)
    lanes = jnp.asarray(np.maximum(src, 0) % rot_dim)
    cos = jnp.where(pad, 1.0, jnp.take(cos, lanes, axis=1))
    sin = jnp.where(pad, 0.0, jnp.take(sin * sign, lanes, axis=1))
    cos = jnp.concatenate([cos, jnp.ones((n_ctx, LANES), F32)], axis=0)
    sin = jnp.concatenate([sin, jnp.zeros((n_ctx, LANES), F32)], axis=0)
    return cos, sin


PREP_MAX_ROWS = 256
_IDENT = np.arange(LANES)


def _prep_runs():
    o_gk = GQA_HEADS * HEAD_DIM
    o_gv = o_gk + GQA_KV_HEADS * HEAD_DIM
    o_mq = o_gv + GQA_KV_HEADS * HEAD_DIM
    o_ckv = o_mq + MLA_HEADS * (MLA_NOPE + MLA_ROPE)
    o_kr = o_ckv + MLA_KV_RANK
    o_dq = o_kr + MLA_ROPE
    o_dv = o_dq + 2 * DIFF_HEADS * 2 * DIFF_QK
    o_gate = o_dv + DIFF_HEADS * LANES
    o_merge = o_gate + 3 * BRANCH_WIDTH
    units = []
    for u in range(o_gv // LANES):
        units.append((0, u * LANES, u * LANES, _lane_src_a()))
    for u in range((o_mq - o_gv) // LANES):
        units.append((0, o_gv + u * LANES, o_gv + u * LANES, _IDENT))
    for hd in range(MLA_HEADS):
        base = o_mq + hd * (MLA_NOPE + MLA_ROPE)
        units.append((1, hd * MLA_QK_PAD, base, _IDENT))
        units.append((1, hd * MLA_QK_PAD + LANES, base + MLA_NOPE, _lane_src_b()))
    for u in range(MLA_KV_RANK // LANES):
        units.append((1, MLA_HEADS * MLA_QK_PAD + u * LANES, o_ckv + u * LANES, _IDENT))
    units.append((1, MLA_HEADS * MLA_QK_PAD + MLA_KV_RANK, o_kr, _lane_src_b()))
    for u in range((o_dv - o_dq) // LANES):
        units.append((2, u * LANES, o_dq + u * LANES, _lane_src_c()))
    for u in range((o_gate - o_dv) // LANES):
        units.append((2, o_dv - o_dq + u * LANES, o_dv + u * LANES, _IDENT))
    for u in range(3 * D_MODEL // LANES):
        units.append((3, u * LANES, o_merge + u * LANES, _IDENT))
    for u in range(3 * BRANCH_WIDTH // LANES):
        units.append((3, 3 * D_MODEL + u * LANES, o_gate + u * LANES, _IDENT))

    runs = []
    for out, dst0, base, lane_src in units:
        for t, idx in enumerate(lane_src):
            src = None if idx < 0 else base + int(idx)
            last = runs[-1] if runs else None
            if (last is not None and last[0] == out and last[1] + last[3] == dst0 + t
                    and last[3] < PREP_MAX_ROWS
                    and ((src is None and last[2] is None)
                         or (src is not None and last[2] is not None and last[2] + last[3] == src))):
                runs[-1] = (out, last[1], last[2], last[3] + 1)
            else:
                runs.append((out, dst0 + t, src, 1))
    assert all(dst % 16 == 0 and rows % 16 == 0 and (src is None or src % 16 == 0)
               for _, dst, src, rows in runs)
    return runs


def _prep_kernel(w_ref, *o_refs, runs):
    for out, dst, src, rows in runs:
        if src is None:
            o_refs[out][0, dst:dst + rows, :] = jnp.zeros((rows, o_refs[out].shape[-1]), BF16)
        else:
            o_refs[out][0, dst:dst + rows, :] = w_ref[0, src:src + rows, :].astype(BF16)


def _prep_call(w_in_t):
    depth, total, d = w_in_t.shape
    runs = _prep_runs()
    heights = [max(dst + rows for out, dst, _, rows in runs if out == o) for o in range(4)]
    td = 256
    return pl.pallas_call(
        functools.partial(_prep_kernel, runs=runs),
        grid=(depth, d // td),
        in_specs=[pl.BlockSpec((1, total, td), lambda l, i: (l, 0, i))],
        out_specs=[pl.BlockSpec((1, ht, td), lambda l, i: (l, 0, i)) for ht in heights],
        out_shape=[jax.ShapeDtypeStruct((depth, ht, d), BF16) for ht in heights],
        compiler_params=_params(("parallel", "parallel")),
        name="weight_layout",
    )(w_in_t)


def kernel(x, c, ctx, c_ctx, norm_w, w_ada, b_ada, w_in, b_merge, gqa_q_norm, gqa_k_norm, mla_q_nope_norm, mla_q_rope_norm, mla_kv_norm, mla_w_uk, mla_w_uv, mla_k_nope_norm, mla_k_rope_norm, diff_q_norm, diff_k_norm, diff_lambda_q1, diff_lambda_k1, diff_lambda_q2, diff_lambda_k2, diff_subln, w_br_gqa, w_br_mla, w_br_diff, w_out):
    b, n_lat, d = x.shape
    n_ctx = ctx.shape[1]
    n = n_lat + n_ctx
    depth = w_in.shape[0]
    assert n_lat % 2048 == 0 and n_ctx == 256 and n % 8 == 0 and b < 8 and d == D_MODEL

    c_all = jnp.concatenate([c, c_ctx[None], jnp.zeros((8 - b - 1, d), F32)], axis=0)
    mod = _ada_call(c_all, w_ada, b_ada).reshape(depth, 8, 3, d)

    w_a, w_b, w_c, w_g = _prep_call(jnp.swapaxes(w_in, 1, 2))
    w_ukv = jnp.concatenate([mla_w_uk, mla_w_uv], axis=-1).astype(BF16)
    w_br = jnp.stack([w_br_gqa, w_br_mla, w_br_diff], axis=1).astype(BF16)
    w_o = w_out.astype(BF16)
    bias_g = jnp.concatenate([b_merge, jnp.zeros((depth, 3 * BRANCH_WIDTH), F32)], axis=-1)
    src_a, src_b, src_c = _lane_src_a(), _lane_src_b(), _lane_src_c()
    g_aq = _permute_lanes(gqa_q_norm, src_a)
    g_ak = _permute_lanes(gqa_k_norm, src_a)
    g_bq = jnp.concatenate([mla_q_nope_norm, _permute_lanes(mla_q_rope_norm, src_b)], axis=-1)
    g_kr = _permute_lanes(mla_k_rope_norm, src_b)
    g_dq = _permute_lanes(jnp.tile(diff_q_norm, (1, 2)), src_c)
    g_dk = _permute_lanes(jnp.tile(diff_k_norm, (1, 2)), src_c)
    lamv = jnp.stack([diff_lambda_q1, diff_lambda_k1, diff_lambda_q2, diff_lambda_k2], axis=1)

    cos_a, sin_a = _rope_tables(n_lat, n_ctx, HEAD_DIM, src_a)
    cos_b, sin_b = _rope_tables(n_lat, n_ctx, MLA_ROPE, src_b)
    cos_c, sin_c = _rope_tables(n_lat, n_ctx, DIFF_QK, src_c)

    lat_chunks = tuple((s, 1024) for s in range(0, n_lat, 1024)) + ((n_lat, n_ctx),)
    assert sum(size for _, size in lat_chunks) == n
    ctx_chunks = ((0, n_ctx),)
    ctx_tile = n_lat // n_ctx

    for l in range(depth):
        last = l == depth - 1
        lam_init = 0.8 - 0.6 * math.exp(-0.3 * l)
        h = _h_call(x, ctx, norm_w[l][None], mod[l])

        qa, ka, va = _proj_a_call(h, w_a, l, g_aq[l][None], g_ak[l][None], cos_a, sin_a)
        qb, ckv, kr = _proj_b_call(h, w_b, l, g_bq[l][None], mla_kv_norm[l][None], g_kr[l][None],
                                   cos_b, sin_b)
        qc, kc, vc = _proj_c_call(h, w_c, l, g_dq[l][None], g_dk[l][None], cos_c, sin_c)
        gm = _gate_call(h, w_g, l, bias_g[l][None])
        kb, vb = _mla_up_call(ckv, kr, w_ukv[l], mla_k_nope_norm[l][None])

        qa = qa.reshape(b, GQA_KV_HEADS, GQA_GROUP, n, HEAD_DIM)
        qb = qb.reshape(b, MLA_HEADS, 1, n, MLA_QK_PAD)
        diff_kw = dict(diff=True, lamv=lamv[l][None], sub=diff_subln[l][None], lam_init=lam_init)

        ba = _score_bound([(gqa_q_norm[l], HEAD_DIM)], [(gqa_k_norm[l], HEAD_DIM)], HEAD_DIM ** -0.5)
        bb = _score_bound([(mla_q_nope_norm[l], MLA_NOPE), (mla_q_rope_norm[l], MLA_ROPE)],
                          [(mla_k_nope_norm[l], MLA_NOPE), (mla_k_rope_norm[l], MLA_ROPE)],
                          (MLA_NOPE + MLA_ROPE) ** -0.5)
        bc = _score_bound([(diff_q_norm[l], DIFF_QK)], [(diff_k_norm[l], DIFF_QK)], DIFF_QK ** -0.5)
        vta = _vt_call(va, "vt_gqa")
        vtb = _vt_call(vb, "vt_mla")
        vtc = _vt_call(vc, "vt_diff")
        if not last:
            ctx_kw = dict(tq=n_ctx, q_tile0=ctx_tile, n_q_tiles=1, kv_tile0=ctx_tile, chunks=ctx_chunks)
            oa = _flash_call(ba, qa, ka, va, vta, name="flash_gqa_ctx", **ctx_kw)
            ob = _flash_call(bb, qb, kb, vb, vtb, name="flash_mla_ctx", **ctx_kw)
            oc = _flash_call(bc, qc, kc, vc, vtc, name="flash_diff_ctx", **ctx_kw, **diff_kw)
            y = _merge_call(oa, ob, oc, gm, w_br[l], tm=n_ctx, gm_tile0=ctx_tile, name="branch_merge_ctx")
            ctx = _out_call(y, w_o[l], ctx, mod[l], tm=n_ctx, ctx_stream=True, name="out_proj_ctx")

        lat_kw = dict(q_tile0=0, kv_tile0=0, chunks=lat_chunks)
        oa = _flash_call(ba, qa, ka, va, vta, tq=512, n_q_tiles=n_lat // 512, name="flash_gqa", **lat_kw)
        ob = _flash_call(bb, qb, kb, vb, vtb, tq=2048, n_q_tiles=n_lat // 2048, name="flash_mla", **lat_kw)
        oc = _flash_call(bc, qc, kc, vc, vtc, tq=1024, n_q_tiles=n_lat // 1024, name="flash_diff",
                         **lat_kw, **diff_kw)
        y = _merge_call(oa, ob, oc, gm, w_br[l], tm=512, gm_tile0=0, name="branch_merge")
        x = _out_call(y, w_o[l], x, mod[l], tm=512, ctx_stream=False, name="out_proj")
    return x
```

```python
import functools
import math

import jax
import jax.numpy as jnp
import numpy as np
from jax import lax
from jax.experimental import pallas as pl
from jax.experimental.pallas import tpu as pltpu

F32 = jnp.float32
BF16 = jnp.bfloat16

GRID_W = 64
ROPE_THETA = 10000.0
EPS = 1e-6

D_MODEL = 2048
HEAD_DIM = 128
BRANCH_WIDTH = 1024
GQA_HEADS = 8
GQA_KV_HEADS = 2
GQA_GROUP = GQA_HEADS // GQA_KV_HEADS
MLA_HEADS = 8
MLA_NOPE = 128
MLA_ROPE = 64
MLA_QK_PAD = 256
MLA_KV_RANK = 512
DIFF_HEADS = 8
DIFF_QK = 64

LANES = 128
MXU_N = 256
VMEM_LIMIT = 56 * 1024 * 1024
LOG2E = 1.4426950408889634
SAFE_BOUND_LOG2 = 57.0


def _params(sem, vmem=VMEM_LIMIT):
    return pltpu.CompilerParams(dimension_semantics=sem, vmem_limit_bytes=vmem)


def _resident(shape, index_map):
    return pl.BlockSpec(shape, index_map, pipeline_mode=pl.Buffered(1))


def _rms(y, gain, n):
    ms = jnp.sum(y * y, axis=-1, keepdims=True) * (1.0 / n)
    return y * lax.rsqrt(ms + EPS) * gain


def _seg_rms(y, gain, lo):
    ss = y * y
    s_lo = jnp.sum(jnp.where(lo, ss, 0.0), axis=-1, keepdims=True)
    s_hi = jnp.sum(jnp.where(lo, 0.0, ss), axis=-1, keepdims=True)
    ms = jnp.where(lo, s_lo, s_hi) * (1.0 / DIFF_QK)
    return y * lax.rsqrt(ms + EPS) * gain


def _rope(y, cos, sin_signed):
    return y * cos + pltpu.roll(y, LANES // 2, 1) * sin_signed


def _lane_src_a():
    e = np.arange(HEAD_DIM // 4)
    return np.concatenate([e, 64 + e, 32 + e, 96 + e])


def _lane_src_b():
    e = np.arange(MLA_ROPE // 4)
    pad = -np.ones(32, np.int64)
    return np.concatenate([e, 32 + e, pad, 16 + e, 48 + e, pad])


def _lane_src_c():
    e = np.arange(DIFF_QK // 4)
    x1 = [m * DIFF_QK + a * 32 + e for m in (0, 1) for a in (0, 1)]
    x2 = [m * DIFF_QK + a * 32 + 16 + e for m in (0, 1) for a in (0, 1)]
    return np.concatenate(x1 + x2)


def _permute_lanes(vec, src):
    out = jnp.take(vec, jnp.asarray(np.maximum(src, 0)), axis=-1)
    return jnp.where(jnp.asarray(src >= 0), out, 0.0)


def _sigmoid(z):
    return 1.0 / (1.0 + jnp.exp(-z))


def _dot_wt(x, wt):
    return lax.dot_general(x, wt, (((1,), (1,)), ((), ())), preferred_element_type=F32)


def _ada_kernel(c_ref, w_ref, b_ref, o_ref):
    c = c_ref[...]
    a = (c * _sigmoid(c)).astype(BF16)
    o_ref[0] = jnp.dot(a, w_ref[0].astype(BF16), preferred_element_type=F32) + b_ref[0]


def _ada_call(c_all, w_ada, b_ada):
    depth, d, n3 = w_ada.shape
    tn = 512
    return pl.pallas_call(
        _ada_kernel,
        grid=(depth, n3 // tn),
        in_specs=[pl.BlockSpec((8, d), lambda l, j: (0, 0)),
                  pl.BlockSpec((1, d, tn), lambda l, j: (l, 0, j)),
                  pl.BlockSpec((1, 1, tn), lambda l, j: (l, 0, j))],
        out_specs=pl.BlockSpec((1, 8, tn), lambda l, j: (l, 0, j)),
        out_shape=jax.ShapeDtypeStruct((depth, 8, n3), F32),
        compiler_params=_params(("parallel", "parallel")),
        name="ada_mod",
    )(c_all, w_ada, b_ada.reshape(depth, 1, n3))


def _h_kernel(x_ref, ctx_ref, nw_ref, mod_ref, modc_ref, h_ref, *, lat_tiles):
    def norm_mod(src_ref, m_ref):
        x = src_ref[0]
        ms = jnp.mean(x * x, axis=-1, keepdims=True)
        y = x * lax.rsqrt(ms + EPS) * nw_ref[...]
        h_ref[0] = (y * (1.0 + m_ref[0, 1:2, :]) + m_ref[0, 0:1, :]).astype(BF16)

    t = pl.program_id(1)

    @pl.when(t < lat_tiles)
    def _():
        norm_mod(x_ref, mod_ref)

    @pl.when(t >= lat_tiles)
    def _():
        norm_mod(ctx_ref, modc_ref)


def _h_call(x, ctx, norm_w, mod):
    b, n_lat, d = x.shape
    tr = ctx.shape[1]
    lat_tiles = n_lat // tr
    return pl.pallas_call(
        functools.partial(_h_kernel, lat_tiles=lat_tiles),
        grid=(b, lat_tiles + 1),
        in_specs=[pl.BlockSpec((1, tr, d), lambda bi, t: (bi, jnp.minimum(t, lat_tiles - 1), 0)),
                  pl.BlockSpec((1, tr, d), lambda bi, t: (bi, 0, 0)),
                  pl.BlockSpec((1, d), lambda bi, t: (0, 0)),
                  pl.BlockSpec((1, 3, d), lambda bi, t: (bi, 0, 0)),
                  pl.BlockSpec((1, 3, d), lambda bi, t: (b, 0, 0))],
        out_specs=pl.BlockSpec((1, tr, d), lambda bi, t: (bi, t, 0)),
        out_shape=jax.ShapeDtypeStruct((b, n_lat + tr, d), BF16),
        compiler_params=_params(("parallel", "parallel")),
        name="norm_mod",
    )(x, ctx, norm_w, mod, mod)


def _proj_a_kernel(h_ref, w_ref, gq_ref, gk_ref, cos_ref, sin_ref, q_ref, k_ref, v_ref):
    h = h_ref[0]
    cos = cos_ref[...]
    sin = sin_ref[...]
    scale = HEAD_DIM ** -0.5 * LOG2E
    n_q = GQA_HEADS * HEAD_DIM // MXU_N
    for c in range(n_q + 2):
        y = _dot_wt(h, w_ref[c * MXU_N:(c + 1) * MXU_N, :])
        for u in range(2):
            yu = y[:, u * LANES:(u + 1) * LANES]
            if c < n_q:
                z = _rope(_rms(yu, gq_ref[...], HEAD_DIM), cos, sin)
                q_ref[0, 2 * c + u] = (z * scale).astype(BF16)
            elif c == n_q:
                z = _rope(_rms(yu, gk_ref[...], HEAD_DIM), cos, sin)
                k_ref[0, u] = z.astype(BF16)
            else:
                v_ref[0, u] = yu.astype(BF16)


def _proj_b_kernel(h_ref, w_ref, wup_ref, gq_ref, gckv_ref, gkr_ref, gkn_ref, cos_ref, sin_ref,
                   q_ref, k_ref, v_ref):
    h = h_ref[0]
    cos = cos_ref[...]
    sin = sin_ref[...]
    scale = (MLA_NOPE + MLA_ROPE) ** -0.5 * LOG2E
    base = MLA_HEADS * MLA_QK_PAD
    y0 = _dot_wt(h, w_ref[base:base + MXU_N, :])
    y1 = _dot_wt(h, w_ref[base + MXU_N:base + 2 * MXU_N, :])
    ms = (jnp.sum(y0 * y0, axis=-1, keepdims=True)
          + jnp.sum(y1 * y1, axis=-1, keepdims=True)) * (1.0 / MLA_KV_RANK)
    inv = lax.rsqrt(ms + EPS)
    ckv = jnp.concatenate([(y0 * inv * gckv_ref[:, 0:MXU_N]).astype(BF16),
                           (y1 * inv * gckv_ref[:, MXU_N:2 * MXU_N]).astype(BF16)], axis=1)
    base += MLA_KV_RANK
    yk = _dot_wt(h, w_ref[base:base + LANES, :])
    kr = _rope(_rms(yk, gkr_ref[...], MLA_ROPE), cos, sin).astype(BF16)
    for hd in range(MLA_HEADS):
        y = _dot_wt(h, w_ref[hd * MLA_QK_PAD:(hd + 1) * MLA_QK_PAD, :])
        nope = _rms(y[:, :LANES], gq_ref[:, :LANES], MLA_NOPE)
        rope = _rope(_rms(y[:, LANES:], gq_ref[:, LANES:], MLA_ROPE), cos, sin)
        q_ref[0, hd, :, 0:LANES] = (nope * scale).astype(BF16)
        q_ref[0, hd, :, LANES:2 * LANES] = (rope * scale).astype(BF16)
    per = MLA_HEADS // 2
    for c in range(2 * per):
        y = jnp.dot(ckv, wup_ref[:, c * MXU_N:(c + 1) * MXU_N], preferred_element_type=F32)
        for u in range(2):
            yu = y[:, u * LANES:(u + 1) * LANES]
            if c < per:
                k_ref[0, 2 * c + u, :, 0:LANES] = _rms(yu, gkn_ref[...], MLA_NOPE).astype(BF16)
                k_ref[0, 2 * c + u, :, LANES:2 * LANES] = kr
            else:
                v_ref[0, 2 * (c - per) + u] = yu.astype(BF16)


def _proj_c_kernel(h_ref, w_ref, gq_ref, gk_ref, cos_ref, sin_ref, q_ref, k_ref, v_ref):
    h = h_ref[0]
    cos = cos_ref[...]
    sin = sin_ref[...]
    scale = DIFF_QK ** -0.5 * LOG2E
    lo = (lax.broadcasted_iota(jnp.int32, (1, LANES), 1) & 32) == 0
    per = DIFF_HEADS // 2
    for c in range(3 * per):
        y = _dot_wt(h, w_ref[c * MXU_N:(c + 1) * MXU_N, :])
        for u in range(2):
            yu = y[:, u * LANES:(u + 1) * LANES]
            if c < per:
                z = _rope(_seg_rms(yu, gq_ref[...], lo), cos, sin) * scale
                q_ref[0, 2 * c + u, 0] = jnp.where(lo, z, 0.0).astype(BF16)
                q_ref[0, 2 * c + u, 1] = jnp.where(lo, 0.0, z).astype(BF16)
            elif c < 2 * per:
                z = _rope(_seg_rms(yu, gk_ref[...], lo), cos, sin)
                k_ref[0, 2 * (c - per) + u] = z.astype(BF16)
            else:
                v_ref[0, 2 * (c - 2 * per) + u] = yu.astype(BF16)


def _proj_tm(n):
    return n // 8


def _layer_weight(w, layer):
    return _resident((None,) + w.shape[1:], lambda bi, i: (layer, 0, 0))


def _proj_a_call(h, w, layer, gq, gk, cos, sin):
    b, n, d = h.shape
    tm = _proj_tm(n)
    row = lambda bi, i: (bi, i, 0)
    head = lambda bi, i: (bi, 0, i, 0)
    const = lambda bi, i: (0, 0)
    return pl.pallas_call(
        _proj_a_kernel,
        grid=(b, n // tm),
        in_specs=[pl.BlockSpec((1, tm, d), row),
                  _layer_weight(w, layer),
                  pl.BlockSpec((1, LANES), const), pl.BlockSpec((1, LANES), const),
                  pl.BlockSpec((tm, LANES), lambda bi, i: (i, 0)),
                  pl.BlockSpec((tm, LANES), lambda bi, i: (i, 0))],
        out_specs=[pl.BlockSpec((1, GQA_HEADS, tm, HEAD_DIM), head),
                   pl.BlockSpec((1, GQA_KV_HEADS, tm, HEAD_DIM), head),
                   pl.BlockSpec((1, GQA_KV_HEADS, tm, HEAD_DIM), head)],
        out_shape=[jax.ShapeDtypeStruct((b, GQA_HEADS, n, HEAD_DIM), BF16),
                   jax.ShapeDtypeStruct((b, GQA_KV_HEADS, n, HEAD_DIM), BF16),
                   jax.ShapeDtypeStruct((b, GQA_KV_HEADS, n, HEAD_DIM), BF16)],
        compiler_params=_params(("parallel", "parallel")),
        name="proj_gqa",
    )(h, w, gq, gk, cos, sin)


def _proj_b_call(h, w, w_up, layer, gq, gckv, gkr, gkn, cos, sin):
    b, n, d = h.shape
    tm = _proj_tm(n)
    row = lambda bi, i: (bi, i, 0)
    head = lambda bi, i: (bi, 0, i, 0)
    const = lambda bi, i: (0, 0)
    return pl.pallas_call(
        _proj_b_kernel,
        grid=(b, n // tm),
        in_specs=[pl.BlockSpec((1, tm, d), row),
                  _layer_weight(w, layer),
                  _layer_weight(w_up, layer),
                  pl.BlockSpec((1, MLA_QK_PAD), const),
                  pl.BlockSpec((1, MLA_KV_RANK), const),
                  pl.BlockSpec((1, LANES), const),
                  pl.BlockSpec((1, LANES), const),
                  pl.BlockSpec((tm, LANES), lambda bi, i: (i, 0)),
                  pl.BlockSpec((tm, LANES), lambda bi, i: (i, 0))],
        out_specs=[pl.BlockSpec((1, MLA_HEADS, tm, MLA_QK_PAD), head),
                   pl.BlockSpec((1, MLA_HEADS, tm, MLA_QK_PAD), head),
                   pl.BlockSpec((1, MLA_HEADS, tm, LANES), head)],
        out_shape=[jax.ShapeDtypeStruct((b, MLA_HEADS, n, MLA_QK_PAD), BF16),
                   jax.ShapeDtypeStruct((b, MLA_HEADS, n, MLA_QK_PAD), BF16),
                   jax.ShapeDtypeStruct((b, MLA_HEADS, n, LANES), BF16)],
        compiler_params=_params(("parallel", "parallel")),
        name="proj_mla",
    )(h, w, w_up, gq, gckv, gkr, gkn, cos, sin)


def _proj_c_call(h, w, layer, gq, gk, cos, sin):
    b, n, d = h.shape
    tm = _proj_tm(n)
    row = lambda bi, i: (bi, i, 0)
    const = lambda bi, i: (0, 0)
    return pl.pallas_call(
        _proj_c_kernel,
        grid=(b, n // tm),
        in_specs=[pl.BlockSpec((1, tm, d), row),
                  _layer_weight(w, layer),
                  pl.BlockSpec((1, LANES), const), pl.BlockSpec((1, LANES), const),
                  pl.BlockSpec((tm, LANES), lambda bi, i: (i, 0)),
                  pl.BlockSpec((tm, LANES), lambda bi, i: (i, 0))],
        out_specs=[pl.BlockSpec((1, DIFF_HEADS, 2, tm, LANES), lambda bi, i: (bi, 0, 0, i, 0)),
                   pl.BlockSpec((1, DIFF_HEADS, tm, LANES), lambda bi, i: (bi, 0, i, 0)),
                   pl.BlockSpec((1, DIFF_HEADS, tm, LANES), lambda bi, i: (bi, 0, i, 0))],
        out_shape=[jax.ShapeDtypeStruct((b, DIFF_HEADS, 2, n, LANES), BF16),
                   jax.ShapeDtypeStruct((b, DIFF_HEADS, n, LANES), BF16),
                   jax.ShapeDtypeStruct((b, DIFF_HEADS, n, LANES), BF16)],
        compiler_params=_params(("parallel", "parallel")),
        name="proj_diff",
    )(h, w, gq, gk, cos, sin)


def _gate_kernel(h_ref, w_ref, b_ref, o_ref, *, n_sigmoid_tiles):
    is_sigmoid = pl.program_id(1) < n_sigmoid_tiles
    h = h_ref[0]
    for c in range(o_ref.shape[-1] // MXU_N):
        cols = slice(c * MXU_N, (c + 1) * MXU_N)
        y = _dot_wt(h, w_ref[cols, :])
        t = _sigmoid(y + b_ref[:, cols])
        o_ref[0, :, cols] = jnp.where(is_sigmoid, t, y * t).astype(BF16)


def _gate_call(h, w, layer, bias):
    b, n, d = h.shape
    cols = w.shape[1]
    tm = n // 8
    tn = 3072
    assert (cols - 3 * BRANCH_WIDTH) % tn == 0 and cols % tn == 0
    return pl.pallas_call(
        functools.partial(_gate_kernel, n_sigmoid_tiles=(cols - 3 * BRANCH_WIDTH) // tn),
        grid=(b, cols // tn, n // tm),
        in_specs=[pl.BlockSpec((1, tm, d), lambda bi, j, i: (bi, i, 0)),
                  pl.BlockSpec((None, tn, d), lambda bi, j, i: (layer, j, 0)),
                  pl.BlockSpec((1, tn), lambda bi, j, i: (0, j))],
        out_specs=pl.BlockSpec((1, tm, tn), lambda bi, j, i: (bi, i, j)),
        out_shape=jax.ShapeDtypeStruct((b, n, cols), BF16),
        compiler_params=_params(("parallel", "parallel", "parallel")),
        name="gate_proj",
    )(h, w, bias)


def _flash_kernel(*refs, group, tq, chunks, diff, lam_init):
    if diff:
        bound_ref, q_ref, k_ref, v_ref, vt_ref, lamv_ref, sub_ref, o_ref = refs
    else:
        bound_ref, q_ref, k_ref, v_ref, vt_ref, o_ref = refs
    dq = q_ref.shape[-1]
    nt = (((1,), (1,)), ((), ()))

    q = q_ref[0, 0].reshape(group * tq, dq)
    half = group * tq // 2
    halves = (slice(0, half), slice(half, 2 * half))
    bound = bound_ref[0]
    safe = bound <= SAFE_BOUND_LOG2

    def finish(out):
        if diff:
            lv = lamv_ref[0]
            lam = (jnp.exp(jnp.sum(lv[0:1] * lv[1:2], axis=-1, keepdims=True))
                   - jnp.exp(jnp.sum(lv[2:3] * lv[3:4], axis=-1, keepdims=True)) + lam_init)
            o = out[:tq] - lam * out[tq:]
            o_ref[0] = (_rms(o, sub_ref[...], LANES) * (1.0 - lam_init)).astype(BF16)
        else:
            for g in range(group):
                o_ref[0, :, g * LANES:(g + 1) * LANES] = out[g * tq:(g + 1) * tq].astype(BF16)

    @pl.when(safe)
    def _():
        accs = [None, None]

        def k_dot_q(start, size):
            k = k_ref[0, 0, start:start + size, :]
            return [lax.dot_general(k, q[rows], nt, preferred_element_type=F32) for rows in halves]

        def vt_dot_pt(pts, start, size):
            vt = vt_ref[0, 0, :, start:start + size]
            for i, pt in enumerate(pts):
                t = jnp.dot(vt, pt, preferred_element_type=F32)
                accs[i] = t if accs[i] is None else accs[i] + t

        pending = None
        for start, size in chunks:
            ss = k_dot_q(start, size)
            if pending is not None:
                vt_dot_pt(*pending)
            pending = ([jnp.exp2(s - bound).astype(BF16) for s in ss], start, size)
        vt_dot_pt(*pending)
        acc = jnp.concatenate(accs, axis=1)
        finish((acc[:LANES] / acc[LANES:LANES + 1]).T)

    @pl.when(jnp.logical_not(safe))
    def _():
        m = l = acc = None
        for start, size in chunks:
            k = k_ref[0, 0, start:start + size, :]
            v = v_ref[0, 0, start:start + size, :]
            s = lax.dot_general(q, k, nt, preferred_element_type=F32)
            m_cur = jnp.max(s, axis=-1, keepdims=True)
            if m is None:
                m = m_cur
                p = jnp.exp2(s - m)
                l = jnp.sum(p, axis=-1, keepdims=True)
                acc = jnp.dot(p.astype(BF16), v, preferred_element_type=F32)
            else:
                m_new = jnp.maximum(m, m_cur)
                alpha = jnp.exp2(m - m_new)
                p = jnp.exp2(s - m_new)
                l = alpha * l + jnp.sum(p, axis=-1, keepdims=True)
                acc = alpha * acc + jnp.dot(p.astype(BF16), v, preferred_element_type=F32)
                m = m_new
        finish(acc / l)


VT_ROWS = LANES + 16


def _vt_kernel(v_ref, o_ref):
    n = v_ref.shape[2]
    for j in range(n // LANES):
        cols = slice(j * LANES, (j + 1) * LANES)
        o_ref[0, 0, 0:LANES, cols] = v_ref[0, 0, cols, :].T
    o_ref[0, 0, LANES:VT_ROWS, :] = jnp.ones((VT_ROWS - LANES, n), BF16)


def _vt_call(v, name):
    b, heads, n, dv = v.shape
    return pl.pallas_call(
        _vt_kernel,
        grid=(b, heads),
        in_specs=[pl.BlockSpec((1, 1, n, dv), lambda bi, h: (bi, h, 0, 0))],
        out_specs=pl.BlockSpec((1, 1, VT_ROWS, n), lambda bi, h: (bi, h, 0, 0)),
        out_shape=jax.ShapeDtypeStruct((b, heads, VT_ROWS, n), BF16),
        compiler_params=_params(("parallel", "parallel")),
        name=name,
    )(v)


def _flash_call(bound, q, k, v, vt, *, tq, q_tile0, n_q_tiles, kv_tile0, chunks,
                diff=False, lamv=None, sub=None, lam_init=0.0, name="flash"):
    b, heads, group, n, dq = q.shape
    kv_len = sum(size for _, size in chunks)
    out_w = LANES if diff else group * LANES
    in_specs = [pl.BlockSpec(memory_space=pltpu.SMEM),
                pl.BlockSpec((1, 1, group, tq, dq), lambda bi, h, i: (bi, h, 0, q_tile0 + i, 0)),
                pl.BlockSpec((1, 1, kv_len, dq), lambda bi, h, i: (bi, h, kv_tile0, 0)),
                pl.BlockSpec((1, 1, kv_len, LANES), lambda bi, h, i: (bi, h, kv_tile0, 0)),
                pl.BlockSpec((1, 1, VT_ROWS, kv_len), lambda bi, h, i: (bi, h, 0, kv_tile0))]
    args = [bound, q, k, v, vt]
    if diff:
        in_specs += [pl.BlockSpec((1, 4, DIFF_QK), lambda bi, h, i: (0, 0, 0)),
                     pl.BlockSpec((1, LANES), lambda bi, h, i: (0, 0))]
        args += [lamv, sub]
    kern = functools.partial(_flash_kernel, group=group, tq=tq, chunks=chunks, diff=diff,
                             lam_init=lam_init)
    return pl.pallas_call(
        kern,
        grid=(b, heads, n_q_tiles),
        in_specs=in_specs,
        out_specs=pl.BlockSpec((1, tq, out_w), lambda bi, h, i: (bi, i, h)),
        out_shape=jax.ShapeDtypeStruct((b, n_q_tiles * tq, BRANCH_WIDTH), BF16),
        compiler_params=_params(("parallel", "parallel", "parallel")),
        name=name,
    )(*args)


def _score_bound(q_terms, k_terms, scale):
    q2 = sum(d * jnp.max(jnp.square(g.astype(F32))) for g, d in q_terms)
    k2 = sum(d * jnp.max(jnp.square(g.astype(F32))) for g, d in k_terms)
    return (1.01 * scale * LOG2E * jnp.sqrt(q2 * k2)).reshape(1)


N_CHUNK = 512


def _merge_kernel(oa_ref, ob_ref, oc_ref, g_ref, ma_ref, mb_ref, mc_ref, w_ref, y_ref):
    ogs = [o_ref[0] * g_ref[0, :, r * BRANCH_WIDTH:(r + 1) * BRANCH_WIDTH]
           for r, o_ref in enumerate((oa_ref, ob_ref, oc_ref))]
    for c in range(y_ref.shape[-1] // N_CHUNK):
        cols = slice(c * N_CHUNK, (c + 1) * N_CHUNK)
        acc = None
        for r, m_ref in enumerate((ma_ref, mb_ref, mc_ref)):
            z = jnp.dot(ogs[r], w_ref[r, :, cols], preferred_element_type=F32)
            t = m_ref[0, :, cols].astype(F32) * z
            acc = t if acc is None else acc + t
        y_ref[0, :, cols] = acc.astype(BF16)


def _merge_call(oa, ob, oc, gm, w_br, *, tm, gm_tile0, name):
    b, rows, _ = oa.shape
    d = w_br.shape[-1]
    gate_w = 3 * BRANCH_WIDTH
    assert (3 * d) % gate_w == 0
    o_spec = pl.BlockSpec((1, tm, BRANCH_WIDTH), lambda bi, i: (bi, i, 0))
    m_specs = [pl.BlockSpec((1, tm, d), lambda bi, i, r=r: (bi, gm_tile0 + i, r)) for r in range(3)]
    return pl.pallas_call(
        _merge_kernel,
        grid=(b, rows // tm),
        in_specs=[o_spec, o_spec, o_spec,
                  pl.BlockSpec((1, tm, gate_w), lambda bi, i: (bi, gm_tile0 + i, 3 * d // gate_w)),
                  *m_specs,
                  _resident(w_br.shape, lambda bi, i: (0, 0, 0))],
        out_specs=pl.BlockSpec((1, tm, d), lambda bi, i: (bi, i, 0)),
        out_shape=jax.ShapeDtypeStruct((b, rows, d), BF16),
        compiler_params=_params(("parallel", "parallel")),
        name=name,
    )(oa, ob, oc, gm, gm, gm, gm, w_br)


def _out_kernel(y_ref, w_ref, x_ref, mod_ref, o_ref):
    y = y_ref[0]
    for c in range(o_ref.shape[-1] // N_CHUNK):
        cols = slice(c * N_CHUNK, (c + 1) * N_CHUNK)
        out = jnp.dot(y, w_ref[:, cols], preferred_element_type=F32)
        o_ref[0, :, cols] = x_ref[0, :, cols] + mod_ref[0, 2:3, cols] * out


def _out_call(y, w_out, xs, mod, *, tm, ctx_stream, name):
    b, rows, d = xs.shape
    return pl.pallas_call(
        _out_kernel,
        grid=(b, rows // tm),
        in_specs=[pl.BlockSpec((1, tm, d), lambda bi, i: (bi, i, 0)),
                  _resident(w_out.shape, lambda bi, i: (0, 0)),
                  pl.BlockSpec((1, tm, d), lambda bi, i: (bi, i, 0)),
                  pl.BlockSpec((1, 3, d), lambda bi, i: (b if ctx_stream else bi, 0, 0))],
        out_specs=pl.BlockSpec((1, tm, d), lambda bi, i: (bi, i, 0)),
        out_shape=jax.ShapeDtypeStruct((b, rows, d), F32),
        compiler_params=_params(("parallel", "parallel")),
        name=name,
    )(y, w_out, xs, mod)


def _rope_tables(n_lat, n_ctx, rot_dim, src):
    axis_dim = rot_dim // 2
    t = jnp.arange(n_lat, dtype=jnp.int32)
    pos_row = (t // GRID_W).astype(F32)
    pos_col = (t % GRID_W).astype(F32)
    inv_freq = ROPE_THETA ** (-jnp.arange(0, axis_dim, 2, dtype=F32) / axis_dim)
    ang_r = pos_row[:, None] * inv_freq
    ang_c = pos_col[:, None] * inv_freq
    ang = jnp.concatenate([ang_r, ang_r, ang_c, ang_c], axis=-1)
    cos, sin = jnp.cos(ang), jnp.sin(ang)
    lane = jnp.arange(rot_dim)
    sign = jnp.where((lane % axis_dim) < axis_dim // 2, -1.0, 1.0).astype(F32)
    pad = jnp.asarray(src < 0)
    lanes = jnp.asarray(np.maximum(src, 0) % rot_dim)
    cos = jnp.where(pad, 1.0, jnp.take(cos, lanes, axis=1))
    sin = jnp.where(pad, 0.0, jnp.take(sin * sign, lanes, axis=1))
    cos = jnp.concatenate([cos, jnp.ones((n_ctx, LANES), F32)], axis=0)
    sin = jnp.concatenate([sin, jnp.zeros((n_ctx, LANES), F32)], axis=0)
    return cos, sin


PREP_MAX_ROWS = 256
_IDENT = np.arange(LANES)


def _prep_runs():
    o_gk = GQA_HEADS * HEAD_DIM
    o_gv = o_gk + GQA_KV_HEADS * HEAD_DIM
    o_mq = o_gv + GQA_KV_HEADS * HEAD_DIM
    o_ckv = o_mq + MLA_HEADS * (MLA_NOPE + MLA_ROPE)
    o_kr = o_ckv + MLA_KV_RANK
    o_dq = o_kr + MLA_ROPE
    o_dv = o_dq + 2 * DIFF_HEADS * 2 * DIFF_QK
    o_gate = o_dv + DIFF_HEADS * LANES
    o_merge = o_gate + 3 * BRANCH_WIDTH
    units = []
    for u in range(o_gv // LANES):
        units.append((0, u * LANES, u * LANES, _lane_src_a()))
    for u in range((o_mq - o_gv) // LANES):
        units.append((0, o_gv + u * LANES, o_gv + u * LANES, _IDENT))
    for hd in range(MLA_HEADS):
        base = o_mq + hd * (MLA_NOPE + MLA_ROPE)
        units.append((1, hd * MLA_QK_PAD, base, _IDENT))
        units.append((1, hd * MLA_QK_PAD + LANES, base + MLA_NOPE, _lane_src_b()))
    for u in range(MLA_KV_RANK // LANES):
        units.append((1, MLA_HEADS * MLA_QK_PAD + u * LANES, o_ckv + u * LANES, _IDENT))
    units.append((1, MLA_HEADS * MLA_QK_PAD + MLA_KV_RANK, o_kr, _lane_src_b()))
    for u in range((o_dv - o_dq) // LANES):
        units.append((2, u * LANES, o_dq + u * LANES, _lane_src_c()))
    for u in range((o_gate - o_dv) // LANES):
        units.append((2, o_dv - o_dq + u * LANES, o_dv + u * LANES, _IDENT))
    for u in range(3 * D_MODEL // LANES):
        units.append((3, u * LANES, o_merge + u * LANES, _IDENT))
    for u in range(3 * BRANCH_WIDTH // LANES):
        units.append((3, 3 * D_MODEL + u * LANES, o_gate + u * LANES, _IDENT))

    runs = []
    for out, dst0, base, lane_src in units:
        for t, idx in enumerate(lane_src):
            src = None if idx < 0 else base + int(idx)
            last = runs[-1] if runs else None
            if (last is not None and last[0] == out and last[1] + last[3] == dst0 + t
                    and last[3] < PREP_MAX_ROWS
                    and ((src is None and last[2] is None)
                         or (src is not None and last[2] is not None and last[2] + last[3] == src))):
                runs[-1] = (out, last[1], last[2], last[3] + 1)
            else:
                runs.append((out, dst0 + t, src, 1))
    assert all(dst % 16 == 0 and rows % 16 == 0 and (src is None or src % 16 == 0)
               for _, dst, src, rows in runs)
    return runs


def _prep_kernel(w_ref, *o_refs, runs):
    for out, dst, src, rows in runs:
        if src is None:
            o_refs[out][0, dst:dst + rows, :] = jnp.zeros((rows, o_refs[out].shape[-1]), BF16)
        else:
            o_refs[out][0, dst:dst + rows, :] = w_ref[0, src:src + rows, :].astype(BF16)


def _prep_call(w_in_t):
    depth, total, d = w_in_t.shape
    runs = _prep_runs()
    heights = [max(dst + rows for out, dst, _, rows in runs if out == o) for o in range(4)]
    td = 256
    return pl.pallas_call(
        functools.partial(_prep_kernel, runs=runs),
        grid=(depth, d // td),
        in_specs=[pl.BlockSpec((1, total, td), lambda l, i: (l, 0, i))],
        out_specs=[pl.BlockSpec((1, ht, td), lambda l, i: (l, 0, i)) for ht in heights],
        out_shape=[jax.ShapeDtypeStruct((depth, ht, d), BF16) for ht in heights],
        compiler_params=_params(("parallel", "parallel")),
        name="weight_layout",
    )(w_in_t)


def kernel(x, c, ctx, c_ctx, norm_w, w_ada, b_ada, w_in, b_merge, gqa_q_norm, gqa_k_norm, mla_q_nope_norm, mla_q_rope_norm, mla_kv_norm, mla_w_uk, mla_w_uv, mla_k_nope_norm, mla_k_rope_norm, diff_q_norm, diff_k_norm, diff_lambda_q1, diff_lambda_k1, diff_lambda_q2, diff_lambda_k2, diff_subln, w_br_gqa, w_br_mla, w_br_diff, w_out):
    b, n_lat, d = x.shape
    n_ctx = ctx.shape[1]
    n = n_lat + n_ctx
    depth = w_in.shape[0]
    assert n_lat % 2048 == 0 and n_ctx == 256 and n % 8 == 0 and b < 8 and d == D_MODEL

    c_all = jnp.concatenate([c, c_ctx[None], jnp.zeros((8 - b - 1, d), F32)], axis=0)
    mod = _ada_call(c_all, w_ada, b_ada).reshape(depth, 8, 3, d)

    w_a, w_b, w_c, w_g = _prep_call(jnp.swapaxes(w_in, 1, 2))
    w_ukv = jnp.concatenate([mla_w_uk, mla_w_uv], axis=-1).astype(BF16)
    w_br = jnp.stack([w_br_gqa, w_br_mla, w_br_diff], axis=1).astype(BF16)
    w_o = w_out.astype(BF16)
    bias_g = jnp.concatenate([b_merge, jnp.zeros((depth, 3 * BRANCH_WIDTH), F32)], axis=-1)
    src_a, src_b, src_c = _lane_src_a(), _lane_src_b(), _lane_src_c()
    g_aq = _permute_lanes(gqa_q_norm, src_a)
    g_ak = _permute_lanes(gqa_k_norm, src_a)
    g_bq = jnp.concatenate([mla_q_nope_norm, _permute_lanes(mla_q_rope_norm, src_b)], axis=-1)
    g_kr = _permute_lanes(mla_k_rope_norm, src_b)
    g_dq = _permute_lanes(jnp.tile(diff_q_norm, (1, 2)), src_c)
    g_dk = _permute_lanes(jnp.tile(diff_k_norm, (1, 2)), src_c)
    lamv = jnp.stack([diff_lambda_q1, diff_lambda_k1, diff_lambda_q2, diff_lambda_k2], axis=1)

    cos_a, sin_a = _rope_tables(n_lat, n_ctx, HEAD_DIM, src_a)
    cos_b, sin_b = _rope_tables(n_lat, n_ctx, MLA_ROPE, src_b)
    cos_c, sin_c = _rope_tables(n_lat, n_ctx, DIFF_QK, src_c)

    lat_chunks = tuple((s, 1024) for s in range(0, n_lat, 1024)) + ((n_lat, n_ctx),)
    assert sum(size for _, size in lat_chunks) == n
    ctx_chunks = ((0, n_ctx),)
    ctx_tile = n_lat // n_ctx

    for l in range(depth):
        last = l == depth - 1
        lam_init = 0.8 - 0.6 * math.exp(-0.3 * l)
        h = _h_call(x, ctx, norm_w[l][None], mod[l])

        qa, ka, va = _proj_a_call(h, w_a, l, g_aq[l][None], g_ak[l][None], cos_a, sin_a)
        qb, kb, vb = _proj_b_call(h, w_b, w_ukv, l, g_bq[l][None], mla_kv_norm[l][None], g_kr[l][None],
                                  mla_k_nope_norm[l][None], cos_b, sin_b)
        qc, kc, vc = _proj_c_call(h, w_c, l, g_dq[l][None], g_dk[l][None], cos_c, sin_c)
        gm = _gate_call(h, w_g, l, bias_g[l][None])

        qa = qa.reshape(b, GQA_KV_HEADS, GQA_GROUP, n, HEAD_DIM)
        qb = qb.reshape(b, MLA_HEADS, 1, n, MLA_QK_PAD)
        diff_kw = dict(diff=True, lamv=lamv[l][None], sub=diff_subln[l][None], lam_init=lam_init)

        ba = _score_bound([(gqa_q_norm[l], HEAD_DIM)], [(gqa_k_norm[l], HEAD_DIM)], HEAD_DIM ** -0.5)
        bb = _score_bound([(mla_q_nope_norm[l], MLA_NOPE), (mla_q_rope_norm[l], MLA_ROPE)],
                          [(mla_k_nope_norm[l], MLA_NOPE), (mla_k_rope_norm[l], MLA_ROPE)],
                          (MLA_NOPE + MLA_ROPE) ** -0.5)
        bc = _score_bound([(diff_q_norm[l], DIFF_QK)], [(diff_k_norm[l], DIFF_QK)], DIFF_QK ** -0.5)
        vta = _vt_call(va, "vt_gqa")
        vtb = _vt_call(vb, "vt_mla")
        vtc = _vt_call(vc, "vt_diff")
        if not last:
            ctx_kw = dict(tq=n_ctx, q_tile0=ctx_tile, n_q_tiles=1, kv_tile0=ctx_tile, chunks=ctx_chunks)
            oa = _flash_call(ba, qa, ka, va, vta, name="flash_gqa_ctx", **ctx_kw)
            ob = _flash_call(bb, qb, kb, vb, vtb, name="flash_mla_ctx", **ctx_kw)
            oc = _flash_call(bc, qc, kc, vc, vtc, name="flash_diff_ctx", **ctx_kw, **diff_kw)
            y = _merge_call(oa, ob, oc, gm, w_br[l], tm=n_ctx, gm_tile0=ctx_tile, name="branch_merge_ctx")
            ctx = _out_call(y, w_o[l], ctx, mod[l], tm=n_ctx, ctx_stream=True, name="out_proj_ctx")

        lat_kw = dict(q_tile0=0, kv_tile0=0, chunks=lat_chunks)
        oa = _flash_call(ba, qa, ka, va, vta, tq=512, n_q_tiles=n_lat // 512, name="flash_gqa", **lat_kw)
        ob = _flash_call(bb, qb, kb, vb, vtb, tq=2048, n_q_tiles=n_lat // 2048, name="flash_mla", **lat_kw)
        oc = _flash_call(bc, qc, kc, vc, vtc, tq=1024, n_q_tiles=n_lat // 1024, name="flash_diff",
                         **lat_kw, **diff_kw)
        y = _merge_call(oa, ob, oc, gm, w_br[l], tm=512, gm_tile0=0, name="branch_merge")
        x = _out_call(y, w_o[l], x, mod[l], tm=512, ctx_stream=False, name="out_proj")
    return x
```

```python
import functools
import math

import jax
import jax.numpy as jnp
import numpy as np
from jax import lax
from jax.experimental import pallas as pl
from jax.experimental.pallas import tpu as pltpu

F32 = jnp.float32
BF16 = jnp.bfloat16

GRID_W = 64
ROPE_THETA = 10000.0
EPS = 1e-6

D_MODEL = 2048
HEAD_DIM = 128
BRANCH_WIDTH = 1024
GQA_HEADS = 8
GQA_KV_HEADS = 2
GQA_GROUP = GQA_HEADS // GQA_KV_HEADS
MLA_HEADS = 8
MLA_NOPE = 128
MLA_ROPE = 64
MLA_QK_PAD = 256
MLA_KV_RANK = 512
DIFF_HEADS = 8
DIFF_QK = 64

LANES = 128
MXU_N = 256
VMEM_LIMIT = 56 * 1024 * 1024
LOG2E = 1.4426950408889634
SAFE_BOUND_LOG2 = 57.0


def _params(sem, vmem=VMEM_LIMIT):
    return pltpu.CompilerParams(dimension_semantics=sem, vmem_limit_bytes=vmem)


def _resident(shape, index_map):
    return pl.BlockSpec(shape, index_map, pipeline_mode=pl.Buffered(1))


def _rms(y, gain, n):
    ms = jnp.sum(y * y, axis=-1, keepdims=True) * (1.0 / n)
    return y * lax.rsqrt(ms + EPS) * gain


def _seg_rms(y, gain, lo):
    ss = y * y
    s_lo = jnp.sum(jnp.where(lo, ss, 0.0), axis=-1, keepdims=True)
    s_hi = jnp.sum(jnp.where(lo, 0.0, ss), axis=-1, keepdims=True)
    ms = jnp.where(lo, s_lo, s_hi) * (1.0 / DIFF_QK)
    return y * lax.rsqrt(ms + EPS) * gain


def _rope(y, cos, sin_signed):
    return y * cos + pltpu.roll(y, LANES // 2, 1) * sin_signed


def _lane_src_a():
    e = np.arange(HEAD_DIM // 4)
    return np.concatenate([e, 64 + e, 32 + e, 96 + e])


def _lane_src_b():
    e = np.arange(MLA_ROPE // 4)
    pad = -np.ones(32, np.int64)
    return np.concatenate([e, 32 + e, pad, 16 + e, 48 + e, pad])


def _lane_src_c():
    e = np.arange(DIFF_QK // 4)
    x1 = [m * DIFF_QK + a * 32 + e for m in (0, 1) for a in (0, 1)]
    x2 = [m * DIFF_QK + a * 32 + 16 + e for m in (0, 1) for a in (0, 1)]
    return np.concatenate(x1 + x2)


def _permute_lanes(vec, src):
    out = jnp.take(vec, jnp.asarray(np.maximum(src, 0)), axis=-1)
    return jnp.where(jnp.asarray(src >= 0), out, 0.0)


def _sigmoid(z):
    return 1.0 / (1.0 + jnp.exp(-z))


def _dot_wt(x, wt):
    return lax.dot_general(x, wt, (((1,), (1,)), ((), ())), preferred_element_type=F32)


def _ada_kernel(c_ref, w_ref, b_ref, o_ref):
    c = c_ref[...]
    a = (c * _sigmoid(c)).astype(BF16)
    o_ref[0] = jnp.dot(a, w_ref[0].astype(BF16), preferred_element_type=F32) + b_ref[0]


def _ada_call(c_all, w_ada, b_ada):
    depth, d, n3 = w_ada.shape
    tn = 512
    return pl.pallas_call(
        _ada_kernel,
        grid=(depth, n3 // tn),
        in_specs=[pl.BlockSpec((8, d), lambda l, j: (0, 0)),
                  pl.BlockSpec((1, d, tn), lambda l, j: (l, 0, j)),
                  pl.BlockSpec((1, 1, tn), lambda l, j: (l, 0, j))],
        out_specs=pl.BlockSpec((1, 8, tn), lambda l, j: (l, 0, j)),
        out_shape=jax.ShapeDtypeStruct((depth, 8, n3), F32),
        compiler_params=_params(("parallel", "parallel")),
        name="ada_mod",
    )(c_all, w_ada, b_ada.reshape(depth, 1, n3))


def _h_kernel(x_ref, ctx_ref, nw_ref, mod_ref, modc_ref, h_ref, *, lat_tiles):
    def norm_mod(src_ref, m_ref):
        x = src_ref[0]
        ms = jnp.mean(x * x, axis=-1, keepdims=True)
        y = x * lax.rsqrt(ms + EPS) * nw_ref[...]
        h_ref[0] = (y * (1.0 + m_ref[0, 1:2, :]) + m_ref[0, 0:1, :]).astype(BF16)

    t = pl.program_id(1)

    @pl.when(t < lat_tiles)
    def _():
        norm_mod(x_ref, mod_ref)

    @pl.when(t >= lat_tiles)
    def _():
        norm_mod(ctx_ref, modc_ref)


def _h_call(x, ctx, norm_w, mod):
    b, n_lat, d = x.shape
    tr = ctx.shape[1]
    lat_tiles = n_lat // tr
    return pl.pallas_call(
        functools.partial(_h_kernel, lat_tiles=lat_tiles),
        grid=(b, lat_tiles + 1),
        in_specs=[pl.BlockSpec((1, tr, d), lambda bi, t: (bi, jnp.minimum(t, lat_tiles - 1), 0)),
                  pl.BlockSpec((1, tr, d), lambda bi, t: (bi, 0, 0)),
                  pl.BlockSpec((1, d), lambda bi, t: (0, 0)),
                  pl.BlockSpec((1, 3, d), lambda bi, t: (bi, 0, 0)),
                  pl.BlockSpec((1, 3, d), lambda bi, t: (b, 0, 0))],
        out_specs=pl.BlockSpec((1, tr, d), lambda bi, t: (bi, t, 0)),
        out_shape=jax.ShapeDtypeStruct((b, n_lat + tr, d), BF16),
        compiler_params=_params(("parallel", "parallel")),
        name="norm_mod",
    )(x, ctx, norm_w, mod, mod)


def _proj_a_kernel(h_ref, w_ref, gq_ref, gk_ref, cos_ref, sin_ref, q_ref, k_ref, v_ref, *, values):
    h = h_ref[0]
    cos = cos_ref[...]
    sin = sin_ref[...]
    scale = HEAD_DIM ** -0.5 * LOG2E
    n_q = GQA_HEADS * HEAD_DIM // MXU_N
    for c in (range(n_q + 1, n_q + 2) if values else range(n_q + 1)):
        y = _dot_wt(h, w_ref[c * MXU_N:(c + 1) * MXU_N, :])
        for u in range(2):
            yu = y[:, u * LANES:(u + 1) * LANES]
            if c < n_q:
                z = _rope(_rms(yu, gq_ref[...], HEAD_DIM), cos, sin)
                q_ref[0, 2 * c + u] = (z * scale).astype(BF16)
            elif c == n_q:
                z = _rope(_rms(yu, gk_ref[...], HEAD_DIM), cos, sin)
                k_ref[0, u] = z.astype(BF16)
            else:
                v_ref[0, u] = yu.astype(BF16)


def _proj_b_kernel(h_ref, w_ref, wup_ref, gq_ref, gckv_ref, gkr_ref, gkn_ref, cos_ref, sin_ref,
                   q_ref, k_ref, v_ref):
    h = h_ref[0]
    cos = cos_ref[...]
    sin = sin_ref[...]
    scale = (MLA_NOPE + MLA_ROPE) ** -0.5 * LOG2E
    base = MLA_HEADS * MLA_QK_PAD
    y0 = _dot_wt(h, w_ref[base:base + MXU_N, :])
    y1 = _dot_wt(h, w_ref[base + MXU_N:base + 2 * MXU_N, :])
    ms = (jnp.sum(y0 * y0, axis=-1, keepdims=True)
          + jnp.sum(y1 * y1, axis=-1, keepdims=True)) * (1.0 / MLA_KV_RANK)
    inv = lax.rsqrt(ms + EPS)
    ckv = jnp.concatenate([(y0 * inv * gckv_ref[:, 0:MXU_N]).astype(BF16),
                           (y1 * inv * gckv_ref[:, MXU_N:2 * MXU_N]).astype(BF16)], axis=1)
    base += MLA_KV_RANK
    yk = _dot_wt(h, w_ref[base:base + LANES, :])
    kr = _rope(_rms(yk, gkr_ref[...], MLA_ROPE), cos, sin).astype(BF16)
    for hd in range(MLA_HEADS):
        y = _dot_wt(h, w_ref[hd * MLA_QK_PAD:(hd + 1) * MLA_QK_PAD, :])
        nope = _rms(y[:, :LANES], gq_ref[:, :LANES], MLA_NOPE)
        rope = _rope(_rms(y[:, LANES:], gq_ref[:, LANES:], MLA_ROPE), cos, sin)
        q_ref[0, hd, :, 0:LANES] = (nope * scale).astype(BF16)
        q_ref[0, hd, :, LANES:2 * LANES] = (rope * scale).astype(BF16)
    per = MLA_HEADS // 2
    for c in range(2 * per):
        y = jnp.dot(ckv, wup_ref[:, c * MXU_N:(c + 1) * MXU_N], preferred_element_type=F32)
        for u in range(2):
            yu = y[:, u * LANES:(u + 1) * LANES]
            if c < per:
                k_ref[0, 2 * c + u, :, 0:LANES] = _rms(yu, gkn_ref[...], MLA_NOPE).astype(BF16)
                k_ref[0, 2 * c + u, :, LANES:2 * LANES] = kr
            else:
                v_ref[0, 2 * (c - per) + u] = yu.astype(BF16)


def _proj_c_kernel(h_ref, w_ref, gq_ref, gk_ref, cos_ref, sin_ref, q_ref, k_ref, v_ref, *, values):
    h = h_ref[0]
    cos = cos_ref[...]
    sin = sin_ref[...]
    scale = DIFF_QK ** -0.5 * LOG2E
    lo = (lax.broadcasted_iota(jnp.int32, (1, LANES), 1) & 32) == 0
    per = DIFF_HEADS // 2
    for c in (range(2 * per, 3 * per) if values else range(2 * per)):
        y = _dot_wt(h, w_ref[c * MXU_N:(c + 1) * MXU_N, :])
        for u in range(2):
            yu = y[:, u * LANES:(u + 1) * LANES]
            if c < per:
                z = _rope(_seg_rms(yu, gq_ref[...], lo), cos, sin) * scale
                q_ref[0, 2 * c + u, 0] = jnp.where(lo, z, 0.0).astype(BF16)
                q_ref[0, 2 * c + u, 1] = jnp.where(lo, 0.0, z).astype(BF16)
            elif c < 2 * per:
                z = _rope(_seg_rms(yu, gk_ref[...], lo), cos, sin)
                k_ref[0, 2 * (c - per) + u] = z.astype(BF16)
            else:
                v_ref[0, 2 * (c - 2 * per) + u] = yu.astype(BF16)


def _proj_tm(n):
    return n // 8


def _layer_weight(w, layer):
    return _resident((None,) + w.shape[1:], lambda bi, i: (layer, 0, 0))


def _proj_ac_kernel(h_ref, wa_ref, wc_ref, gaq_ref, gak_ref, gcq_ref, gck_ref,
                    cosa_ref, sina_ref, cosc_ref, sinc_ref,
                    qa_ref, ka_ref, va_ref, qc_ref, kc_ref, vc_ref):
    c_refs = (h_ref, wc_ref, gcq_ref, gck_ref, cosc_ref, sinc_ref, qc_ref, kc_ref, vc_ref)
    a_refs = (h_ref, wa_ref, gaq_ref, gak_ref, cosa_ref, sina_ref, qa_ref, ka_ref, va_ref)
    _proj_c_kernel(*c_refs, values=False)
    _proj_a_kernel(*a_refs, values=False)
    _proj_c_kernel(*c_refs, values=True)
    _proj_a_kernel(*a_refs, values=True)


def _proj_ac_call(h, w_a, w_c, layer, gaq, gak, gcq, gck, cos_a, sin_a, cos_c, sin_c):
    b, n, d = h.shape
    tm = _proj_tm(n)
    row = lambda bi, i: (bi, i, 0)
    head = lambda bi, i: (bi, 0, i, 0)
    const = lambda bi, i: (0, 0)
    gain = pl.BlockSpec((1, LANES), const)
    table = pl.BlockSpec((tm, LANES), lambda bi, i: (i, 0))
    return pl.pallas_call(
        _proj_ac_kernel,
        grid=(b, n // tm),
        in_specs=[pl.BlockSpec((1, tm, d), row), _layer_weight(w_a, layer), _layer_weight(w_c, layer),
                  gain, gain, gain, gain, table, table, table, table],
        out_specs=[pl.BlockSpec((1, GQA_HEADS, tm, HEAD_DIM), head),
                   pl.BlockSpec((1, GQA_KV_HEADS, tm, HEAD_DIM), head),
                   pl.BlockSpec((1, GQA_KV_HEADS, tm, HEAD_DIM), head),
                   pl.BlockSpec((1, DIFF_HEADS, 2, tm, LANES), lambda bi, i: (bi, 0, 0, i, 0)),
                   pl.BlockSpec((1, DIFF_HEADS, tm, LANES), head),
                   pl.BlockSpec((1, DIFF_HEADS, tm, LANES), head)],
        out_shape=[jax.ShapeDtypeStruct((b, GQA_HEADS, n, HEAD_DIM), BF16),
                   jax.ShapeDtypeStruct((b, GQA_KV_HEADS, n, HEAD_DIM), BF16),
                   jax.ShapeDtypeStruct((b, GQA_KV_HEADS, n, HEAD_DIM), BF16),
                   jax.ShapeDtypeStruct((b, DIFF_HEADS, 2, n, LANES), BF16),
                   jax.ShapeDtypeStruct((b, DIFF_HEADS, n, LANES), BF16),
                   jax.ShapeDtypeStruct((b, DIFF_HEADS, n, LANES), BF16)],
        compiler_params=_params(("parallel", "parallel")),
        name="proj_gqa_diff",
    )(h, w_a, w_c, gaq, gak, gcq, gck, cos_a, sin_a, cos_c, sin_c)


def _proj_b_call(h, w, w_up, layer, gq, gckv, gkr, gkn, cos, sin):
    b, n, d = h.shape
    tm = _proj_tm(n)
    row = lambda bi, i: (bi, i, 0)
    head = lambda bi, i: (bi, 0, i, 0)
    const = lambda bi, i: (0, 0)
    return pl.pallas_call(
        _proj_b_kernel,
        grid=(b, n // tm),
        in_specs=[pl.BlockSpec((1, tm, d), row),
                  _layer_weight(w, layer),
                  _layer_weight(w_up, layer),
                  pl.BlockSpec((1, MLA_QK_PAD), const),
                  pl.BlockSpec((1, MLA_KV_RANK), const),
                  pl.BlockSpec((1, LANES), const),
                  pl.BlockSpec((1, LANES), const),
                  pl.BlockSpec((tm, LANES), lambda bi, i: (i, 0)),
                  pl.BlockSpec((tm, LANES), lambda bi, i: (i, 0))],
        out_specs=[pl.BlockSpec((1, MLA_HEADS, tm, MLA_QK_PAD), head),
                   pl.BlockSpec((1, MLA_HEADS, tm, MLA_QK_PAD), head),
                   pl.BlockSpec((1, MLA_HEADS, tm, LANES), head)],
        out_shape=[jax.ShapeDtypeStruct((b, MLA_HEADS, n, MLA_QK_PAD), BF16),
                   jax.ShapeDtypeStruct((b, MLA_HEADS, n, MLA_QK_PAD), BF16),
                   jax.ShapeDtypeStruct((b, MLA_HEADS, n, LANES), BF16)],
        compiler_params=_params(("parallel", "parallel")),
        name="proj_mla",
    )(h, w, w_up, gq, gckv, gkr, gkn, cos, sin)


def _gate_kernel(h_ref, w_ref, b_ref, o_ref, *, n_sigmoid_tiles):
    is_sigmoid = pl.program_id(2) < n_sigmoid_tiles
    h = h_ref[0]
    for c in range(o_ref.shape[-1] // MXU_N):
        cols = slice(c * MXU_N, (c + 1) * MXU_N)
        y = _dot_wt(h, w_ref[cols, :])
        t = _sigmoid(y + b_ref[:, cols])
        o_ref[0, :, cols] = jnp.where(is_sigmoid, t, y * t).astype(BF16)


def _gate_call(h, w, layer, bias):
    b, n, d = h.shape
    cols = w.shape[1]
    tm = n // 4
    tn = 3072
    assert (cols - 3 * BRANCH_WIDTH) % tn == 0 and cols % tn == 0
    return pl.pallas_call(
        functools.partial(_gate_kernel, n_sigmoid_tiles=(cols - 3 * BRANCH_WIDTH) // tn),
        grid=(b, n // tm, cols // tn),
        in_specs=[pl.BlockSpec((1, tm, d), lambda bi, i, j: (bi, i, 0)),
                  pl.BlockSpec((None, tn, d), lambda bi, i, j: (layer, j, 0)),
                  pl.BlockSpec((1, tn), lambda bi, i, j: (0, j))],
        out_specs=pl.BlockSpec((1, tm, tn), lambda bi, i, j: (bi, i, j)),
        out_shape=jax.ShapeDtypeStruct((b, n, cols), BF16),
        compiler_params=_params(("parallel", "parallel", "parallel")),
        name="gate_proj",
    )(h, w, bias)


def _flash_kernel(*refs, group, tq, chunks, diff, lam_init):
    if diff:
        bound_ref, q_ref, k_ref, v_ref, vt_ref, lamv_ref, sub_ref, o_ref = refs
    else:
        bound_ref, q_ref, k_ref, v_ref, vt_ref, o_ref = refs
    dq = q_ref.shape[-1]
    nt = (((1,), (1,)), ((), ()))

    q = q_ref[0, 0].reshape(group * tq, dq)
    half = group * tq // 2
    halves = (slice(0, half), slice(half, 2 * half))
    bound = bound_ref[0]
    safe = bound <= SAFE_BOUND_LOG2

    def finish(out):
        if diff:
            lv = lamv_ref[0]
            lam = (jnp.exp(jnp.sum(lv[0:1] * lv[1:2], axis=-1, keepdims=True))
                   - jnp.exp(jnp.sum(lv[2:3] * lv[3:4], axis=-1, keepdims=True)) + lam_init)
            o = out[:tq] - lam * out[tq:]
            o_ref[0] = (_rms(o, sub_ref[...], LANES) * (1.0 - lam_init)).astype(BF16)
        else:
            for g in range(group):
                o_ref[0, :, g * LANES:(g + 1) * LANES] = out[g * tq:(g + 1) * tq].astype(BF16)

    @pl.when(safe)
    def _():
        accs = [None, None]

        def k_dot_q(start, size):
            k = k_ref[0, 0, start:start + size, :]
            return [lax.dot_general(k, q[rows], nt, preferred_element_type=F32) for rows in halves]

        def vt_dot_pt(pts, start, size):
            vt = vt_ref[0, 0, :, start:start + size]
            for i, pt in enumerate(pts):
                t = jnp.dot(vt, pt, preferred_element_type=F32)
                accs[i] = t if accs[i] is None else accs[i] + t

        pending = None
        for start, size in chunks:
            ss = k_dot_q(start, size)
            if pending is not None:
                vt_dot_pt(*pending)
            pending = ([jnp.exp2(s - bound).astype(BF16) for s in ss], start, size)
        vt_dot_pt(*pending)
        acc = jnp.concatenate(accs, axis=1)
        finish((acc[:LANES] / acc[LANES:LANES + 1]).T)

    @pl.when(jnp.logical_not(safe))
    def _():
        m = l = acc = None
        for start, size in chunks:
            k = k_ref[0, 0, start:start + size, :]
            v = v_ref[0, 0, start:start + size, :]
            s = lax.dot_general(q, k, nt, preferred_element_type=F32)
            m_cur = jnp.max(s, axis=-1, keepdims=True)
            if m is None:
                m = m_cur
                p = jnp.exp2(s - m)
                l = jnp.sum(p, axis=-1, keepdims=True)
                acc = jnp.dot(p.astype(BF16), v, preferred_element_type=F32)
            else:
                m_new = jnp.maximum(m, m_cur)
                alpha = jnp.exp2(m - m_new)
                p = jnp.exp2(s - m_new)
                l = alpha * l + jnp.sum(p, axis=-1, keepdims=True)
                acc = alpha * acc + jnp.dot(p.astype(BF16), v, preferred_element_type=F32)
                m = m_new
        finish(acc / l)


VT_ROWS = LANES + 16


def _vt_kernel(v_ref, o_ref):
    n = v_ref.shape[2]
    for hd in range(v_ref.shape[1]):
        for j in range(n // LANES):
            cols = slice(j * LANES, (j + 1) * LANES)
            o_ref[0, hd, 0:LANES, cols] = v_ref[0, hd, cols, :].T
        o_ref[0, hd, LANES:VT_ROWS, :] = jnp.ones((VT_ROWS - LANES, n), BF16)


def _vt_call(v, name):
    b, heads, n, dv = v.shape
    hb = min(heads, 4)
    return pl.pallas_call(
        _vt_kernel,
        grid=(b, heads // hb),
        in_specs=[pl.BlockSpec((1, hb, n, dv), lambda bi, h: (bi, h, 0, 0))],
        out_specs=pl.BlockSpec((1, hb, VT_ROWS, n), lambda bi, h: (bi, h, 0, 0)),
        out_shape=jax.ShapeDtypeStruct((b, heads, VT_ROWS, n), BF16),
        compiler_params=_params(("parallel", "parallel")),
        name=name,
    )(v)


def _flash_call(bound, q, k, v, vt, *, tq, q_tile0, n_q_tiles, kv_tile0, chunks,
                diff=False, lamv=None, sub=None, lam_init=0.0, name="flash"):
    b, heads, group, n, dq = q.shape
    kv_len = sum(size for _, size in chunks)
    out_w = LANES if diff else group * LANES
    in_specs = [pl.BlockSpec(memory_space=pltpu.SMEM),
                pl.BlockSpec((1, 1, group, tq, dq), lambda bi, h, i: (bi, h, 0, q_tile0 + i, 0)),
                pl.BlockSpec((1, 1, kv_len, dq), lambda bi, h, i: (bi, h, kv_tile0, 0)),
                pl.BlockSpec((1, 1, kv_len, LANES), lambda bi, h, i: (bi, h, kv_tile0, 0)),
                pl.BlockSpec((1, 1, VT_ROWS, kv_len), lambda bi, h, i: (bi, h, 0, kv_tile0))]
    args = [bound, q, k, v, vt]
    if diff:
        in_specs += [pl.BlockSpec((1, 4, DIFF_QK), lambda bi, h, i: (0, 0, 0)),
                     pl.BlockSpec((1, LANES), lambda bi, h, i: (0, 0))]
        args += [lamv, sub]
    kern = functools.partial(_flash_kernel, group=group, tq=tq, chunks=chunks, diff=diff,
                             lam_init=lam_init)
    return pl.pallas_call(
        kern,
        grid=(b, heads, n_q_tiles),
        in_specs=in_specs,
        out_specs=pl.BlockSpec((1, tq, out_w), lambda bi, h, i: (bi, i, h)),
        out_shape=jax.ShapeDtypeStruct((b, n_q_tiles * tq, BRANCH_WIDTH), BF16),
        compiler_params=_params(("parallel", "parallel", "parallel")),
        name=name,
    )(*args)


def _score_bound(q_terms, k_terms, scale):
    q2 = sum(d * jnp.max(jnp.square(g.astype(F32))) for g, d in q_terms)
    k2 = sum(d * jnp.max(jnp.square(g.astype(F32))) for g, d in k_terms)
    return (1.01 * scale * LOG2E * jnp.sqrt(q2 * k2)).reshape(1)


N_CHUNK = 512


def _merge_kernel(oa_ref, ob_ref, oc_ref, g_ref, ma_ref, mb_ref, mc_ref, w_ref, y_ref):
    ogs = [o_ref[0] * g_ref[0, :, r * BRANCH_WIDTH:(r + 1) * BRANCH_WIDTH]
           for r, o_ref in enumerate((oa_ref, ob_ref, oc_ref))]
    for c in range(y_ref.shape[-1] // N_CHUNK):
        cols = slice(c * N_CHUNK, (c + 1) * N_CHUNK)
        acc = None
        for r, m_ref in enumerate((ma_ref, mb_ref, mc_ref)):
            z = jnp.dot(ogs[r], w_ref[r, :, cols], preferred_element_type=F32)
            t = m_ref[0, :, cols].astype(F32) * z
            acc = t if acc is None else acc + t
        y_ref[0, :, cols] = acc.astype(BF16)


def _merge_call(oa, ob, oc, gm, w_br, *, tm, gm_tile0, name):
    b, rows, _ = oa.shape
    d = w_br.shape[-1]
    gate_w = 3 * BRANCH_WIDTH
    assert (3 * d) % gate_w == 0
    o_spec = pl.BlockSpec((1, tm, BRANCH_WIDTH), lambda bi, i: (bi, i, 0))
    m_specs = [pl.BlockSpec((1, tm, d), lambda bi, i, r=r: (bi, gm_tile0 + i, r)) for r in range(3)]
    return pl.pallas_call(
        _merge_kernel,
        grid=(b, rows // tm),
        in_specs=[o_spec, o_spec, o_spec,
                  pl.BlockSpec((1, tm, gate_w), lambda bi, i: (bi, gm_tile0 + i, 3 * d // gate_w)),
                  *m_specs,
                  _resident(w_br.shape, lambda bi, i: (0, 0, 0))],
        out_specs=pl.BlockSpec((1, tm, d), lambda bi, i: (bi, i, 0)),
        out_shape=jax.ShapeDtypeStruct((b, rows, d), BF16),
        compiler_params=_params(("parallel", "parallel")),
        name=name,
    )(oa, ob, oc, gm, gm, gm, gm, w_br)


def _out_kernel(y_ref, w_ref, x_ref, mod_ref, o_ref):
    y = y_ref[0]
    for c in range(o_ref.shape[-1] // N_CHUNK):
        cols = slice(c * N_CHUNK, (c + 1) * N_CHUNK)
        out = jnp.dot(y, w_ref[:, cols], preferred_element_type=F32)
        o_ref[0, :, cols] = x_ref[0, :, cols] + mod_ref[0, 2:3, cols] * out


def _out_call(y, w_out, xs, mod, *, tm, ctx_stream, name):
    b, rows, d = xs.shape
    return pl.pallas_call(
        _out_kernel,
        grid=(b, rows // tm),
        in_specs=[pl.BlockSpec((1, tm, d), lambda bi, i: (bi, i, 0)),
                  _resident(w_out.shape, lambda bi, i: (0, 0)),
                  pl.BlockSpec((1, tm, d), lambda bi, i: (bi, i, 0)),
                  pl.BlockSpec((1, 3, d), lambda bi, i: (b if ctx_stream else bi, 0, 0))],
        out_specs=pl.BlockSpec((1, tm, d), lambda bi, i: (bi, i, 0)),
        out_shape=jax.ShapeDtypeStruct((b, rows, d), F32),
        compiler_params=_params(("parallel", "parallel")),
        name=name,
    )(y, w_out, xs, mod)


def _rope_tables(n_lat, n_ctx, rot_dim, src):
    axis_dim = rot_dim // 2
    t = jnp.arange(n_lat, dtype=jnp.int32)
    pos_row = (t // GRID_W).astype(F32)
    pos_col = (t % GRID_W).astype(F32)
    inv_freq = ROPE_THETA ** (-jnp.arange(0, axis_dim, 2, dtype=F32) / axis_dim)
    ang_r = pos_row[:, None] * inv_freq
    ang_c = pos_col[:, None] * inv_freq
    ang = jnp.concatenate([ang_r, ang_r, ang_c, ang_c], axis=-1)
    cos, sin = jnp.cos(ang), jnp.sin(ang)
    lane = jnp.arange(rot_dim)
    sign = jnp.where((lane % axis_dim) < axis_dim // 2, -1.0, 1.0).astype(F32)
    pad = jnp.asarray(src < 0)
    lanes = jnp.asarray(np.maximum(src, 0) % rot_dim)
    cos = jnp.where(pad, 1.0, jnp.take(cos, lanes, axis=1))
    sin = jnp.where(pad, 0.0, jnp.take(sin * sign, lanes, axis=1))
    cos = jnp.concatenate([cos, jnp.ones((n_ctx, LANES), F32)], axis=0)
    sin = jnp.concatenate([sin, jnp.zeros((n_ctx, LANES), F32)], axis=0)
    return cos, sin


PREP_MAX_ROWS = 256
_IDENT = np.arange(LANES)


def _prep_runs():
    o_gk = GQA_HEADS * HEAD_DIM
    o_gv = o_gk + GQA_KV_HEADS * HEAD_DIM
    o_mq = o_gv + GQA_KV_HEADS * HEAD_DIM
    o_ckv = o_mq + MLA_HEADS * (MLA_NOPE + MLA_ROPE)
    o_kr = o_ckv + MLA_KV_RANK
    o_dq = o_kr + MLA_ROPE
    o_dv = o_dq + 2 * DIFF_HEADS * 2 * DIFF_QK
    o_gate = o_dv + DIFF_HEADS * LANES
    o_merge = o_gate + 3 * BRANCH_WIDTH
    units = []
    for u in range(o_gv // LANES):
        units.append((0, u * LANES, u * LANES, _lane_src_a()))
    for u in range((o_mq - o_gv) // LANES):
        units.append((0, o_gv + u * LANES, o_gv + u * LANES, _IDENT))
    for hd in range(MLA_HEADS):
        base = o_mq + hd * (MLA_NOPE + MLA_ROPE)
        units.append((1, hd * MLA_QK_PAD, base, _IDENT))
        units.append((1, hd * MLA_QK_PAD + LANES, base + MLA_NOPE, _lane_src_b()))
    for u in range(MLA_KV_RANK // LANES):
        units.append((1, MLA_HEADS * MLA_QK_PAD + u * LANES, o_ckv + u * LANES, _IDENT))
    units.append((1, MLA_HEADS * MLA_QK_PAD + MLA_KV_RANK, o_kr, _lane_src_b()))
    for u in range((o_dv - o_dq) // LANES):
        units.append((2, u * LANES, o_dq + u * LANES, _lane_src_c()))
    for u in range((o_gate - o_dv) // LANES):
        units.append((2, o_dv - o_dq + u * LANES, o_dv + u * LANES, _IDENT))
    for u in range(3 * D_MODEL // LANES):
        units.append((3, u * LANES, o_merge + u * LANES, _IDENT))
    for u in range(3 * BRANCH_WIDTH // LANES):
        units.append((3, 3 * D_MODEL + u * LANES, o_gate + u * LANES, _IDENT))

    runs = []
    for out, dst0, base, lane_src in units:
        for t, idx in enumerate(lane_src):
            src = None if idx < 0 else base + int(idx)
            last = runs[-1] if runs else None
            if (last is not None and last[0] == out and last[1] + last[3] == dst0 + t
                    and last[3] < PREP_MAX_ROWS
                    and ((src is None and last[2] is None)
                         or (src is not None and last[2] is not None and last[2] + last[3] == src))):
                runs[-1] = (out, last[1], last[2], last[3] + 1)
            else:
                runs.append((out, dst0 + t, src, 1))
    assert all(dst % 16 == 0 and rows % 16 == 0 and (src is None or src % 16 == 0)
               for _, dst, src, rows in runs)
    return runs


def _prep_kernel(w_ref, *o_refs, runs):
    for out, dst, src, rows in runs:
        if src is None:
            o_refs[out][0, dst:dst + rows, :] = jnp.zeros((rows, o_refs[out].shape[-1]), BF16)
        else:
            o_refs[out][0, dst:dst + rows, :] = w_ref[0, src:src + rows, :].astype(BF16)


def _prep_call(w_in_t):
    depth, total, d = w_in_t.shape
    runs = _prep_runs()
    heights = [max(dst + rows for out, dst, _, rows in runs if out == o) for o in range(4)]
    td = 256
    return pl.pallas_call(
        functools.partial(_prep_kernel, runs=runs),
        grid=(depth, d // td),
        in_specs=[pl.BlockSpec((1, total, td), lambda l, i: (l, 0, i))],
        out_specs=[pl.BlockSpec((1, ht, td), lambda l, i: (l, 0, i)) for ht in heights],
        out_shape=[jax.ShapeDtypeStruct((depth, ht, d), BF16) for ht in heights],
        compiler_params=_params(("parallel", "parallel")),
        name="weight_layout",
    )(w_in_t)


def kernel(x, c, ctx, c_ctx, norm_w, w_ada, b_ada, w_in, b_merge, gqa_q_norm, gqa_k_norm, mla_q_nope_norm, mla_q_rope_norm, mla_kv_norm, mla_w_uk, mla_w_uv, mla_k_nope_norm, mla_k_rope_norm, diff_q_norm, diff_k_norm, diff_lambda_q1, diff_lambda_k1, diff_lambda_q2, diff_lambda_k2, diff_subln, w_br_gqa, w_br_mla, w_br_diff, w_out):
    b, n_lat, d = x.shape
    n_ctx = ctx.shape[1]
    n = n_lat + n_ctx
    depth = w_in.shape[0]
    assert n_lat % 2048 == 0 and n_ctx == 256 and n % 8 == 0 and b < 8 and d == D_MODEL

    c_all = jnp.concatenate([c, c_ctx[None], jnp.zeros((8 - b - 1, d), F32)], axis=0)
    mod = _ada_call(c_all, w_ada, b_ada).reshape(depth, 8, 3, d)

    w_a, w_b, w_c, w_g = _prep_call(jnp.swapaxes(w_in, 1, 2))
    w_ukv = jnp.concatenate([mla_w_uk, mla_w_uv], axis=-1).astype(BF16)
    w_br = jnp.stack([w_br_gqa, w_br_mla, w_br_diff], axis=1).astype(BF16)
    w_o = w_out.astype(BF16)
    bias_g = jnp.concatenate([b_merge, jnp.zeros((depth, 3 * BRANCH_WIDTH), F32)], axis=-1)
    src_a, src_b, src_c = _lane_src_a(), _lane_src_b(), _lane_src_c()
    g_aq = _permute_lanes(gqa_q_norm, src_a)
    g_ak = _permute_lanes(gqa_k_norm, src_a)
    g_bq = jnp.concatenate([mla_q_nope_norm, _permute_lanes(mla_q_rope_norm, src_b)], axis=-1)
    g_kr = _permute_lanes(mla_k_rope_norm, src_b)
    g_dq = _permute_lanes(jnp.tile(diff_q_norm, (1, 2)), src_c)
    g_dk = _permute_lanes(jnp.tile(diff_k_norm, (1, 2)), src_c)
    lamv = jnp.stack([diff_lambda_q1, diff_lambda_k1, diff_lambda_q2, diff_lambda_k2], axis=1)

    cos_a, sin_a = _rope_tables(n_lat, n_ctx, HEAD_DIM, src_a)
    cos_b, sin_b = _rope_tables(n_lat, n_ctx, MLA_ROPE, src_b)
    cos_c, sin_c = _rope_tables(n_lat, n_ctx, DIFF_QK, src_c)

    lat_chunks = tuple((s, 1024) for s in range(0, n_lat, 1024)) + ((n_lat, n_ctx),)
    assert sum(size for _, size in lat_chunks) == n
    ctx_chunks = ((0, n_ctx),)
    ctx_tile = n_lat // n_ctx

    for l in range(depth):
        last = l == depth - 1
        lam_init = 0.8 - 0.6 * math.exp(-0.3 * l)
        h = _h_call(x, ctx, norm_w[l][None], mod[l])

        qa, ka, va, qc, kc, vc = _proj_ac_call(
            h, w_a, w_c, l, g_aq[l][None], g_ak[l][None], g_dq[l][None], g_dk[l][None],
            cos_a, sin_a, cos_c, sin_c)
        qb, kb, vb = _proj_b_call(h, w_b, w_ukv, l, g_bq[l][None], mla_kv_norm[l][None], g_kr[l][None],
                                  mla_k_nope_norm[l][None], cos_b, sin_b)
        gm = _gate_call(h, w_g, l, bias_g[l][None])

        qa = qa.reshape(b, GQA_KV_HEADS, GQA_GROUP, n, HEAD_DIM)
        qb = qb.reshape(b, MLA_HEADS, 1, n, MLA_QK_PAD)
        diff_kw = dict(diff=True, lamv=lamv[l][None], sub=diff_subln[l][None], lam_init=lam_init)

        ba = _score_bound([(gqa_q_norm[l], HEAD_DIM)], [(gqa_k_norm[l], HEAD_DIM)], HEAD_DIM ** -0.5)
        bb = _score_bound([(mla_q_nope_norm[l], MLA_NOPE), (mla_q_rope_norm[l], MLA_ROPE)],
                          [(mla_k_nope_norm[l], MLA_NOPE), (mla_k_rope_norm[l], MLA_ROPE)],
                          (MLA_NOPE + MLA_ROPE) ** -0.5)
        bc = _score_bound([(diff_q_norm[l], DIFF_QK)], [(diff_k_norm[l], DIFF_QK)], DIFF_QK ** -0.5)
        vta = _vt_call(va, "vt_gqa")
        vtb = _vt_call(vb, "vt_mla")
        vtc = _vt_call(vc, "vt_diff")
        if not last:
            ctx_kw = dict(tq=n_ctx, q_tile0=ctx_tile, n_q_tiles=1, kv_tile0=ctx_tile, chunks=ctx_chunks)
            oa = _flash_call(ba, qa, ka, va, vta, name="flash_gqa_ctx", **ctx_kw)
            ob = _flash_call(bb, qb, kb, vb, vtb, name="flash_mla_ctx", **ctx_kw)
            oc = _flash_call(bc, qc, kc, vc, vtc, name="flash_diff_ctx", **ctx_kw, **diff_kw)
            y = _merge_call(oa, ob, oc, gm, w_br[l], tm=n_ctx, gm_tile0=ctx_tile, name="branch_merge_ctx")
            ctx = _out_call(y, w_o[l], ctx, mod[l], tm=n_ctx, ctx_stream=True, name="out_proj_ctx")

        lat_kw = dict(q_tile0=0, kv_tile0=0, chunks=lat_chunks)
        oa = _flash_call(ba, qa, ka, va, vta, tq=512, n_q_tiles=n_lat // 512, name="flash_gqa", **lat_kw)
        ob = _flash_call(bb, qb, kb, vb, vtb, tq=2048, n_q_tiles=n_lat // 2048, name="flash_mla", **lat_kw)
        oc = _flash_call(bc, qc, kc, vc, vtc, tq=1024, n_q_tiles=n_lat // 1024, name="flash_diff",
                         **lat_kw, **diff_kw)
        y = _merge_call(oa, ob, oc, gm, w_br[l], tm=512, gm_tile0=0, name="branch_merge")
        x = _out_call(y, w_o[l], x, mod[l], tm=512, ctx_stream=False, name="out_proj")
    return x
```

```python
import functools
import math

import jax
import jax.numpy as jnp
import numpy as np
from jax import lax
from jax.experimental import pallas as pl
from jax.experimental.pallas import tpu as pltpu

F32 = jnp.float32
BF16 = jnp.bfloat16

GRID_W = 64
ROPE_THETA = 10000.0
EPS = 1e-6

D_MODEL = 2048
HEAD_DIM = 128
BRANCH_WIDTH = 1024
GQA_HEADS = 8
GQA_KV_HEADS = 2
GQA_GROUP = GQA_HEADS // GQA_KV_HEADS
MLA_HEADS = 8
MLA_NOPE = 128
MLA_ROPE = 64
MLA_QK_PAD = 256
MLA_KV_RANK = 512
DIFF_HEADS = 8
DIFF_QK = 64

LANES = 128
MXU_N = 256
VMEM_LIMIT = 56 * 1024 * 1024
LOG2E = 1.4426950408889634
SAFE_BOUND_LOG2 = 57.0


def _params(sem, vmem=VMEM_LIMIT):
    return pltpu.CompilerParams(dimension_semantics=sem, vmem_limit_bytes=vmem)


def _resident(shape, index_map):
    return pl.BlockSpec(shape, index_map, pipeline_mode=pl.Buffered(1))


def _rms(y, gain, n):
    ms = jnp.sum(y * y, axis=-1, keepdims=True) * (1.0 / n)
    return y * lax.rsqrt(ms + EPS) * gain


def _seg_rms(y, gain, lo):
    ss = y * y
    s_lo = jnp.sum(jnp.where(lo, ss, 0.0), axis=-1, keepdims=True)
    s_hi = jnp.sum(jnp.where(lo, 0.0, ss), axis=-1, keepdims=True)
    ms = jnp.where(lo, s_lo, s_hi) * (1.0 / DIFF_QK)
    return y * lax.rsqrt(ms + EPS) * gain


def _rope(y, cos, sin_signed):
    return y * cos + pltpu.roll(y, LANES // 2, 1) * sin_signed


def _lane_src_a():
    e = np.arange(HEAD_DIM // 4)
    return np.concatenate([e, 64 + e, 32 + e, 96 + e])


def _lane_src_b():
    e = np.arange(MLA_ROPE // 4)
    pad = -np.ones(32, np.int64)
    return np.concatenate([e, 32 + e, pad, 16 + e, 48 + e, pad])


def _lane_src_c():
    e = np.arange(DIFF_QK // 4)
    x1 = [m * DIFF_QK + a * 32 + e for m in (0, 1) for a in (0, 1)]
    x2 = [m * DIFF_QK + a * 32 + 16 + e for m in (0, 1) for a in (0, 1)]
    return np.concatenate(x1 + x2)


def _permute_lanes(vec, src):
    out = jnp.take(vec, jnp.asarray(np.maximum(src, 0)), axis=-1)
    return jnp.where(jnp.asarray(src >= 0), out, 0.0)


def _sigmoid(z):
    return 1.0 / (1.0 + jnp.exp(-z))


def _dot_wt(x, wt):
    return lax.dot_general(x, wt, (((1,), (1,)), ((), ())), preferred_element_type=F32)


def _ada_kernel(c_ref, w_ref, b_ref, o_ref):
    c = c_ref[...]
    a = (c * _sigmoid(c)).astype(BF16)
    o_ref[0] = jnp.dot(a, w_ref[0].astype(BF16), preferred_element_type=F32) + b_ref[0]


def _ada_call(c_all, w_ada, b_ada):
    depth, d, n3 = w_ada.shape
    tn = 512
    return pl.pallas_call(
        _ada_kernel,
        grid=(depth, n3 // tn),
        in_specs=[pl.BlockSpec((8, d), lambda l, j: (0, 0)),
                  pl.BlockSpec((1, d, tn), lambda l, j: (l, 0, j)),
                  pl.BlockSpec((1, 1, tn), lambda l, j: (l, 0, j))],
        out_specs=pl.BlockSpec((1, 8, tn), lambda l, j: (l, 0, j)),
        out_shape=jax.ShapeDtypeStruct((depth, 8, n3), F32),
        compiler_params=_params(("parallel", "parallel")),
        name="ada_mod",
    )(c_all, w_ada, b_ada.reshape(depth, 1, n3))


def _h_kernel(x_ref, ctx_ref, nw_ref, mod_ref, modc_ref, h_ref, *, lat_tiles):
    def norm_mod(src_ref, m_ref):
        x = src_ref[0]
        ms = jnp.mean(x * x, axis=-1, keepdims=True)
        y = x * lax.rsqrt(ms + EPS) * nw_ref[...]
        h_ref[0] = (y * (1.0 + m_ref[0, 1:2, :]) + m_ref[0, 0:1, :]).astype(BF16)

    t = pl.program_id(1)

    @pl.when(t < lat_tiles)
    def _():
        norm_mod(x_ref, mod_ref)

    @pl.when(t >= lat_tiles)
    def _():
        norm_mod(ctx_ref, modc_ref)


def _h_call(x, ctx, norm_w, mod):
    b, n_lat, d = x.shape
    tr = ctx.shape[1]
    lat_tiles = n_lat // tr
    return pl.pallas_call(
        functools.partial(_h_kernel, lat_tiles=lat_tiles),
        grid=(b, lat_tiles + 1),
        in_specs=[pl.BlockSpec((1, tr, d), lambda bi, t: (bi, jnp.minimum(t, lat_tiles - 1), 0)),
                  pl.BlockSpec((1, tr, d), lambda bi, t: (bi, 0, 0)),
                  pl.BlockSpec((1, d), lambda bi, t: (0, 0)),
                  pl.BlockSpec((1, 3, d), lambda bi, t: (bi, 0, 0)),
                  pl.BlockSpec((1, 3, d), lambda bi, t: (b, 0, 0))],
        out_specs=pl.BlockSpec((1, tr, d), lambda bi, t: (bi, t, 0)),
        out_shape=jax.ShapeDtypeStruct((b, n_lat + tr, d), BF16),
        compiler_params=_params(("parallel", "parallel")),
        name="norm_mod",
    )(x, ctx, norm_w, mod, mod)


def _proj_a_kernel(h_ref, w_ref, gq_ref, gk_ref, cos_ref, sin_ref, q_ref, k_ref, v_ref, *, values):
    h = h_ref[0]
    cos = cos_ref[...]
    sin = sin_ref[...]
    scale = HEAD_DIM ** -0.5 * LOG2E
    n_q = GQA_HEADS * HEAD_DIM // MXU_N
    for c in (range(n_q + 1, n_q + 2) if values else range(n_q + 1)):
        y = _dot_wt(h, w_ref[c * MXU_N:(c + 1) * MXU_N, :])
        for u in range(2):
            yu = y[:, u * LANES:(u + 1) * LANES]
            if c < n_q:
                z = _rope(_rms(yu, gq_ref[...], HEAD_DIM), cos, sin)
                q_ref[0, 2 * c + u] = (z * scale).astype(BF16)
            elif c == n_q:
                z = _rope(_rms(yu, gk_ref[...], HEAD_DIM), cos, sin)
                k_ref[0, u] = z.astype(BF16)
            else:
                v_ref[0, u] = yu.astype(BF16)


def _proj_b_kernel(h_ref, w_ref, wup_ref, gq_ref, gckv_ref, gkr_ref, gkn_ref, cos_ref, sin_ref,
                   q_ref, k_ref, v_ref):
    h = h_ref[0]
    cos = cos_ref[...]
    sin = sin_ref[...]
    scale = (MLA_NOPE + MLA_ROPE) ** -0.5 * LOG2E
    base = MLA_HEADS * MLA_QK_PAD
    y0 = _dot_wt(h, w_ref[base:base + MXU_N, :])
    y1 = _dot_wt(h, w_ref[base + MXU_N:base + 2 * MXU_N, :])
    ms = (jnp.sum(y0 * y0, axis=-1, keepdims=True)
          + jnp.sum(y1 * y1, axis=-1, keepdims=True)) * (1.0 / MLA_KV_RANK)
    inv = lax.rsqrt(ms + EPS)
    ckv = jnp.concatenate([(y0 * inv * gckv_ref[:, 0:MXU_N]).astype(BF16),
                           (y1 * inv * gckv_ref[:, MXU_N:2 * MXU_N]).astype(BF16)], axis=1)
    base += MLA_KV_RANK
    yk = _dot_wt(h, w_ref[base:base + LANES, :])
    kr = _rope(_rms(yk, gkr_ref[...], MLA_ROPE), cos, sin).astype(BF16)
    for hd in range(MLA_HEADS):
        y = _dot_wt(h, w_ref[hd * MLA_QK_PAD:(hd + 1) * MLA_QK_PAD, :])
        nope = _rms(y[:, :LANES], gq_ref[:, :LANES], MLA_NOPE)
        rope = _rope(_rms(y[:, LANES:], gq_ref[:, LANES:], MLA_ROPE), cos, sin)
        q_ref[0, hd, :, 0:LANES] = (nope * scale).astype(BF16)
        q_ref[0, hd, :, LANES:2 * LANES] = (rope * scale).astype(BF16)
    per = MLA_HEADS // 2
    for c in range(2 * per):
        y = jnp.dot(ckv, wup_ref[:, c * MXU_N:(c + 1) * MXU_N], preferred_element_type=F32)
        for u in range(2):
            yu = y[:, u * LANES:(u + 1) * LANES]
            if c < per:
                k_ref[0, 2 * c + u, :, 0:LANES] = _rms(yu, gkn_ref[...], MLA_NOPE).astype(BF16)
                k_ref[0, 2 * c + u, :, LANES:2 * LANES] = kr
            else:
                v_ref[0, 2 * (c - per) + u] = yu.astype(BF16)


def _proj_c_kernel(h_ref, w_ref, gq_ref, gk_ref, cos_ref, sin_ref, q_ref, k_ref, v_ref, *, values):
    h = h_ref[0]
    cos = cos_ref[...]
    sin = sin_ref[...]
    scale = DIFF_QK ** -0.5 * LOG2E
    lo = (lax.broadcasted_iota(jnp.int32, (1, LANES), 1) & 32) == 0
    per = DIFF_HEADS // 2
    for c in (range(2 * per, 3 * per) if values else range(2 * per)):
        y = _dot_wt(h, w_ref[c * MXU_N:(c + 1) * MXU_N, :])
        for u in range(2):
            yu = y[:, u * LANES:(u + 1) * LANES]
            if c < per:
                z = _rope(_seg_rms(yu, gq_ref[...], lo), cos, sin) * scale
                q_ref[0, 2 * c + u, 0] = jnp.where(lo, z, 0.0).astype(BF16)
                q_ref[0, 2 * c + u, 1] = jnp.where(lo, 0.0, z).astype(BF16)
            elif c < 2 * per:
                z = _rope(_seg_rms(yu, gk_ref[...], lo), cos, sin)
                k_ref[0, 2 * (c - per) + u] = z.astype(BF16)
            else:
                v_ref[0, 2 * (c - 2 * per) + u] = yu.astype(BF16)


def _proj_tm(n):
    return n // 8


def _layer_weight(w, layer):
    return _resident((None,) + w.shape[1:], lambda bi, i: (layer, 0, 0))


def _proj_ac_kernel(h_ref, wa_ref, wc_ref, gaq_ref, gak_ref, gcq_ref, gck_ref,
                    cosa_ref, sina_ref, cosc_ref, sinc_ref,
                    qa_ref, ka_ref, va_ref, qc_ref, kc_ref, vc_ref):
    c_refs = (h_ref, wc_ref, gcq_ref, gck_ref, cosc_ref, sinc_ref, qc_ref, kc_ref, vc_ref)
    a_refs = (h_ref, wa_ref, gaq_ref, gak_ref, cosa_ref, sina_ref, qa_ref, ka_ref, va_ref)
    _proj_c_kernel(*c_refs, values=False)
    _proj_a_kernel(*a_refs, values=False)
    _proj_c_kernel(*c_refs, values=True)
    _proj_a_kernel(*a_refs, values=True)


def _proj_ac_call(h, w_a, w_c, layer, gaq, gak, gcq, gck, cos_a, sin_a, cos_c, sin_c):
    b, n, d = h.shape
    tm = _proj_tm(n)
    row = lambda bi, i: (bi, i, 0)
    head = lambda bi, i: (bi, 0, i, 0)
    const = lambda bi, i: (0, 0)
    gain = pl.BlockSpec((1, LANES), const)
    table = pl.BlockSpec((tm, LANES), lambda bi, i: (i, 0))
    return pl.pallas_call(
        _proj_ac_kernel,
        grid=(b, n // tm),
        in_specs=[pl.BlockSpec((1, tm, d), row), _layer_weight(w_a, layer), _layer_weight(w_c, layer),
                  gain, gain, gain, gain, table, table, table, table],
        out_specs=[pl.BlockSpec((1, GQA_HEADS, tm, HEAD_DIM), head),
                   pl.BlockSpec((1, GQA_KV_HEADS, tm, HEAD_DIM), head),
                   pl.BlockSpec((1, GQA_KV_HEADS, tm, HEAD_DIM), head),
                   pl.BlockSpec((1, DIFF_HEADS, 2, tm, LANES), lambda bi, i: (bi, 0, 0, i, 0)),
                   pl.BlockSpec((1, DIFF_HEADS, tm, LANES), head),
                   pl.BlockSpec((1, DIFF_HEADS, tm, LANES), head)],
        out_shape=[jax.ShapeDtypeStruct((b, GQA_HEADS, n, HEAD_DIM), BF16),
                   jax.ShapeDtypeStruct((b, GQA_KV_HEADS, n, HEAD_DIM), BF16),
                   jax.ShapeDtypeStruct((b, GQA_KV_HEADS, n, HEAD_DIM), BF16),
                   jax.ShapeDtypeStruct((b, DIFF_HEADS, 2, n, LANES), BF16),
                   jax.ShapeDtypeStruct((b, DIFF_HEADS, n, LANES), BF16),
                   jax.ShapeDtypeStruct((b, DIFF_HEADS, n, LANES), BF16)],
        compiler_params=_params(("parallel", "parallel")),
        name="proj_gqa_diff",
    )(h, w_a, w_c, gaq, gak, gcq, gck, cos_a, sin_a, cos_c, sin_c)


def _proj_b_call(h, w, w_up, layer, gq, gckv, gkr, gkn, cos, sin):
    b, n, d = h.shape
    tm = _proj_tm(n)
    row = lambda bi, i: (bi, i, 0)
    head = lambda bi, i: (bi, 0, i, 0)
    const = lambda bi, i: (0, 0)
    return pl.pallas_call(
        _proj_b_kernel,
        grid=(b, n // tm),
        in_specs=[pl.BlockSpec((1, tm, d), row),
                  _layer_weight(w, layer),
                  _layer_weight(w_up, layer),
                  pl.BlockSpec((1, MLA_QK_PAD), const),
                  pl.BlockSpec((1, MLA_KV_RANK), const),
                  pl.BlockSpec((1, LANES), const),
                  pl.BlockSpec((1, LANES), const),
                  pl.BlockSpec((tm, LANES), lambda bi, i: (i, 0)),
                  pl.BlockSpec((tm, LANES), lambda bi, i: (i, 0))],
        out_specs=[pl.BlockSpec((1, MLA_HEADS, tm, MLA_QK_PAD), head),
                   pl.BlockSpec((1, MLA_HEADS, tm, MLA_QK_PAD), head),
                   pl.BlockSpec((1, MLA_HEADS, tm, LANES), head)],
        out_shape=[jax.ShapeDtypeStruct((b, MLA_HEADS, n, MLA_QK_PAD), BF16),
                   jax.ShapeDtypeStruct((b, MLA_HEADS, n, MLA_QK_PAD), BF16),
                   jax.ShapeDtypeStruct((b, MLA_HEADS, n, LANES), BF16)],
        compiler_params=_params(("parallel", "parallel")),
        name="proj_mla",
    )(h, w, w_up, gq, gckv, gkr, gkn, cos, sin)


def _gate_kernel(h_ref, w_ref, b_ref, o_ref, *, n_sigmoid_tiles):
    is_sigmoid = pl.program_id(2) < n_sigmoid_tiles
    h = h_ref[0]
    for c in range(o_ref.shape[-1] // MXU_N):
        cols = slice(c * MXU_N, (c + 1) * MXU_N)
        y = _dot_wt(h, w_ref[cols, :])
        t = _sigmoid(y + b_ref[:, cols])
        o_ref[0, :, cols] = jnp.where(is_sigmoid, t, y * t).astype(BF16)


def _gate_call(h, w, layer, bias):
    b, n, d = h.shape
    cols = w.shape[1]
    tm = n // 4
    tn = 3072
    assert (cols - 3 * BRANCH_WIDTH) % tn == 0 and cols % tn == 0
    return pl.pallas_call(
        functools.partial(_gate_kernel, n_sigmoid_tiles=(cols - 3 * BRANCH_WIDTH) // tn),
        grid=(b, n // tm, cols // tn),
        in_specs=[pl.BlockSpec((1, tm, d), lambda bi, i, j: (bi, i, 0)),
                  pl.BlockSpec((None, tn, d), lambda bi, i, j: (layer, j, 0)),
                  pl.BlockSpec((1, tn), lambda bi, i, j: (0, j))],
        out_specs=pl.BlockSpec((1, tm, tn), lambda bi, i, j: (bi, i, j)),
        out_shape=jax.ShapeDtypeStruct((b, n, cols), BF16),
        compiler_params=_params(("parallel", "parallel", "parallel")),
        name="gate_proj",
    )(h, w, bias)


def _flash_kernel(*refs, group, tq, chunks, diff, lam_init):
    if diff:
        bound_ref, q_ref, k_ref, v_ref, vt_ref, lamv_ref, sub_ref, o_ref = refs
    else:
        bound_ref, q_ref, k_ref, v_ref, vt_ref, o_ref = refs
    dq = q_ref.shape[-1]
    nt = (((1,), (1,)), ((), ()))

    q = q_ref[0, 0].reshape(group * tq, dq)
    half = group * tq // 2
    halves = (slice(0, half), slice(half, 2 * half))
    bound = bound_ref[0]
    safe = bound <= SAFE_BOUND_LOG2

    def finish(out):
        if diff:
            lv = lamv_ref[0]
            lam = (jnp.exp(jnp.sum(lv[0:1] * lv[1:2], axis=-1, keepdims=True))
                   - jnp.exp(jnp.sum(lv[2:3] * lv[3:4], axis=-1, keepdims=True)) + lam_init)
            o = out[:tq] - lam * out[tq:]
            o_ref[0] = (_rms(o, sub_ref[...], LANES) * (1.0 - lam_init)).astype(BF16)
        else:
            for g in range(group):
                o_ref[0, :, g * LANES:(g + 1) * LANES] = out[g * tq:(g + 1) * tq].astype(BF16)

    @pl.when(safe)
    def _():
        accs = [None, None]

        def k_dot_q(start, size):
            k = k_ref[0, 0, start:start + size, :]
            return [lax.dot_general(k, q[rows], nt, preferred_element_type=F32) for rows in halves]

        def vt_dot_pt(pts, start, size):
            vt = vt_ref[0, 0, :, start:start + size]
            for i, pt in enumerate(pts):
                t = jnp.dot(vt, pt, preferred_element_type=F32)
                accs[i] = t if accs[i] is None else accs[i] + t

        pending = None
        for start, size in chunks:
            ss = k_dot_q(start, size)
            if pending is not None:
                vt_dot_pt(*pending)
            pending = ([jnp.exp2(s - bound).astype(BF16) for s in ss], start, size)
        vt_dot_pt(*pending)
        acc = jnp.concatenate(accs, axis=1)
        finish((acc[:LANES] / acc[LANES:LANES + 1]).T)

    @pl.when(jnp.logical_not(safe))
    def _():
        m = l = acc = None
        for start, size in chunks:
            k = k_ref[0, 0, start:start + size, :]
            v = v_ref[0, 0, start:start + size, :]
            s = lax.dot_general(q, k, nt, preferred_element_type=F32)
            m_cur = jnp.max(s, axis=-1, keepdims=True)
            if m is None:
                m = m_cur
                p = jnp.exp2(s - m)
                l = jnp.sum(p, axis=-1, keepdims=True)
                acc = jnp.dot(p.astype(BF16), v, preferred_element_type=F32)
            else:
                m_new = jnp.maximum(m, m_cur)
                alpha = jnp.exp2(m - m_new)
                p = jnp.exp2(s - m_new)
                l = alpha * l + jnp.sum(p, axis=-1, keepdims=True)
                acc = alpha * acc + jnp.dot(p.astype(BF16), v, preferred_element_type=F32)
                m = m_new
        finish(acc / l)


VT_ROWS = LANES + 16


def _vt_kernel(v_ref, o_ref):
    n = v_ref.shape[2]
    for hd in range(v_ref.shape[1]):
        for j in range(n // LANES):
            cols = slice(j * LANES, (j + 1) * LANES)
            o_ref[0, hd, 0:LANES, cols] = v_ref[0, hd, cols, :].T
        o_ref[0, hd, LANES:VT_ROWS, :] = jnp.ones((VT_ROWS - LANES, n), BF16)


def _vt_call(v, name):
    b, heads, n, dv = v.shape
    hb = min(heads, 4)
    return pl.pallas_call(
        _vt_kernel,
        grid=(b, heads // hb),
        in_specs=[pl.BlockSpec((1, hb, n, dv), lambda bi, h: (bi, h, 0, 0))],
        out_specs=pl.BlockSpec((1, hb, VT_ROWS, n), lambda bi, h: (bi, h, 0, 0)),
        out_shape=jax.ShapeDtypeStruct((b, heads, VT_ROWS, n), BF16),
        compiler_params=_params(("parallel", "parallel")),
        name=name,
    )(v)


def _flash_call(bound, q, k, v, vt, *, tq, q_tile0, n_q_tiles, kv_tile0, chunks,
                diff=False, lamv=None, sub=None, lam_init=0.0, name="flash"):
    b, heads, group, n, dq = q.shape
    kv_len = sum(size for _, size in chunks)
    out_w = LANES if diff else group * LANES
    in_specs = [pl.BlockSpec(memory_space=pltpu.SMEM),
                pl.BlockSpec((1, 1, group, tq, dq), lambda bi, h, i: (bi, h, 0, q_tile0 + i, 0)),
                pl.BlockSpec((1, 1, kv_len, dq), lambda bi, h, i: (bi, h, kv_tile0, 0)),
                pl.BlockSpec((1, 1, kv_len, LANES), lambda bi, h, i: (bi, h, kv_tile0, 0)),
                pl.BlockSpec((1, 1, VT_ROWS, kv_len), lambda bi, h, i: (bi, h, 0, kv_tile0))]
    args = [bound, q, k, v, vt]
    if diff:
        in_specs += [pl.BlockSpec((1, 4, DIFF_QK), lambda bi, h, i: (0, 0, 0)),
                     pl.BlockSpec((1, LANES), lambda bi, h, i: (0, 0))]
        args += [lamv, sub]
    kern = functools.partial(_flash_kernel, group=group, tq=tq, chunks=chunks, diff=diff,
                             lam_init=lam_init)
    return pl.pallas_call(
        kern,
        grid=(b, heads, n_q_tiles),
        in_specs=in_specs,
        out_specs=pl.BlockSpec((1, tq, out_w), lambda bi, h, i: (bi, i, h)),
        out_shape=jax.ShapeDtypeStruct((b, n_q_tiles * tq, BRANCH_WIDTH), BF16),
        compiler_params=_params(("parallel", "parallel", "parallel")),
        name=name,
    )(*args)


def _score_bound(q_terms, k_terms, scale):
    q2 = sum(d * jnp.max(jnp.square(g.astype(F32))) for g, d in q_terms)
    k2 = sum(d * jnp.max(jnp.square(g.astype(F32))) for g, d in k_terms)
    return (1.01 * scale * LOG2E * jnp.sqrt(q2 * k2)).reshape(1)


N_CHUNK = 512


def _merge_kernel(oa_ref, ob_ref, oc_ref, g_ref, ma_ref, mb_ref, mc_ref, wa_ref, wb_ref, wc_ref,
                  y_ref):
    ogs = [o_ref[0] * g_ref[0, :, r * BRANCH_WIDTH:(r + 1) * BRANCH_WIDTH]
           for r, o_ref in enumerate((oa_ref, ob_ref, oc_ref))]
    w_refs = (wa_ref, wb_ref, wc_ref)
    for c in range(y_ref.shape[-1] // N_CHUNK):
        cols = slice(c * N_CHUNK, (c + 1) * N_CHUNK)
        acc = None
        for r, m_ref in enumerate((ma_ref, mb_ref, mc_ref)):
            z = jnp.dot(ogs[r], w_refs[r][:, cols], preferred_element_type=F32)
            t = m_ref[0, :, cols].astype(F32) * z
            acc = t if acc is None else acc + t
        y_ref[0, :, cols] = acc.astype(BF16)


def _merge_call(oa, ob, oc, gm, w_brs, layer, *, tm, gm_tile0, name):
    b, rows, _ = oa.shape
    d = w_brs[0].shape[-1]
    gate_w = 3 * BRANCH_WIDTH
    assert (3 * d) % gate_w == 0
    o_spec = pl.BlockSpec((1, tm, BRANCH_WIDTH), lambda bi, i: (bi, i, 0))
    m_specs = [pl.BlockSpec((1, tm, d), lambda bi, i, r=r: (bi, gm_tile0 + i, r)) for r in range(3)]
    return pl.pallas_call(
        _merge_kernel,
        grid=(b, rows // tm),
        in_specs=[o_spec, o_spec, o_spec,
                  pl.BlockSpec((1, tm, gate_w), lambda bi, i: (bi, gm_tile0 + i, 3 * d // gate_w)),
                  *m_specs,
                  *[_layer_weight(w, layer) for w in w_brs]],
        out_specs=pl.BlockSpec((1, tm, d), lambda bi, i: (bi, i, 0)),
        out_shape=jax.ShapeDtypeStruct((b, rows, d), BF16),
        compiler_params=_params(("parallel", "parallel")),
        name=name,
    )(oa, ob, oc, gm, gm, gm, gm, *w_brs)


def _out_kernel(y_ref, w_ref, x_ref, mod_ref, o_ref):
    y = y_ref[0]
    for c in range(o_ref.shape[-1] // N_CHUNK):
        cols = slice(c * N_CHUNK, (c + 1) * N_CHUNK)
        out = jnp.dot(y, w_ref[:, cols], preferred_element_type=F32)
        o_ref[0, :, cols] = x_ref[0, :, cols] + mod_ref[0, 2:3, cols] * out


def _out_call(y, w_out, layer, xs, mod, *, tm, ctx_stream, name):
    b, rows, d = xs.shape
    return pl.pallas_call(
        _out_kernel,
        grid=(b, rows // tm),
        in_specs=[pl.BlockSpec((1, tm, d), lambda bi, i: (bi, i, 0)),
                  _layer_weight(w_out, layer),
                  pl.BlockSpec((1, tm, d), lambda bi, i: (bi, i, 0)),
                  pl.BlockSpec((1, 3, d), lambda bi, i: (b if ctx_stream else bi, 0, 0))],
        out_specs=pl.BlockSpec((1, tm, d), lambda bi, i: (bi, i, 0)),
        out_shape=jax.ShapeDtypeStruct((b, rows, d), F32),
        compiler_params=_params(("parallel", "parallel")),
        name=name,
    )(y, w_out, xs, mod)


def _rope_tables(n_lat, n_ctx, rot_dim, src):
    axis_dim = rot_dim // 2
    t = jnp.arange(n_lat, dtype=jnp.int32)
    pos_row = (t // GRID_W).astype(F32)
    pos_col = (t % GRID_W).astype(F32)
    inv_freq = ROPE_THETA ** (-jnp.arange(0, axis_dim, 2, dtype=F32) / axis_dim)
    ang_r = pos_row[:, None] * inv_freq
    ang_c = pos_col[:, None] * inv_freq
    ang = jnp.concatenate([ang_r, ang_r, ang_c, ang_c], axis=-1)
    cos, sin = jnp.cos(ang), jnp.sin(ang)
    lane = jnp.arange(rot_dim)
    sign = jnp.where((lane % axis_dim) < axis_dim // 2, -1.0, 1.0).astype(F32)
    pad = jnp.asarray(src < 0)
    lanes = jnp.asarray(np.maximum(src, 0) % rot_dim)
    cos = jnp.where(pad, 1.0, jnp.take(cos, lanes, axis=1))
    sin = jnp.where(pad, 0.0, jnp.take(sin * sign, lanes, axis=1))
    cos = jnp.concatenate([cos, jnp.ones((n_ctx, LANES), F32)], axis=0)
    sin = jnp.concatenate([sin, jnp.zeros((n_ctx, LANES), F32)], axis=0)
    return cos, sin


PREP_MAX_ROWS = 256
_IDENT = np.arange(LANES)


def _prep_runs():
    o_gk = GQA_HEADS * HEAD_DIM
    o_gv = o_gk + GQA_KV_HEADS * HEAD_DIM
    o_mq = o_gv + GQA_KV_HEADS * HEAD_DIM
    o_ckv = o_mq + MLA_HEADS * (MLA_NOPE + MLA_ROPE)
    o_kr = o_ckv + MLA_KV_RANK
    o_dq = o_kr + MLA_ROPE
    o_dv = o_dq + 2 * DIFF_HEADS * 2 * DIFF_QK
    o_gate = o_dv + DIFF_HEADS * LANES
    o_merge = o_gate + 3 * BRANCH_WIDTH
    units = []
    for u in range(o_gv // LANES):
        units.append((0, u * LANES, u * LANES, _lane_src_a()))
    for u in range((o_mq - o_gv) // LANES):
        units.append((0, o_gv + u * LANES, o_gv + u * LANES, _IDENT))
    for hd in range(MLA_HEADS):
        base = o_mq + hd * (MLA_NOPE + MLA_ROPE)
        units.append((1, hd * MLA_QK_PAD, base, _IDENT))
        units.append((1, hd * MLA_QK_PAD + LANES, base + MLA_NOPE, _lane_src_b()))
    for u in range(MLA_KV_RANK // LANES):
        units.append((1, MLA_HEADS * MLA_QK_PAD + u * LANES, o_ckv + u * LANES, _IDENT))
    units.append((1, MLA_HEADS * MLA_QK_PAD + MLA_KV_RANK, o_kr, _lane_src_b()))
    for u in range((o_dv - o_dq) // LANES):
        units.append((2, u * LANES, o_dq + u * LANES, _lane_src_c()))
    for u in range((o_gate - o_dv) // LANES):
        units.append((2, o_dv - o_dq + u * LANES, o_dv + u * LANES, _IDENT))
    for u in range(3 * D_MODEL // LANES):
        units.append((3, u * LANES, o_merge + u * LANES, _IDENT))
    for u in range(3 * BRANCH_WIDTH // LANES):
        units.append((3, 3 * D_MODEL + u * LANES, o_gate + u * LANES, _IDENT))

    runs = []
    for out, dst0, base, lane_src in units:
        for t, idx in enumerate(lane_src):
            src = None if idx < 0 else base + int(idx)
            last = runs[-1] if runs else None
            if (last is not None and last[0] == out and last[1] + last[3] == dst0 + t
                    and last[3] < PREP_MAX_ROWS
                    and ((src is None and last[2] is None)
                         or (src is not None and last[2] is not None and last[2] + last[3] == src))):
                runs[-1] = (out, last[1], last[2], last[3] + 1)
            else:
                runs.append((out, dst0 + t, src, 1))
    assert all(dst % 16 == 0 and rows % 16 == 0 and (src is None or src % 16 == 0)
               for _, dst, src, rows in runs)
    return runs


def _prep_kernel(w_ref, *o_refs, runs):
    for out, dst, src, rows in runs:
        if src is None:
            o_refs[out][0, dst:dst + rows, :] = jnp.zeros((rows, o_refs[out].shape[-1]), BF16)
        else:
            o_refs[out][0, dst:dst + rows, :] = w_ref[0, src:src + rows, :].astype(BF16)


def _prep_call(w_in_t):
    depth, total, d = w_in_t.shape
    runs = _prep_runs()
    heights = [max(dst + rows for out, dst, _, rows in runs if out == o) for o in range(4)]
    td = 256
    return pl.pallas_call(
        functools.partial(_prep_kernel, runs=runs),
        grid=(depth, d // td),
        in_specs=[pl.BlockSpec((1, total, td), lambda l, i: (l, 0, i))],
        out_specs=[pl.BlockSpec((1, ht, td), lambda l, i: (l, 0, i)) for ht in heights],
        out_shape=[jax.ShapeDtypeStruct((depth, ht, d), BF16) for ht in heights],
        compiler_params=_params(("parallel", "parallel")),
        name="weight_layout",
    )(w_in_t)


def kernel(x, c, ctx, c_ctx, norm_w, w_ada, b_ada, w_in, b_merge, gqa_q_norm, gqa_k_norm, mla_q_nope_norm, mla_q_rope_norm, mla_kv_norm, mla_w_uk, mla_w_uv, mla_k_nope_norm, mla_k_rope_norm, diff_q_norm, diff_k_norm, diff_lambda_q1, diff_lambda_k1, diff_lambda_q2, diff_lambda_k2, diff_subln, w_br_gqa, w_br_mla, w_br_diff, w_out):
    b, n_lat, d = x.shape
    n_ctx = ctx.shape[1]
    n = n_lat + n_ctx
    depth = w_in.shape[0]
    assert n_lat % 2048 == 0 and n_ctx == 256 and n % 8 == 0 and b < 8 and d == D_MODEL

    c_all = jnp.concatenate([c, c_ctx[None], jnp.zeros((8 - b - 1, d), F32)], axis=0)
    mod = _ada_call(c_all, w_ada, b_ada).reshape(depth, 8, 3, d)

    w_a, w_b, w_c, w_g = _prep_call(jnp.swapaxes(w_in, 1, 2))
    w_ukv = jnp.concatenate([mla_w_uk, mla_w_uv], axis=-1).astype(BF16)
    w_brs = [w.astype(BF16) for w in (w_br_gqa, w_br_mla, w_br_diff)]
    w_o = w_out.astype(BF16)
    bias_g = jnp.concatenate([b_merge, jnp.zeros((depth, 3 * BRANCH_WIDTH), F32)], axis=-1)
    src_a, src_b, src_c = _lane_src_a(), _lane_src_b(), _lane_src_c()
    g_aq = _permute_lanes(gqa_q_norm, src_a)
    g_ak = _permute_lanes(gqa_k_norm, src_a)
    g_bq = jnp.concatenate([mla_q_nope_norm, _permute_lanes(mla_q_rope_norm, src_b)], axis=-1)
    g_kr = _permute_lanes(mla_k_rope_norm, src_b)
    g_dq = _permute_lanes(jnp.tile(diff_q_norm, (1, 2)), src_c)
    g_dk = _permute_lanes(jnp.tile(diff_k_norm, (1, 2)), src_c)
    lamv = jnp.stack([diff_lambda_q1, diff_lambda_k1, diff_lambda_q2, diff_lambda_k2], axis=1)

    cos_a, sin_a = _rope_tables(n_lat, n_ctx, HEAD_DIM, src_a)
    cos_b, sin_b = _rope_tables(n_lat, n_ctx, MLA_ROPE, src_b)
    cos_c, sin_c = _rope_tables(n_lat, n_ctx, DIFF_QK, src_c)

    lat_chunks = tuple((s, 1024) for s in range(0, n_lat, 1024)) + ((n_lat, n_ctx),)
    assert sum(size for _, size in lat_chunks) == n
    ctx_chunks = ((0, n_ctx),)
    ctx_tile = n_lat // n_ctx

    for l in range(depth):
        last = l == depth - 1
        lam_init = 0.8 - 0.6 * math.exp(-0.3 * l)
        h = _h_call(x, ctx, norm_w[l][None], mod[l])

        qa, ka, va, qc, kc, vc = _proj_ac_call(
            h, w_a, w_c, l, g_aq[l][None], g_ak[l][None], g_dq[l][None], g_dk[l][None],
            cos_a, sin_a, cos_c, sin_c)
        qb, kb, vb = _proj_b_call(h, w_b, w_ukv, l, g_bq[l][None], mla_kv_norm[l][None], g_kr[l][None],
                                  mla_k_nope_norm[l][None], cos_b, sin_b)
        gm = _gate_call(h, w_g, l, bias_g[l][None])

        qa = qa.reshape(b, GQA_KV_HEADS, GQA_GROUP, n, HEAD_DIM)
        qb = qb.reshape(b, MLA_HEADS, 1, n, MLA_QK_PAD)
        diff_kw = dict(diff=True, lamv=lamv[l][None], sub=diff_subln[l][None], lam_init=lam_init)

        ba = _score_bound([(gqa_q_norm[l], HEAD_DIM)], [(gqa_k_norm[l], HEAD_DIM)], HEAD_DIM ** -0.5)
        bb = _score_bound([(mla_q_nope_norm[l], MLA_NOPE), (mla_q_rope_norm[l], MLA_ROPE)],
                          [(mla_k_nope_norm[l], MLA_NOPE), (mla_k_rope_norm[l], MLA_ROPE)],
                          (MLA_NOPE + MLA_ROPE) ** -0.5)
        bc = _score_bound([(diff_q_norm[l], DIFF_QK)], [(diff_k_norm[l], DIFF_QK)], DIFF_QK ** -0.5)
        vta = _vt_call(va, "vt_gqa")
        vtb = _vt_call(vb, "vt_mla")
        vtc = _vt_call(vc, "vt_diff")
        if not last:
            ctx_kw = dict(tq=n_ctx, q_tile0=ctx_tile, n_q_tiles=1, kv_tile0=ctx_tile, chunks=ctx_chunks)
            oa = _flash_call(ba, qa, ka, va, vta, name="flash_gqa_ctx", **ctx_kw)
            ob = _flash_call(bb, qb, kb, vb, vtb, name="flash_mla_ctx", **ctx_kw)
            oc = _flash_call(bc, qc, kc, vc, vtc, name="flash_diff_ctx", **ctx_kw, **diff_kw)
            y = _merge_call(oa, ob, oc, gm, w_brs, l, tm=n_ctx, gm_tile0=ctx_tile, name="branch_merge_ctx")
            ctx = _out_call(y, w_o, l, ctx, mod[l], tm=n_ctx, ctx_stream=True, name="out_proj_ctx")

        lat_kw = dict(q_tile0=0, kv_tile0=0, chunks=lat_chunks)
        oa = _flash_call(ba, qa, ka, va, vta, tq=512, n_q_tiles=n_lat // 512, name="flash_gqa", **lat_kw)
        ob = _flash_call(bb, qb, kb, vb, vtb, tq=2048, n_q_tiles=n_lat // 2048, name="flash_mla", **lat_kw)
        oc = _flash_call(bc, qc, kc, vc, vtc, tq=1024, n_q_tiles=n_lat // 1024, name="flash_diff",
                         **lat_kw, **diff_kw)
        y = _merge_call(oa, ob, oc, gm, w_brs, l, tm=512, gm_tile0=0, name="branch_merge")
        x = _out_call(y, w_o, l, x, mod[l], tm=512, ctx_stream=False, name="out_proj")
    return x
```

```python
import functools
import math

import jax
import jax.numpy as jnp
import numpy as np
from jax import lax
from jax.experimental import pallas as pl
from jax.experimental.pallas import tpu as pltpu

F32 = jnp.float32
BF16 = jnp.bfloat16

GRID_W = 64
ROPE_THETA = 10000.0
EPS = 1e-6

D_MODEL = 2048
HEAD_DIM = 128
BRANCH_WIDTH = 1024
GQA_HEADS = 8
GQA_KV_HEADS = 2
GQA_GROUP = GQA_HEADS // GQA_KV_HEADS
MLA_HEADS = 8
MLA_NOPE = 128
MLA_ROPE = 64
MLA_QK_PAD = 256
MLA_KV_RANK = 512
DIFF_HEADS = 8
DIFF_QK = 64

LANES = 128
MXU_N = 256
VMEM_LIMIT = 56 * 1024 * 1024
LOG2E = 1.4426950408889634
SAFE_BOUND_LOG2 = 57.0


def _params(sem, vmem=VMEM_LIMIT):
    return pltpu.CompilerParams(dimension_semantics=sem, vmem_limit_bytes=vmem)


def _resident(shape, index_map):
    return pl.BlockSpec(shape, index_map, pipeline_mode=pl.Buffered(1))


def _rms(y, gain, n):
    ms = jnp.sum(y * y, axis=-1, keepdims=True) * (1.0 / n)
    return y * lax.rsqrt(ms + EPS) * gain


def _seg_rms(y, gain, lo):
    ss = y * y
    s_lo = jnp.sum(jnp.where(lo, ss, 0.0), axis=-1, keepdims=True)
    s_hi = jnp.sum(jnp.where(lo, 0.0, ss), axis=-1, keepdims=True)
    ms = jnp.where(lo, s_lo, s_hi) * (1.0 / DIFF_QK)
    return y * lax.rsqrt(ms + EPS) * gain


def _rope(y, cos, sin_signed):
    return y * cos + pltpu.roll(y, LANES // 2, 1) * sin_signed


def _lane_src_a():
    e = np.arange(HEAD_DIM // 4)
    return np.concatenate([e, 64 + e, 32 + e, 96 + e])


def _lane_src_b():
    e = np.arange(MLA_ROPE // 4)
    pad = -np.ones(32, np.int64)
    return np.concatenate([e, 32 + e, pad, 16 + e, 48 + e, pad])


def _lane_src_c():
    e = np.arange(DIFF_QK // 4)
    x1 = [m * DIFF_QK + a * 32 + e for m in (0, 1) for a in (0, 1)]
    x2 = [m * DIFF_QK + a * 32 + 16 + e for m in (0, 1) for a in (0, 1)]
    return np.concatenate(x1 + x2)


def _permute_lanes(vec, src):
    out = jnp.take(vec, jnp.asarray(np.maximum(src, 0)), axis=-1)
    return jnp.where(jnp.asarray(src >= 0), out, 0.0)


def _sigmoid(z):
    return 1.0 / (1.0 + jnp.exp(-z))


def _dot_wt(x, wt):
    return lax.dot_general(x, wt, (((1,), (1,)), ((), ())), preferred_element_type=F32)


def _ada_kernel(c_ref, w_ref, b_ref, o_ref):
    c = c_ref[...]
    a = (c * _sigmoid(c)).astype(BF16)
    o_ref[0] = jnp.dot(a, w_ref[0].astype(BF16), preferred_element_type=F32) + b_ref[0]


def _ada_call(c_all, w_ada, b_ada):
    depth, d, n3 = w_ada.shape
    tn = 512
    return pl.pallas_call(
        _ada_kernel,
        grid=(depth, n3 // tn),
        in_specs=[pl.BlockSpec((8, d), lambda l, j: (0, 0)),
                  pl.BlockSpec((1, d, tn), lambda l, j: (l, 0, j)),
                  pl.BlockSpec((1, 1, tn), lambda l, j: (l, 0, j))],
        out_specs=pl.BlockSpec((1, 8, tn), lambda l, j: (l, 0, j)),
        out_shape=jax.ShapeDtypeStruct((depth, 8, n3), F32),
        compiler_params=_params(("parallel", "parallel")),
        name="ada_mod",
    )(c_all, w_ada, b_ada.reshape(depth, 1, n3))


def _h_kernel(x_ref, ctx_ref, nw_ref, mod_ref, modc_ref, h_ref, *, lat_tiles):
    def norm_mod(src_ref, m_ref):
        x = src_ref[0]
        ms = jnp.mean(x * x, axis=-1, keepdims=True)
        y = x * lax.rsqrt(ms + EPS) * nw_ref[...]
        h_ref[0] = (y * (1.0 + m_ref[0, 1:2, :]) + m_ref[0, 0:1, :]).astype(BF16)

    t = pl.program_id(1)

    @pl.when(t < lat_tiles)
    def _():
        norm_mod(x_ref, mod_ref)

    @pl.when(t >= lat_tiles)
    def _():
        norm_mod(ctx_ref, modc_ref)


def _h_call(x, ctx, norm_w, mod):
    b, n_lat, d = x.shape
    tr = ctx.shape[1]
    lat_tiles = n_lat // tr
    return pl.pallas_call(
        functools.partial(_h_kernel, lat_tiles=lat_tiles),
        grid=(b, lat_tiles + 1),
        in_specs=[pl.BlockSpec((1, tr, d), lambda bi, t: (bi, jnp.minimum(t, lat_tiles - 1), 0)),
                  pl.BlockSpec((1, tr, d), lambda bi, t: (bi, 0, 0)),
                  pl.BlockSpec((1, d), lambda bi, t: (0, 0)),
                  pl.BlockSpec((1, 3, d), lambda bi, t: (bi, 0, 0)),
                  pl.BlockSpec((1, 3, d), lambda bi, t: (b, 0, 0))],
        out_specs=pl.BlockSpec((1, tr, d), lambda bi, t: (bi, t, 0)),
        out_shape=jax.ShapeDtypeStruct((b, n_lat + tr, d), BF16),
        compiler_params=_params(("parallel", "parallel")),
        name="norm_mod",
    )(x, ctx, norm_w, mod, mod)


def _proj_a_kernel(h_ref, w_ref, gq_ref, gk_ref, cos_ref, sin_ref, q_ref, k_ref, v_ref, *, values):
    h = h_ref[0]
    cos = cos_ref[...]
    sin = sin_ref[...]
    scale = HEAD_DIM ** -0.5 * LOG2E
    n_q = GQA_HEADS * HEAD_DIM // MXU_N
    for c in (range(n_q + 1, n_q + 2) if values else range(n_q + 1)):
        y = _dot_wt(h, w_ref[c * MXU_N:(c + 1) * MXU_N, :])
        for u in range(2):
            yu = y[:, u * LANES:(u + 1) * LANES]
            if c < n_q:
                z = _rope(_rms(yu, gq_ref[...], HEAD_DIM), cos, sin)
                q_ref[0, 2 * c + u] = (z * scale).astype(BF16)
            elif c == n_q:
                z = _rope(_rms(yu, gk_ref[...], HEAD_DIM), cos, sin)
                k_ref[0, u] = z.astype(BF16)
            else:
                v_ref[0, u] = yu.astype(BF16)


def _proj_b_kernel(h_ref, w_ref, wup_ref, gq_ref, gckv_ref, gkr_ref, gkn_ref, cos_ref, sin_ref,
                   q_ref, k_ref, v_ref):
    h = h_ref[0]
    cos = cos_ref[...]
    sin = sin_ref[...]
    scale = (MLA_NOPE + MLA_ROPE) ** -0.5 * LOG2E
    base = MLA_HEADS * MLA_QK_PAD
    y0 = _dot_wt(h, w_ref[base:base + MXU_N, :])
    y1 = _dot_wt(h, w_ref[base + MXU_N:base + 2 * MXU_N, :])
    ms = (jnp.sum(y0 * y0, axis=-1, keepdims=True)
          + jnp.sum(y1 * y1, axis=-1, keepdims=True)) * (1.0 / MLA_KV_RANK)
    inv = lax.rsqrt(ms + EPS)
    ckv = jnp.concatenate([(y0 * inv * gckv_ref[:, 0:MXU_N]).astype(BF16),
                           (y1 * inv * gckv_ref[:, MXU_N:2 * MXU_N]).astype(BF16)], axis=1)
    base += MLA_KV_RANK
    yk = _dot_wt(h, w_ref[base:base + LANES, :])
    kr = _rope(_rms(yk, gkr_ref[...], MLA_ROPE), cos, sin).astype(BF16)
    for hd in range(MLA_HEADS):
        y = _dot_wt(h, w_ref[hd * MLA_QK_PAD:(hd + 1) * MLA_QK_PAD, :])
        nope = _rms(y[:, :LANES], gq_ref[:, :LANES], MLA_NOPE)
        rope = _rope(_rms(y[:, LANES:], gq_ref[:, LANES:], MLA_ROPE), cos, sin)
        q_ref[0, hd, :, 0:LANES] = (nope * scale).astype(BF16)
        q_ref[0, hd, :, LANES:2 * LANES] = (rope * scale).astype(BF16)
    per = MLA_HEADS // 2
    for c in range(2 * per):
        y = jnp.dot(ckv, wup_ref[:, c * MXU_N:(c + 1) * MXU_N], preferred_element_type=F32)
        for u in range(2):
            yu = y[:, u * LANES:(u + 1) * LANES]
            if c < per:
                k_ref[0, 2 * c + u, :, 0:LANES] = _rms(yu, gkn_ref[...], MLA_NOPE).astype(BF16)
                k_ref[0, 2 * c + u, :, LANES:2 * LANES] = kr
            else:
                v_ref[0, 2 * (c - per) + u] = yu.astype(BF16)


def _proj_c_kernel(h_ref, w_ref, gq_ref, gk_ref, cos_ref, sin_ref, q_ref, k_ref, v_ref, *, values):
    h = h_ref[0]
    cos = cos_ref[...]
    sin = sin_ref[...]
    scale = DIFF_QK ** -0.5 * LOG2E
    lo = (lax.broadcasted_iota(jnp.int32, (1, LANES), 1) & 32) == 0
    per = DIFF_HEADS // 2
    for c in (range(2 * per, 3 * per) if values else range(2 * per)):
        y = _dot_wt(h, w_ref[c * MXU_N:(c + 1) * MXU_N, :])
        for u in range(2):
            yu = y[:, u * LANES:(u + 1) * LANES]
            if c < per:
                z = _rope(_seg_rms(yu, gq_ref[...], lo), cos, sin) * scale
                q_ref[0, 2 * c + u, 0] = jnp.where(lo, z, 0.0).astype(BF16)
                q_ref[0, 2 * c + u, 1] = jnp.where(lo, 0.0, z).astype(BF16)
            elif c < 2 * per:
                z = _rope(_seg_rms(yu, gk_ref[...], lo), cos, sin)
                k_ref[0, 2 * (c - per) + u] = z.astype(BF16)
            else:
                v_ref[0, 2 * (c - 2 * per) + u] = yu.astype(BF16)


def _proj_tm(n):
    return n // 8


def _layer_weight(w, layer):
    return _resident((None,) + w.shape[1:], lambda bi, i: (layer, 0, 0))


def _proj_ac_kernel(h_ref, wa_ref, wc_ref, gaq_ref, gak_ref, gcq_ref, gck_ref,
                    cosa_ref, sina_ref, cosc_ref, sinc_ref,
                    qa_ref, ka_ref, va_ref, qc_ref, kc_ref, vc_ref):
    c_refs = (h_ref, wc_ref, gcq_ref, gck_ref, cosc_ref, sinc_ref, qc_ref, kc_ref, vc_ref)
    a_refs = (h_ref, wa_ref, gaq_ref, gak_ref, cosa_ref, sina_ref, qa_ref, ka_ref, va_ref)
    _proj_c_kernel(*c_refs, values=False)
    _proj_a_kernel(*a_refs, values=False)
    _proj_c_kernel(*c_refs, values=True)
    _proj_a_kernel(*a_refs, values=True)


def _proj_ac_call(h, w_a, w_c, layer, gaq, gak, gcq, gck, cos_a, sin_a, cos_c, sin_c):
    b, n, d = h.shape
    tm = _proj_tm(n)
    row = lambda bi, i: (bi, i, 0)
    head = lambda bi, i: (bi, 0, i, 0)
    const = lambda bi, i: (0, 0)
    gain = pl.BlockSpec((1, LANES), const)
    table = pl.BlockSpec((tm, LANES), lambda bi, i: (i, 0))
    return pl.pallas_call(
        _proj_ac_kernel,
        grid=(b, n // tm),
        in_specs=[pl.BlockSpec((1, tm, d), row), _layer_weight(w_a, layer), _layer_weight(w_c, layer),
                  gain, gain, gain, gain, table, table, table, table],
        out_specs=[pl.BlockSpec((1, GQA_HEADS, tm, HEAD_DIM), head),
                   pl.BlockSpec((1, GQA_KV_HEADS, tm, HEAD_DIM), head),
                   pl.BlockSpec((1, GQA_KV_HEADS, tm, HEAD_DIM), head),
                   pl.BlockSpec((1, DIFF_HEADS, 2, tm, LANES), lambda bi, i: (bi, 0, 0, i, 0)),
                   pl.BlockSpec((1, DIFF_HEADS, tm, LANES), head),
                   pl.BlockSpec((1, DIFF_HEADS, tm, LANES), head)],
        out_shape=[jax.ShapeDtypeStruct((b, GQA_HEADS, n, HEAD_DIM), BF16),
                   jax.ShapeDtypeStruct((b, GQA_KV_HEADS, n, HEAD_DIM), BF16),
                   jax.ShapeDtypeStruct((b, GQA_KV_HEADS, n, HEAD_DIM), BF16),
                   jax.ShapeDtypeStruct((b, DIFF_HEADS, 2, n, LANES), BF16),
                   jax.ShapeDtypeStruct((b, DIFF_HEADS, n, LANES), BF16),
                   jax.ShapeDtypeStruct((b, DIFF_HEADS, n, LANES), BF16)],
        compiler_params=_params(("parallel", "parallel")),
        name="proj_gqa_diff",
    )(h, w_a, w_c, gaq, gak, gcq, gck, cos_a, sin_a, cos_c, sin_c)


def _proj_b_call(h, w, w_up, layer, gq, gckv, gkr, gkn, cos, sin):
    b, n, d = h.shape
    tm = _proj_tm(n)
    row = lambda bi, i: (bi, i, 0)
    head = lambda bi, i: (bi, 0, i, 0)
    const = lambda bi, i: (0, 0)
    return pl.pallas_call(
        _proj_b_kernel,
        grid=(b, n // tm),
        in_specs=[pl.BlockSpec((1, tm, d), row),
                  _layer_weight(w, layer),
                  _layer_weight(w_up, layer),
                  pl.BlockSpec((1, MLA_QK_PAD), const),
                  pl.BlockSpec((1, MLA_KV_RANK), const),
                  pl.BlockSpec((1, LANES), const),
                  pl.BlockSpec((1, LANES), const),
                  pl.BlockSpec((tm, LANES), lambda bi, i: (i, 0)),
                  pl.BlockSpec((tm, LANES), lambda bi, i: (i, 0))],
        out_specs=[pl.BlockSpec((1, MLA_HEADS, tm, MLA_QK_PAD), head),
                   pl.BlockSpec((1, MLA_HEADS, tm, MLA_QK_PAD), head),
                   pl.BlockSpec((1, MLA_HEADS, tm, LANES), head)],
        out_shape=[jax.ShapeDtypeStruct((b, MLA_HEADS, n, MLA_QK_PAD), BF16),
                   jax.ShapeDtypeStruct((b, MLA_HEADS, n, MLA_QK_PAD), BF16),
                   jax.ShapeDtypeStruct((b, MLA_HEADS, n, LANES), BF16)],
        compiler_params=_params(("parallel", "parallel")),
        name="proj_mla",
    )(h, w, w_up, gq, gckv, gkr, gkn, cos, sin)


def _gate_kernel(h_ref, w_ref, b_ref, o_ref, *, n_sigmoid_tiles):
    is_sigmoid = pl.program_id(1) < n_sigmoid_tiles
    h = h_ref[0]
    for c in range(o_ref.shape[-1] // MXU_N):
        cols = slice(c * MXU_N, (c + 1) * MXU_N)
        y = _dot_wt(h, w_ref[cols, :])
        t = _sigmoid(y + b_ref[:, cols])
        o_ref[0, :, cols] = jnp.where(is_sigmoid, t, y * t).astype(BF16)


def _gate_call(h, w, layer, bias):
    b, n, d = h.shape
    cols = w.shape[1]
    tm = n // 4
    tn = 3072
    assert (cols - 3 * BRANCH_WIDTH) % tn == 0 and cols % tn == 0
    return pl.pallas_call(
        functools.partial(_gate_kernel, n_sigmoid_tiles=(cols - 3 * BRANCH_WIDTH) // tn),
        grid=(b, cols // tn, n // tm),
        in_specs=[pl.BlockSpec((1, tm, d), lambda bi, j, i: (bi, i, 0)),
                  pl.BlockSpec((None, tn, d), lambda bi, j, i: (layer, j, 0)),
                  pl.BlockSpec((1, tn), lambda bi, j, i: (0, j))],
        out_specs=pl.BlockSpec((1, tm, tn), lambda bi, j, i: (bi, i, j)),
        out_shape=jax.ShapeDtypeStruct((b, n, cols), BF16),
        compiler_params=_params(("parallel", "parallel", "parallel")),
        name="gate_proj",
    )(h, w, bias)


def _flash_kernel(*refs, group, tq, chunks, diff, lam_init):
    if diff:
        bound_ref, q_ref, k_ref, v_ref, vt_ref, lamv_ref, sub_ref, o_ref = refs
    else:
        bound_ref, q_ref, k_ref, v_ref, vt_ref, o_ref = refs
    dq = q_ref.shape[-1]
    nt = (((1,), (1,)), ((), ()))

    bound = bound_ref[0]
    safe = bound <= SAFE_BOUND_LOG2
    n_heads = q_ref.shape[1]
    out_w = o_ref.shape[-1] // n_heads
    half = group * tq // 2
    halves = (slice(0, half), slice(half, 2 * half))

    def load_q(hh):
        return q_ref[0, hh].reshape(group * tq, dq)

    def finish(hh, out):
        if diff:
            lv = lamv_ref[0]
            lam = (jnp.exp(jnp.sum(lv[0:1] * lv[1:2], axis=-1, keepdims=True))
                   - jnp.exp(jnp.sum(lv[2:3] * lv[3:4], axis=-1, keepdims=True)) + lam_init)
            o = out[:tq] - lam * out[tq:]
            o_ref[0, :, hh * out_w:(hh + 1) * out_w] = (
                _rms(o, sub_ref[...], LANES) * (1.0 - lam_init)).astype(BF16)
        else:
            for g in range(group):
                col = hh * out_w + g * LANES
                o_ref[0, :, col:col + LANES] = out[g * tq:(g + 1) * tq].astype(BF16)

    @pl.when(safe)
    def _():
        for hh in range(n_heads):
            q = load_q(hh)
            accs = [None, None]

            def k_dot_q(start, size):
                k = k_ref[0, hh, start:start + size, :]
                return [lax.dot_general(k, q[rows], nt, preferred_element_type=F32)
                        for rows in halves]

            def vt_dot_pt(pts, start, size):
                vt = vt_ref[0, hh, :, start:start + size]
                for i, pt in enumerate(pts):
                    t = jnp.dot(vt, pt, preferred_element_type=F32)
                    accs[i] = t if accs[i] is None else accs[i] + t

            pending = None
            for start, size in chunks:
                ss = k_dot_q(start, size)
                if pending is not None:
                    vt_dot_pt(*pending)
                pending = ([jnp.exp2(s - bound).astype(BF16) for s in ss], start, size)
            vt_dot_pt(*pending)
            acc = jnp.concatenate(accs, axis=1)
            finish(hh, (acc[:LANES] / acc[LANES:LANES + 1]).T)

    @pl.when(jnp.logical_not(safe))
    def _():
        for hh in range(n_heads):
            q = load_q(hh)
            m = l = acc = None
            for start, size in chunks:
                k = k_ref[0, hh, start:start + size, :]
                v = v_ref[0, hh, start:start + size, :]
                s = lax.dot_general(q, k, nt, preferred_element_type=F32)
                m_cur = jnp.max(s, axis=-1, keepdims=True)
                if m is None:
                    m = m_cur
                    p = jnp.exp2(s - m)
                    l = jnp.sum(p, axis=-1, keepdims=True)
                    acc = jnp.dot(p.astype(BF16), v, preferred_element_type=F32)
                else:
                    m_new = jnp.maximum(m, m_cur)
                    alpha = jnp.exp2(m - m_new)
                    p = jnp.exp2(s - m_new)
                    l = alpha * l + jnp.sum(p, axis=-1, keepdims=True)
                    acc = alpha * acc + jnp.dot(p.astype(BF16), v, preferred_element_type=F32)
                    m = m_new
            finish(hh, acc / l)


VT_ROWS = LANES + 16


def _vt_kernel(v_ref, o_ref):
    n = v_ref.shape[2]
    for hd in range(v_ref.shape[1]):
        for j in range(n // LANES):
            cols = slice(j * LANES, (j + 1) * LANES)
            o_ref[0, hd, 0:LANES, cols] = v_ref[0, hd, cols, :].T
        o_ref[0, hd, LANES:VT_ROWS, :] = jnp.ones((VT_ROWS - LANES, n), BF16)


def _vt_call(v, name):
    b, heads, n, dv = v.shape
    hb = min(heads, 4)
    return pl.pallas_call(
        _vt_kernel,
        grid=(b, heads // hb),
        in_specs=[pl.BlockSpec((1, hb, n, dv), lambda bi, h: (bi, h, 0, 0))],
        out_specs=pl.BlockSpec((1, hb, VT_ROWS, n), lambda bi, h: (bi, h, 0, 0)),
        out_shape=jax.ShapeDtypeStruct((b, heads, VT_ROWS, n), BF16),
        compiler_params=_params(("parallel", "parallel")),
        name=name,
    )(v)


def _flash_call(bound, q, k, v, vt, *, tq, q_tile0, n_q_tiles, kv_tile0, chunks, heads_per_step=1,
                diff=False, lamv=None, sub=None, lam_init=0.0, name="flash"):
    b, heads, group, n, dq = q.shape
    kv_len = sum(size for _, size in chunks)
    hb = heads_per_step
    out_w = hb * (LANES if diff else group * LANES)
    in_specs = [pl.BlockSpec(memory_space=pltpu.SMEM),
                pl.BlockSpec((1, hb, group, tq, dq), lambda bi, h, i: (bi, h, 0, q_tile0 + i, 0)),
                pl.BlockSpec((1, hb, kv_len, dq), lambda bi, h, i: (bi, h, kv_tile0, 0)),
                pl.BlockSpec((1, hb, kv_len, LANES), lambda bi, h, i: (bi, h, kv_tile0, 0)),
                pl.BlockSpec((1, hb, VT_ROWS, kv_len), lambda bi, h, i: (bi, h, 0, kv_tile0))]
    args = [bound, q, k, v, vt]
    if diff:
        in_specs += [pl.BlockSpec((1, 4, DIFF_QK), lambda bi, h, i: (0, 0, 0)),
                     pl.BlockSpec((1, LANES), lambda bi, h, i: (0, 0))]
        args += [lamv, sub]
    kern = functools.partial(_flash_kernel, group=group, tq=tq, chunks=chunks, diff=diff,
                             lam_init=lam_init)
    return pl.pallas_call(
        kern,
        grid=(b, heads // hb, n_q_tiles),
        in_specs=in_specs,
        out_specs=pl.BlockSpec((1, tq, out_w), lambda bi, h, i: (bi, i, h)),
        out_shape=jax.ShapeDtypeStruct((b, n_q_tiles * tq, BRANCH_WIDTH), BF16),
        compiler_params=_params(("parallel", "parallel", "parallel")),
        name=name,
    )(*args)


def _score_bound(q_terms, k_terms, scale):
    q2 = sum(d * jnp.max(jnp.square(g.astype(F32))) for g, d in q_terms)
    k2 = sum(d * jnp.max(jnp.square(g.astype(F32))) for g, d in k_terms)
    return (1.01 * scale * LOG2E * jnp.sqrt(q2 * k2)).reshape(1)


N_CHUNK = 512


def _merge_kernel(oa_ref, ob_ref, oc_ref, g_ref, ma_ref, mb_ref, mc_ref, wa_ref, wb_ref, wc_ref,
                  y_ref):
    ogs = [o_ref[0] * g_ref[0, :, r * BRANCH_WIDTH:(r + 1) * BRANCH_WIDTH]
           for r, o_ref in enumerate((oa_ref, ob_ref, oc_ref))]
    w_refs = (wa_ref, wb_ref, wc_ref)
    for c in range(y_ref.shape[-1] // N_CHUNK):
        cols = slice(c * N_CHUNK, (c + 1) * N_CHUNK)
        acc = None
        for r, m_ref in enumerate((ma_ref, mb_ref, mc_ref)):
            z = jnp.dot(ogs[r], w_refs[r][:, cols], preferred_element_type=F32)
            t = m_ref[0, :, cols].astype(F32) * z
            acc = t if acc is None else acc + t
        y_ref[0, :, cols] = acc.astype(BF16)


def _merge_call(oa, ob, oc, gm, w_brs, layer, *, tm, gm_tile0, name):
    b, rows, _ = oa.shape
    d = w_brs[0].shape[-1]
    gate_w = 3 * BRANCH_WIDTH
    assert (3 * d) % gate_w == 0
    o_spec = pl.BlockSpec((1, tm, BRANCH_WIDTH), lambda bi, i: (bi, i, 0))
    m_specs = [pl.BlockSpec((1, tm, d), lambda bi, i, r=r: (bi, gm_tile0 + i, r)) for r in range(3)]
    return pl.pallas_call(
        _merge_kernel,
        grid=(b, rows // tm),
        in_specs=[o_spec, o_spec, o_spec,
                  pl.BlockSpec((1, tm, gate_w), lambda bi, i: (bi, gm_tile0 + i, 3 * d // gate_w)),
                  *m_specs,
                  *[_layer_weight(w, layer) for w in w_brs]],
        out_specs=pl.BlockSpec((1, tm, d), lambda bi, i: (bi, i, 0)),
        out_shape=jax.ShapeDtypeStruct((b, rows, d), BF16),
        compiler_params=_params(("parallel", "parallel")),
        name=name,
    )(oa, ob, oc, gm, gm, gm, gm, *w_brs)


def _out_kernel(y_ref, w_ref, x_ref, mod_ref, o_ref):
    y = y_ref[0]
    for c in range(o_ref.shape[-1] // N_CHUNK):
        cols = slice(c * N_CHUNK, (c + 1) * N_CHUNK)
        out = jnp.dot(y, w_ref[:, cols], preferred_element_type=F32)
        o_ref[0, :, cols] = x_ref[0, :, cols] + mod_ref[0, 2:3, cols] * out


def _out_call(y, w_out, layer, xs, mod, *, tm, ctx_stream, name):
    b, rows, d = xs.shape
    return pl.pallas_call(
        _out_kernel,
        grid=(b, rows // tm),
        in_specs=[pl.BlockSpec((1, tm, d), lambda bi, i: (bi, i, 0)),
                  _layer_weight(w_out, layer),
                  pl.BlockSpec((1, tm, d), lambda bi, i: (bi, i, 0)),
                  pl.BlockSpec((1, 3, d), lambda bi, i: (b if ctx_stream else bi, 0, 0))],
        out_specs=pl.BlockSpec((1, tm, d), lambda bi, i: (bi, i, 0)),
        out_shape=jax.ShapeDtypeStruct((b, rows, d), F32),
        compiler_params=_params(("parallel", "parallel")),
        name=name,
    )(y, w_out, xs, mod)


def _rope_tables(n_lat, n_ctx, rot_dim, src):
    axis_dim = rot_dim // 2
    t = jnp.arange(n_lat, dtype=jnp.int32)
    pos_row = (t // GRID_W).astype(F32)
    pos_col = (t % GRID_W).astype(F32)
    inv_freq = ROPE_THETA ** (-jnp.arange(0, axis_dim, 2, dtype=F32) / axis_dim)
    ang_r = pos_row[:, None] * inv_freq
    ang_c = pos_col[:, None] * inv_freq
    ang = jnp.concatenate([ang_r, ang_r, ang_c, ang_c], axis=-1)
    cos, sin = jnp.cos(ang), jnp.sin(ang)
    lane = jnp.arange(rot_dim)
    sign = jnp.where((lane % axis_dim) < axis_dim // 2, -1.0, 1.0).astype(F32)
    pad = jnp.asarray(src < 0)
    lanes = jnp.asarray(np.maximum(src, 0) % rot_dim)
    cos = jnp.where(pad, 1.0, jnp.take(cos, lanes, axis=1))
    sin = jnp.where(pad, 0.0, jnp.take(sin * sign, lanes, axis=1))
    cos = jnp.concatenate([cos, jnp.ones((n_ctx, LANES), F32)], axis=0)
    sin = jnp.concatenate([sin, jnp.zeros((n_ctx, LANES), F32)], axis=0)
    return cos, sin


PREP_MAX_ROWS = 256
_IDENT = np.arange(LANES)


def _prep_runs():
    o_gk = GQA_HEADS * HEAD_DIM
    o_gv = o_gk + GQA_KV_HEADS * HEAD_DIM
    o_mq = o_gv + GQA_KV_HEADS * HEAD_DIM
    o_ckv = o_mq + MLA_HEADS * (MLA_NOPE + MLA_ROPE)
    o_kr = o_ckv + MLA_KV_RANK
    o_dq = o_kr + MLA_ROPE
    o_dv = o_dq + 2 * DIFF_HEADS * 2 * DIFF_QK
    o_gate = o_dv + DIFF_HEADS * LANES
    o_merge = o_gate + 3 * BRANCH_WIDTH
    units = []
    for u in range(o_gv // LANES):
        units.append((0, u * LANES, u * LANES, _lane_src_a()))
    for u in range((o_mq - o_gv) // LANES):
        units.append((0, o_gv + u * LANES, o_gv + u * LANES, _IDENT))
    for hd in range(MLA_HEADS):
        base = o_mq + hd * (MLA_NOPE + MLA_ROPE)
        units.append((1, hd * MLA_QK_PAD, base, _IDENT))
        units.append((1, hd * MLA_QK_PAD + LANES, base + MLA_NOPE, _lane_src_b()))
    for u in range(MLA_KV_RANK // LANES):
        units.append((1, MLA_HEADS * MLA_QK_PAD + u * LANES, o_ckv + u * LANES, _IDENT))
    units.append((1, MLA_HEADS * MLA_QK_PAD + MLA_KV_RANK, o_kr, _lane_src_b()))
    for u in range((o_dv - o_dq) // LANES):
        units.append((2, u * LANES, o_dq + u * LANES, _lane_src_c()))
    for u in range((o_gate - o_dv) // LANES):
        units.append((2, o_dv - o_dq + u * LANES, o_dv + u * LANES, _IDENT))
    for u in range(3 * D_MODEL // LANES):
        units.append((3, u * LANES, o_merge + u * LANES, _IDENT))
    for u in range(3 * BRANCH_WIDTH // LANES):
        units.append((3, 3 * D_MODEL + u * LANES, o_gate + u * LANES, _IDENT))

    runs = []
    for out, dst0, base, lane_src in units:
        for t, idx in enumerate(lane_src):
            src = None if idx < 0 else base + int(idx)
            last = runs[-1] if runs else None
            if (last is not None and last[0] == out and last[1] + last[3] == dst0 + t
                    and last[3] < PREP_MAX_ROWS
                    and ((src is None and last[2] is None)
                         or (src is not None and last[2] is not None and last[2] + last[3] == src))):
                runs[-1] = (out, last[1], last[2], last[3] + 1)
            else:
                runs.append((out, dst0 + t, src, 1))
    assert all(dst % 16 == 0 and rows % 16 == 0 and (src is None or src % 16 == 0)
               for _, dst, src, rows in runs)
    return runs


def _prep_kernel(w_ref, *o_refs, runs):
    for out, dst, src, rows in runs:
        if src is None:
            o_refs[out][0, dst:dst + rows, :] = jnp.zeros((rows, o_refs[out].shape[-1]), BF16)
        else:
            o_refs[out][0, dst:dst + rows, :] = w_ref[0, src:src + rows, :].astype(BF16)


def _prep_call(w_in_t):
    depth, total, d = w_in_t.shape
    runs = _prep_runs()
    heights = [max(dst + rows for out, dst, _, rows in runs if out == o) for o in range(4)]
    td = 256
    return pl.pallas_call(
        functools.partial(_prep_kernel, runs=runs),
        grid=(depth, d // td),
        in_specs=[pl.BlockSpec((1, total, td), lambda l, i: (l, 0, i))],
        out_specs=[pl.BlockSpec((1, ht, td), lambda l, i: (l, 0, i)) for ht in heights],
        out_shape=[jax.ShapeDtypeStruct((depth, ht, d), BF16) for ht in heights],
        compiler_params=_params(("parallel", "parallel")),
        name="weight_layout",
    )(w_in_t)


def kernel(x, c, ctx, c_ctx, norm_w, w_ada, b_ada, w_in, b_merge, gqa_q_norm, gqa_k_norm, mla_q_nope_norm, mla_q_rope_norm, mla_kv_norm, mla_w_uk, mla_w_uv, mla_k_nope_norm, mla_k_rope_norm, diff_q_norm, diff_k_norm, diff_lambda_q1, diff_lambda_k1, diff_lambda_q2, diff_lambda_k2, diff_subln, w_br_gqa, w_br_mla, w_br_diff, w_out):
    b, n_lat, d = x.shape
    n_ctx = ctx.shape[1]
    n = n_lat + n_ctx
    depth = w_in.shape[0]
    assert n_lat % 2048 == 0 and n_ctx == 256 and n % 8 == 0 and b < 8 and d == D_MODEL

    c_all = jnp.concatenate([c, c_ctx[None], jnp.zeros((8 - b - 1, d), F32)], axis=0)
    mod = _ada_call(c_all, w_ada, b_ada).reshape(depth, 8, 3, d)

    w_a, w_b, w_c, w_g = _prep_call(jnp.swapaxes(w_in, 1, 2))
    w_ukv = jnp.concatenate([mla_w_uk, mla_w_uv], axis=-1).astype(BF16)
    w_brs = [w.astype(BF16) for w in (w_br_gqa, w_br_mla, w_br_diff)]
    w_o = w_out.astype(BF16)
    bias_g = jnp.concatenate([b_merge, jnp.zeros((depth, 3 * BRANCH_WIDTH), F32)], axis=-1)
    src_a, src_b, src_c = _lane_src_a(), _lane_src_b(), _lane_src_c()
    g_aq = _permute_lanes(gqa_q_norm, src_a)
    g_ak = _permute_lanes(gqa_k_norm, src_a)
    g_bq = jnp.concatenate([mla_q_nope_norm, _permute_lanes(mla_q_rope_norm, src_b)], axis=-1)
    g_kr = _permute_lanes(mla_k_rope_norm, src_b)
    g_dq = _permute_lanes(jnp.tile(diff_q_norm, (1, 2)), src_c)
    g_dk = _permute_lanes(jnp.tile(diff_k_norm, (1, 2)), src_c)
    lamv = jnp.stack([diff_lambda_q1, diff_lambda_k1, diff_lambda_q2, diff_lambda_k2], axis=1)

    cos_a, sin_a = _rope_tables(n_lat, n_ctx, HEAD_DIM, src_a)
    cos_b, sin_b = _rope_tables(n_lat, n_ctx, MLA_ROPE, src_b)
    cos_c, sin_c = _rope_tables(n_lat, n_ctx, DIFF_QK, src_c)

    lat_chunks = tuple((s, 1024) for s in range(0, n_lat, 1024)) + ((n_lat, n_ctx),)
    assert sum(size for _, size in lat_chunks) == n
    ctx_chunks = ((0, n_ctx),)
    ctx_tile = n_lat // n_ctx

    for l in range(depth):
        last = l == depth - 1
        lam_init = 0.8 - 0.6 * math.exp(-0.3 * l)
        h = _h_call(x, ctx, norm_w[l][None], mod[l])

        qa, ka, va, qc, kc, vc = _proj_ac_call(
            h, w_a, w_c, l, g_aq[l][None], g_ak[l][None], g_dq[l][None], g_dk[l][None],
            cos_a, sin_a, cos_c, sin_c)
        qb, kb, vb = _proj_b_call(h, w_b, w_ukv, l, g_bq[l][None], mla_kv_norm[l][None], g_kr[l][None],
                                  mla_k_nope_norm[l][None], cos_b, sin_b)
        gm = _gate_call(h, w_g, l, bias_g[l][None])

        qa = qa.reshape(b, GQA_KV_HEADS, GQA_GROUP, n, HEAD_DIM)
        qb = qb.reshape(b, MLA_HEADS, 1, n, MLA_QK_PAD)
        diff_kw = dict(diff=True, lamv=lamv[l][None], sub=diff_subln[l][None], lam_init=lam_init)

        ba = _score_bound([(gqa_q_norm[l], HEAD_DIM)], [(gqa_k_norm[l], HEAD_DIM)], HEAD_DIM ** -0.5)
        bb = _score_bound([(mla_q_nope_norm[l], MLA_NOPE), (mla_q_rope_norm[l], MLA_ROPE)],
                          [(mla_k_nope_norm[l], MLA_NOPE), (mla_k_rope_norm[l], MLA_ROPE)],
                          (MLA_NOPE + MLA_ROPE) ** -0.5)
        bc = _score_bound([(diff_q_norm[l], DIFF_QK)], [(diff_k_norm[l], DIFF_QK)], DIFF_QK ** -0.5)
        vta = _vt_call(va, "vt_gqa")
        vtb = _vt_call(vb, "vt_mla")
        vtc = _vt_call(vc, "vt_diff")
        if not last:
            ctx_kw = dict(tq=n_ctx, q_tile0=ctx_tile, n_q_tiles=1, kv_tile0=ctx_tile, chunks=ctx_chunks)
            oa = _flash_call(ba, qa, ka, va, vta, heads_per_step=GQA_KV_HEADS, name="flash_gqa_ctx",
                             **ctx_kw)
            ob = _flash_call(bb, qb, kb, vb, vtb, heads_per_step=MLA_HEADS, name="flash_mla_ctx",
                             **ctx_kw)
            oc = _flash_call(bc, qc, kc, vc, vtc, heads_per_step=DIFF_HEADS, name="flash_diff_ctx",
                             **ctx_kw, **diff_kw)
            y = _merge_call(oa, ob, oc, gm, w_brs, l, tm=n_ctx, gm_tile0=ctx_tile, name="branch_merge_ctx")
            ctx = _out_call(y, w_o, l, ctx, mod[l], tm=n_ctx, ctx_stream=True, name="out_proj_ctx")

        lat_kw = dict(q_tile0=0, kv_tile0=0, chunks=lat_chunks)
        oa = _flash_call(ba, qa, ka, va, vta, tq=512, n_q_tiles=n_lat // 512, name="flash_gqa", **lat_kw)
        ob = _flash_call(bb, qb, kb, vb, vtb, tq=2048, n_q_tiles=n_lat // 2048, name="flash_mla", **lat_kw)
        oc = _flash_call(bc, qc, kc, vc, vtc, tq=1024, n_q_tiles=n_lat // 1024, name="flash_diff",
                         **lat_kw, **diff_kw)
        y = _merge_call(oa, ob, oc, gm, w_brs, l, tm=512, gm_tile0=0, name="branch_merge")
        x = _out_call(y, w_o, l, x, mod[l], tm=512, ctx_stream=False, name="out_proj")
    return x
```

```python
import functools
import math

import jax
import jax.numpy as jnp
import numpy as np
from jax import lax
from jax.experimental import pallas as pl
from jax.experimental.pallas import tpu as pltpu

F32 = jnp.float32
BF16 = jnp.bfloat16

GRID_W = 64
ROPE_THETA = 10000.0
EPS = 1e-6

D_MODEL = 2048
HEAD_DIM = 128
BRANCH_WIDTH = 1024
GQA_HEADS = 8
GQA_KV_HEADS = 2
GQA_GROUP = GQA_HEADS // GQA_KV_HEADS
MLA_HEADS = 8
MLA_NOPE = 128
MLA_ROPE = 64
MLA_QK_PAD = 256
MLA_KV_RANK = 512
DIFF_HEADS = 8
DIFF_QK = 64

LANES = 128
MXU_N = 256
VMEM_LIMIT = 56 * 1024 * 1024
LOG2E = 1.4426950408889634
SAFE_BOUND_LOG2 = 57.0


def _params(sem, vmem=VMEM_LIMIT):
    return pltpu.CompilerParams(dimension_semantics=sem, vmem_limit_bytes=vmem)


def _resident(shape, index_map):
    return pl.BlockSpec(shape, index_map, pipeline_mode=pl.Buffered(1))


def _rms(y, gain, n):
    ms = jnp.sum(y * y, axis=-1, keepdims=True) * (1.0 / n)
    return y * lax.rsqrt(ms + EPS) * gain


def _seg_rms(y, gain, lo):
    ss = y * y
    s_lo = jnp.sum(jnp.where(lo, ss, 0.0), axis=-1, keepdims=True)
    s_hi = jnp.sum(jnp.where(lo, 0.0, ss), axis=-1, keepdims=True)
    ms = jnp.where(lo, s_lo, s_hi) * (1.0 / DIFF_QK)
    return y * lax.rsqrt(ms + EPS) * gain


def _rope(y, cos, sin_signed):
    return y * cos + pltpu.roll(y, LANES // 2, 1) * sin_signed


def _lane_src_a():
    e = np.arange(HEAD_DIM // 4)
    return np.concatenate([e, 64 + e, 32 + e, 96 + e])


def _lane_src_b():
    e = np.arange(MLA_ROPE // 4)
    pad = -np.ones(32, np.int64)
    return np.concatenate([e, 32 + e, pad, 16 + e, 48 + e, pad])


def _lane_src_c():
    e = np.arange(DIFF_QK // 4)
    x1 = [m * DIFF_QK + a * 32 + e for m in (0, 1) for a in (0, 1)]
    x2 = [m * DIFF_QK + a * 32 + 16 + e for m in (0, 1) for a in (0, 1)]
    return np.concatenate(x1 + x2)


def _permute_lanes(vec, src):
    out = jnp.take(vec, jnp.asarray(np.maximum(src, 0)), axis=-1)
    return jnp.where(jnp.asarray(src >= 0), out, 0.0)


def _sigmoid(z):
    return 1.0 / (1.0 + jnp.exp(-z))


def _dot_wt(x, wt):
    return lax.dot_general(x, wt, (((1,), (1,)), ((), ())), preferred_element_type=F32)


def _ada_kernel(c_ref, w_ref, b_ref, o_ref):
    c = c_ref[...]
    a = (c * _sigmoid(c)).astype(BF16)
    o_ref[0] = jnp.dot(a, w_ref[0].astype(BF16), preferred_element_type=F32) + b_ref[0]


def _ada_call(c_all, w_ada, b_ada):
    depth, d, n3 = w_ada.shape
    tn = 512
    return pl.pallas_call(
        _ada_kernel,
        grid=(depth, n3 // tn),
        in_specs=[pl.BlockSpec((8, d), lambda l, j: (0, 0)),
                  pl.BlockSpec((1, d, tn), lambda l, j: (l, 0, j)),
                  pl.BlockSpec((1, 1, tn), lambda l, j: (l, 0, j))],
        out_specs=pl.BlockSpec((1, 8, tn), lambda l, j: (l, 0, j)),
        out_shape=jax.ShapeDtypeStruct((depth, 8, n3), F32),
        compiler_params=_params(("parallel", "parallel")),
        name="ada_mod",
    )(c_all, w_ada, b_ada.reshape(depth, 1, n3))


def _h_kernel(x_ref, ctx_ref, nw_ref, mod_ref, modc_ref, h_ref, *, lat_tiles):
    def norm_mod(src_ref, m_ref, per_batch):
        for j in range(src_ref.shape[0]):
            x = src_ref[j]
            mj = j if per_batch else 0
            ms = jnp.mean(x * x, axis=-1, keepdims=True)
            y = x * lax.rsqrt(ms + EPS) * nw_ref[...]
            h_ref[j] = (y * (1.0 + m_ref[mj, 1:2, :]) + m_ref[mj, 0:1, :]).astype(BF16)

    t = pl.program_id(1)

    @pl.when(t < lat_tiles)
    def _():
        norm_mod(x_ref, mod_ref, True)

    @pl.when(t >= lat_tiles)
    def _():
        norm_mod(ctx_ref, modc_ref, False)


def _h_call(x, ctx, norm_w, mod):
    b, n_lat, d = x.shape
    tr = ctx.shape[1]
    lat_tiles = n_lat // tr
    nb = 2 if b % 2 == 0 else 1
    return pl.pallas_call(
        functools.partial(_h_kernel, lat_tiles=lat_tiles),
        grid=(b // nb, lat_tiles + 1),
        in_specs=[pl.BlockSpec((nb, tr, d), lambda bi, t: (bi, jnp.minimum(t, lat_tiles - 1), 0)),
                  pl.BlockSpec((nb, tr, d), lambda bi, t: (bi, 0, 0)),
                  pl.BlockSpec((1, d), lambda bi, t: (0, 0)),
                  pl.BlockSpec((nb, 3, d), lambda bi, t: (bi, 0, 0)),
                  pl.BlockSpec((1, 3, d), lambda bi, t: (b, 0, 0))],
        out_specs=pl.BlockSpec((nb, tr, d), lambda bi, t: (bi, t, 0)),
        out_shape=jax.ShapeDtypeStruct((b, n_lat + tr, d), BF16),
        compiler_params=_params(("parallel", "parallel")),
        name="norm_mod",
    )(x, ctx, norm_w, mod, mod)


def _proj_a_kernel(h_ref, w_ref, gq_ref, gk_ref, cos_ref, sin_ref, q_ref, k_ref, v_ref, *, values):
    h = h_ref[0]
    cos = cos_ref[...]
    sin = sin_ref[...]
    scale = HEAD_DIM ** -0.5 * LOG2E
    n_q = GQA_HEADS * HEAD_DIM // MXU_N
    for c in (range(n_q + 1, n_q + 2) if values else range(n_q + 1)):
        y = _dot_wt(h, w_ref[c * MXU_N:(c + 1) * MXU_N, :])
        for u in range(2):
            yu = y[:, u * LANES:(u + 1) * LANES]
            if c < n_q:
                z = _rope(_rms(yu, gq_ref[...], HEAD_DIM), cos, sin)
                q_ref[0, 2 * c + u] = (z * scale).astype(BF16)
            elif c == n_q:
                z = _rope(_rms(yu, gk_ref[...], HEAD_DIM), cos, sin)
                k_ref[0, u] = z.astype(BF16)
            else:
                v_ref[0, u] = yu.astype(BF16)


def _proj_b_kernel(h_ref, w_ref, wup_ref, gq_ref, gckv_ref, gkr_ref, gkn_ref, cos_ref, sin_ref,
                   q_ref, k_ref, v_ref):
    h = h_ref[0]
    cos = cos_ref[...]
    sin = sin_ref[...]
    scale = (MLA_NOPE + MLA_ROPE) ** -0.5 * LOG2E
    base = MLA_HEADS * MLA_QK_PAD
    y0 = _dot_wt(h, w_ref[base:base + MXU_N, :])
    y1 = _dot_wt(h, w_ref[base + MXU_N:base + 2 * MXU_N, :])
    ms = (jnp.sum(y0 * y0, axis=-1, keepdims=True)
          + jnp.sum(y1 * y1, axis=-1, keepdims=True)) * (1.0 / MLA_KV_RANK)
    inv = lax.rsqrt(ms + EPS)
    ckv = jnp.concatenate([(y0 * inv * gckv_ref[:, 0:MXU_N]).astype(BF16),
                           (y1 * inv * gckv_ref[:, MXU_N:2 * MXU_N]).astype(BF16)], axis=1)
    base += MLA_KV_RANK
    yk = _dot_wt(h, w_ref[base:base + LANES, :])
    kr = _rope(_rms(yk, gkr_ref[...], MLA_ROPE), cos, sin).astype(BF16)
    for hd in range(MLA_HEADS):
        y = _dot_wt(h, w_ref[hd * MLA_QK_PAD:(hd + 1) * MLA_QK_PAD, :])
        nope = _rms(y[:, :LANES], gq_ref[:, :LANES], MLA_NOPE)
        rope = _rope(_rms(y[:, LANES:], gq_ref[:, LANES:], MLA_ROPE), cos, sin)
        q_ref[0, hd, :, 0:LANES] = (nope * scale).astype(BF16)
        q_ref[0, hd, :, LANES:2 * LANES] = (rope * scale).astype(BF16)
    per = MLA_HEADS // 2
    for c in range(2 * per):
        y = jnp.dot(ckv, wup_ref[:, c * MXU_N:(c + 1) * MXU_N], preferred_element_type=F32)
        for u in range(2):
            yu = y[:, u * LANES:(u + 1) * LANES]
            if c < per:
                k_ref[0, 2 * c + u, :, 0:LANES] = _rms(yu, gkn_ref[...], MLA_NOPE).astype(BF16)
                k_ref[0, 2 * c + u, :, LANES:2 * LANES] = kr
            else:
                v_ref[0, 2 * (c - per) + u] = yu.astype(BF16)


def _proj_c_kernel(h_ref, w_ref, gq_ref, gk_ref, cos_ref, sin_ref, q_ref, k_ref, v_ref, *, values):
    h = h_ref[0]
    cos = cos_ref[...]
    sin = sin_ref[...]
    scale = DIFF_QK ** -0.5 * LOG2E
    lo = (lax.broadcasted_iota(jnp.int32, (1, LANES), 1) & 32) == 0
    per = DIFF_HEADS // 2
    for c in (range(2 * per, 3 * per) if values else range(2 * per)):
        y = _dot_wt(h, w_ref[c * MXU_N:(c + 1) * MXU_N, :])
        for u in range(2):
            yu = y[:, u * LANES:(u + 1) * LANES]
            if c < per:
                z = _rope(_seg_rms(yu, gq_ref[...], lo), cos, sin) * scale
                q_ref[0, 2 * c + u, 0] = jnp.where(lo, z, 0.0).astype(BF16)
                q_ref[0, 2 * c + u, 1] = jnp.where(lo, 0.0, z).astype(BF16)
            elif c < 2 * per:
                z = _rope(_seg_rms(yu, gk_ref[...], lo), cos, sin)
                k_ref[0, 2 * (c - per) + u] = z.astype(BF16)
            else:
                v_ref[0, 2 * (c - 2 * per) + u] = yu.astype(BF16)


def _proj_tm(n):
    return n // 8


def _layer_weight(w, layer):
    return _resident((None,) + w.shape[1:], lambda bi, i: (layer, 0, 0))


def _proj_ac_kernel(h_ref, wa_ref, wc_ref, gaq_ref, gak_ref, gcq_ref, gck_ref,
                    cosa_ref, sina_ref, cosc_ref, sinc_ref,
                    qa_ref, ka_ref, va_ref, qc_ref, kc_ref, vc_ref):
    c_refs = (h_ref, wc_ref, gcq_ref, gck_ref, cosc_ref, sinc_ref, qc_ref, kc_ref, vc_ref)
    a_refs = (h_ref, wa_ref, gaq_ref, gak_ref, cosa_ref, sina_ref, qa_ref, ka_ref, va_ref)
    _proj_c_kernel(*c_refs, values=False)
    _proj_a_kernel(*a_refs, values=False)
    _proj_c_kernel(*c_refs, values=True)
    _proj_a_kernel(*a_refs, values=True)


def _proj_ac_call(h, w_a, w_c, layer, gaq, gak, gcq, gck, cos_a, sin_a, cos_c, sin_c):
    b, n, d = h.shape
    tm = _proj_tm(n)
    row = lambda bi, i: (bi, i, 0)
    head = lambda bi, i: (bi, 0, i, 0)
    const = lambda bi, i: (0, 0)
    gain = pl.BlockSpec((1, LANES), const)
    table = pl.BlockSpec((tm, LANES), lambda bi, i: (i, 0))
    return pl.pallas_call(
        _proj_ac_kernel,
        grid=(b, n // tm),
        in_specs=[pl.BlockSpec((1, tm, d), row), _layer_weight(w_a, layer), _layer_weight(w_c, layer),
                  gain, gain, gain, gain, table, table, table, table],
        out_specs=[pl.BlockSpec((1, GQA_HEADS, tm, HEAD_DIM), head),
                   pl.BlockSpec((1, GQA_KV_HEADS, tm, HEAD_DIM), head),
                   pl.BlockSpec((1, GQA_KV_HEADS, tm, HEAD_DIM), head),
                   pl.BlockSpec((1, DIFF_HEADS, 2, tm, LANES), lambda bi, i: (bi, 0, 0, i, 0)),
                   pl.BlockSpec((1, DIFF_HEADS, tm, LANES), head),
                   pl.BlockSpec((1, DIFF_HEADS, tm, LANES), head)],
        out_shape=[jax.ShapeDtypeStruct((b, GQA_HEADS, n, HEAD_DIM), BF16),
                   jax.ShapeDtypeStruct((b, GQA_KV_HEADS, n, HEAD_DIM), BF16),
                   jax.ShapeDtypeStruct((b, GQA_KV_HEADS, n, HEAD_DIM), BF16),
                   jax.ShapeDtypeStruct((b, DIFF_HEADS, 2, n, LANES), BF16),
                   jax.ShapeDtypeStruct((b, DIFF_HEADS, n, LANES), BF16),
                   jax.ShapeDtypeStruct((b, DIFF_HEADS, n, LANES), BF16)],
        compiler_params=_params(("parallel", "parallel")),
        name="proj_gqa_diff",
    )(h, w_a, w_c, gaq, gak, gcq, gck, cos_a, sin_a, cos_c, sin_c)


def _proj_b_call(h, w, w_up, layer, gq, gckv, gkr, gkn, cos, sin):
    b, n, d = h.shape
    tm = _proj_tm(n)
    row = lambda bi, i: (bi, i, 0)
    head = lambda bi, i: (bi, 0, i, 0)
    const = lambda bi, i: (0, 0)
    return pl.pallas_call(
        _proj_b_kernel,
        grid=(b, n // tm),
        in_specs=[pl.BlockSpec((1, tm, d), row),
                  _layer_weight(w, layer),
                  _layer_weight(w_up, layer),
                  pl.BlockSpec((1, MLA_QK_PAD), const),
                  pl.BlockSpec((1, MLA_KV_RANK), const),
                  pl.BlockSpec((1, LANES), const),
                  pl.BlockSpec((1, LANES), const),
                  pl.BlockSpec((tm, LANES), lambda bi, i: (i, 0)),
                  pl.BlockSpec((tm, LANES), lambda bi, i: (i, 0))],
        out_specs=[pl.BlockSpec((1, MLA_HEADS, tm, MLA_QK_PAD), head),
                   pl.BlockSpec((1, MLA_HEADS, tm, MLA_QK_PAD), head),
                   pl.BlockSpec((1, MLA_HEADS, tm, LANES), head)],
        out_shape=[jax.ShapeDtypeStruct((b, MLA_HEADS, n, MLA_QK_PAD), BF16),
                   jax.ShapeDtypeStruct((b, MLA_HEADS, n, MLA_QK_PAD), BF16),
                   jax.ShapeDtypeStruct((b, MLA_HEADS, n, LANES), BF16)],
        compiler_params=_params(("parallel", "parallel")),
        name="proj_mla",
    )(h, w, w_up, gq, gckv, gkr, gkn, cos, sin)


def _gate_kernel(h_ref, w_ref, b_ref, o_ref, *, n_sigmoid_tiles):
    is_sigmoid = pl.program_id(1) < n_sigmoid_tiles

    def tile(merge_gate):
        h = h_ref[0]
        for c in range(o_ref.shape[-1] // MXU_N):
            cols = slice(c * MXU_N, (c + 1) * MXU_N)
            y = _dot_wt(h, w_ref[cols, :])
            out = _sigmoid(y + b_ref[:, cols]) if merge_gate else y * _sigmoid(y)
            o_ref[0, :, cols] = out.astype(BF16)

    @pl.when(is_sigmoid)
    def _():
        tile(True)

    @pl.when(jnp.logical_not(is_sigmoid))
    def _():
        tile(False)


def _gate_call(h, w, layer, bias):
    b, n, d = h.shape
    cols = w.shape[1]
    tm = n // 4
    tn = 3072
    assert (cols - 3 * BRANCH_WIDTH) % tn == 0 and cols % tn == 0
    return pl.pallas_call(
        functools.partial(_gate_kernel, n_sigmoid_tiles=(cols - 3 * BRANCH_WIDTH) // tn),
        grid=(b, cols // tn, n // tm),
        in_specs=[pl.BlockSpec((1, tm, d), lambda bi, j, i: (bi, i, 0)),
                  pl.BlockSpec((None, tn, d), lambda bi, j, i: (layer, j, 0)),
                  pl.BlockSpec((1, tn), lambda bi, j, i: (0, j))],
        out_specs=pl.BlockSpec((1, tm, tn), lambda bi, j, i: (bi, i, j)),
        out_shape=jax.ShapeDtypeStruct((b, n, cols), BF16),
        compiler_params=_params(("parallel", "parallel", "parallel")),
        name="gate_proj",
    )(h, w, bias)


def _flash_kernel(*refs, group, tq, chunks, diff, lam_init):
    if diff:
        bound_ref, q_ref, k_ref, v_ref, vt_ref, lamv_ref, sub_ref, o_ref = refs
    else:
        bound_ref, q_ref, k_ref, v_ref, vt_ref, o_ref = refs
    dq = q_ref.shape[-1]
    nt = (((1,), (1,)), ((), ()))

    bound = bound_ref[0]
    safe = bound <= SAFE_BOUND_LOG2
    n_heads = q_ref.shape[1]
    out_w = o_ref.shape[-1] // n_heads
    half = group * tq // 2
    halves = (slice(0, half), slice(half, 2 * half))

    def load_q(hh):
        return q_ref[0, hh].reshape(group * tq, dq)

    def finish(hh, out):
        if diff:
            lv = lamv_ref[0]
            lam = (jnp.exp(jnp.sum(lv[0:1] * lv[1:2], axis=-1, keepdims=True))
                   - jnp.exp(jnp.sum(lv[2:3] * lv[3:4], axis=-1, keepdims=True)) + lam_init)
            o = out[:tq] - lam * out[tq:]
            o_ref[0, :, hh * out_w:(hh + 1) * out_w] = (
                _rms(o, sub_ref[...], LANES) * (1.0 - lam_init)).astype(BF16)
        else:
            for g in range(group):
                col = hh * out_w + g * LANES
                o_ref[0, :, col:col + LANES] = out[g * tq:(g + 1) * tq].astype(BF16)

    @pl.when(safe)
    def _():
        for hh in range(n_heads):
            q = load_q(hh)
            accs = [None, None]

            def k_dot_q(start, size):
                k = k_ref[0, hh, start:start + size, :]
                return [lax.dot_general(k, q[rows], nt, preferred_element_type=F32)
                        for rows in halves]

            def vt_dot_pt(pts, start, size):
                vt = vt_ref[0, hh, :, start:start + size]
                for i, pt in enumerate(pts):
                    t = jnp.dot(vt, pt, preferred_element_type=F32)
                    accs[i] = t if accs[i] is None else accs[i] + t

            pending = None
            for start, size in chunks:
                ss = k_dot_q(start, size)
                if pending is not None:
                    vt_dot_pt(*pending)
                pending = ([jnp.exp2(s - bound).astype(BF16) for s in ss], start, size)
            vt_dot_pt(*pending)
            acc = jnp.concatenate(accs, axis=1)
            finish(hh, (acc[:LANES] / acc[LANES:LANES + 1]).T)

    @pl.when(jnp.logical_not(safe))
    def _():
        for hh in range(n_heads):
            q = load_q(hh)
            m = l = acc = None
            for start, size in chunks:
                k = k_ref[0, hh, start:start + size, :]
                v = v_ref[0, hh, start:start + size, :]
                s = lax.dot_general(q, k, nt, preferred_element_type=F32)
                m_cur = jnp.max(s, axis=-1, keepdims=True)
                if m is None:
                    m = m_cur
                    p = jnp.exp2(s - m)
                    l = jnp.sum(p, axis=-1, keepdims=True)
                    acc = jnp.dot(p.astype(BF16), v, preferred_element_type=F32)
                else:
                    m_new = jnp.maximum(m, m_cur)
                    alpha = jnp.exp2(m - m_new)
                    p = jnp.exp2(s - m_new)
                    l = alpha * l + jnp.sum(p, axis=-1, keepdims=True)
                    acc = alpha * acc + jnp.dot(p.astype(BF16), v, preferred_element_type=F32)
                    m = m_new
            finish(hh, acc / l)


VT_ROWS = LANES + 16


def _vt_kernel(v_ref, o_ref):
    n = v_ref.shape[2]
    for hd in range(v_ref.shape[1]):
        for j in range(n // LANES):
            cols = slice(j * LANES, (j + 1) * LANES)
            o_ref[0, hd, 0:LANES, cols] = v_ref[0, hd, cols, :].T
        o_ref[0, hd, LANES:VT_ROWS, :] = jnp.ones((VT_ROWS - LANES, n), BF16)


def _vt_call(v, name):
    b, heads, n, dv = v.shape
    hb = min(heads, 4)
    return pl.pallas_call(
        _vt_kernel,
        grid=(b, heads // hb),
        in_specs=[pl.BlockSpec((1, hb, n, dv), lambda bi, h: (bi, h, 0, 0))],
        out_specs=pl.BlockSpec((1, hb, VT_ROWS, n), lambda bi, h: (bi, h, 0, 0)),
        out_shape=jax.ShapeDtypeStruct((b, heads, VT_ROWS, n), BF16),
        compiler_params=_params(("parallel", "parallel")),
        name=name,
    )(v)


def _flash_call(bound, q, k, v, vt, *, tq, q_tile0, n_q_tiles, kv_tile0, chunks, heads_per_step=1,
                diff=False, lamv=None, sub=None, lam_init=0.0, name="flash"):
    b, heads, group, n, dq = q.shape
    kv_len = sum(size for _, size in chunks)
    hb = heads_per_step
    out_w = hb * (LANES if diff else group * LANES)
    in_specs = [pl.BlockSpec(memory_space=pltpu.SMEM),
                pl.BlockSpec((1, hb, group, tq, dq), lambda bi, h, i: (bi, h, 0, q_tile0 + i, 0)),
                pl.BlockSpec((1, hb, kv_len, dq), lambda bi, h, i: (bi, h, kv_tile0, 0)),
                pl.BlockSpec((1, hb, kv_len, LANES), lambda bi, h, i: (bi, h, kv_tile0, 0)),
                pl.BlockSpec((1, hb, VT_ROWS, kv_len), lambda bi, h, i: (bi, h, 0, kv_tile0))]
    args = [bound, q, k, v, vt]
    if diff:
        in_specs += [pl.BlockSpec((1, 4, DIFF_QK), lambda bi, h, i: (0, 0, 0)),
                     pl.BlockSpec((1, LANES), lambda bi, h, i: (0, 0))]
        args += [lamv, sub]
    kern = functools.partial(_flash_kernel, group=group, tq=tq, chunks=chunks, diff=diff,
                             lam_init=lam_init)
    return pl.pallas_call(
        kern,
        grid=(b, heads // hb, n_q_tiles),
        in_specs=in_specs,
        out_specs=pl.BlockSpec((1, tq, out_w), lambda bi, h, i: (bi, i, h)),
        out_shape=jax.ShapeDtypeStruct((b, n_q_tiles * tq, BRANCH_WIDTH), BF16),
        compiler_params=_params(("parallel", "parallel", "parallel")),
        name=name,
    )(*args)


def _score_bound(q_terms, k_terms, scale):
    q2 = sum(d * jnp.max(jnp.square(g.astype(F32))) for g, d in q_terms)
    k2 = sum(d * jnp.max(jnp.square(g.astype(F32))) for g, d in k_terms)
    return (1.01 * scale * LOG2E * jnp.sqrt(q2 * k2)).reshape(1)


N_CHUNK = 512


def _merge_kernel(oa_ref, ob_ref, oc_ref, g_ref, ma_ref, mb_ref, mc_ref, wa_ref, wb_ref, wc_ref,
                  y_ref):
    ogs = [o_ref[0] * g_ref[0, :, r * BRANCH_WIDTH:(r + 1) * BRANCH_WIDTH]
           for r, o_ref in enumerate((oa_ref, ob_ref, oc_ref))]
    w_refs = (wa_ref, wb_ref, wc_ref)
    for c in range(y_ref.shape[-1] // N_CHUNK):
        cols = slice(c * N_CHUNK, (c + 1) * N_CHUNK)
        acc = None
        for r, m_ref in enumerate((ma_ref, mb_ref, mc_ref)):
            z = jnp.dot(ogs[r], w_refs[r][:, cols], preferred_element_type=F32)
            t = m_ref[0, :, cols].astype(F32) * z
            acc = t if acc is None else acc + t
        y_ref[0, :, cols] = acc.astype(BF16)


def _merge_call(oa, ob, oc, gm, w_brs, layer, *, tm, gm_tile0, name):
    b, rows, _ = oa.shape
    d = w_brs[0].shape[-1]
    gate_w = 3 * BRANCH_WIDTH
    assert (3 * d) % gate_w == 0
    o_spec = pl.BlockSpec((1, tm, BRANCH_WIDTH), lambda bi, i: (bi, i, 0))
    m_specs = [pl.BlockSpec((1, tm, d), lambda bi, i, r=r: (bi, gm_tile0 + i, r)) for r in range(3)]
    return pl.pallas_call(
        _merge_kernel,
        grid=(b, rows // tm),
        in_specs=[o_spec, o_spec, o_spec,
                  pl.BlockSpec((1, tm, gate_w), lambda bi, i: (bi, gm_tile0 + i, 3 * d // gate_w)),
                  *m_specs,
                  *[_layer_weight(w, layer) for w in w_brs]],
        out_specs=pl.BlockSpec((1, tm, d), lambda bi, i: (bi, i, 0)),
        out_shape=jax.ShapeDtypeStruct((b, rows, d), BF16),
        compiler_params=_params(("parallel", "parallel")),
        name=name,
    )(oa, ob, oc, gm, gm, gm, gm, *w_brs)


def _out_kernel(y_ref, w_ref, x_ref, mod_ref, o_ref):
    y = y_ref[0]
    for c in range(o_ref.shape[-1] // N_CHUNK):
        cols = slice(c * N_CHUNK, (c + 1) * N_CHUNK)
        out = jnp.dot(y, w_ref[:, cols], preferred_element_type=F32)
        o_ref[0, :, cols] = x_ref[0, :, cols] + mod_ref[0, 2:3, cols] * out


def _out_call(y, w_out, layer, xs, mod, *, tm, ctx_stream, name):
    b, rows, d = xs.shape
    return pl.pallas_call(
        _out_kernel,
        grid=(b, rows // tm),
        in_specs=[pl.BlockSpec((1, tm, d), lambda bi, i: (bi, i, 0)),
                  _layer_weight(w_out, layer),
                  pl.BlockSpec((1, tm, d), lambda bi, i: (bi, i, 0)),
                  pl.BlockSpec((1, 3, d), lambda bi, i: (b if ctx_stream else bi, 0, 0))],
        out_specs=pl.BlockSpec((1, tm, d), lambda bi, i: (bi, i, 0)),
        out_shape=jax.ShapeDtypeStruct((b, rows, d), F32),
        compiler_params=_params(("parallel", "parallel")),
        name=name,
    )(y, w_out, xs, mod)


def _rope_tables(n_lat, n_ctx, rot_dim, src):
    axis_dim = rot_dim // 2
    t = jnp.arange(n_lat, dtype=jnp.int32)
    pos_row = (t // GRID_W).astype(F32)
    pos_col = (t % GRID_W).astype(F32)
    inv_freq = ROPE_THETA ** (-jnp.arange(0, axis_dim, 2, dtype=F32) / axis_dim)
    ang_r = pos_row[:, None] * inv_freq
    ang_c = pos_col[:, None] * inv_freq
    ang = jnp.concatenate([ang_r, ang_r, ang_c, ang_c], axis=-1)
    cos, sin = jnp.cos(ang), jnp.sin(ang)
    lane = jnp.arange(rot_dim)
    sign = jnp.where((lane % axis_dim) < axis_dim // 2, -1.0, 1.0).astype(F32)
    pad = jnp.asarray(src < 0)
    lanes = jnp.asarray(np.maximum(src, 0) % rot_dim)
    cos = jnp.where(pad, 1.0, jnp.take(cos, lanes, axis=1))
    sin = jnp.where(pad, 0.0, jnp.take(sin * sign, lanes, axis=1))
    cos = jnp.concatenate([cos, jnp.ones((n_ctx, LANES), F32)], axis=0)
    sin = jnp.concatenate([sin, jnp.zeros((n_ctx, LANES), F32)], axis=0)
    return cos, sin


PREP_MAX_ROWS = 256
_IDENT = np.arange(LANES)


def _prep_runs():
    o_gk = GQA_HEADS * HEAD_DIM
    o_gv = o_gk + GQA_KV_HEADS * HEAD_DIM
    o_mq = o_gv + GQA_KV_HEADS * HEAD_DIM
    o_ckv = o_mq + MLA_HEADS * (MLA_NOPE + MLA_ROPE)
    o_kr = o_ckv + MLA_KV_RANK
    o_dq = o_kr + MLA_ROPE
    o_dv = o_dq + 2 * DIFF_HEADS * 2 * DIFF_QK
    o_gate = o_dv + DIFF_HEADS * LANES
    o_merge = o_gate + 3 * BRANCH_WIDTH
    units = []
    for u in range(o_gv // LANES):
        units.append((0, u * LANES, u * LANES, _lane_src_a()))
    for u in range((o_mq - o_gv) // LANES):
        units.append((0, o_gv + u * LANES, o_gv + u * LANES, _IDENT))
    for hd in range(MLA_HEADS):
        base = o_mq + hd * (MLA_NOPE + MLA_ROPE)
        units.append((1, hd * MLA_QK_PAD, base, _IDENT))
        units.append((1, hd * MLA_QK_PAD + LANES, base + MLA_NOPE, _lane_src_b()))
    for u in range(MLA_KV_RANK // LANES):
        units.append((1, MLA_HEADS * MLA_QK_PAD + u * LANES, o_ckv + u * LANES, _IDENT))
    units.append((1, MLA_HEADS * MLA_QK_PAD + MLA_KV_RANK, o_kr, _lane_src_b()))
    for u in range((o_dv - o_dq) // LANES):
        units.append((2, u * LANES, o_dq + u * LANES, _lane_src_c()))
    for u in range((o_gate - o_dv) // LANES):
        units.append((2, o_dv - o_dq + u * LANES, o_dv + u * LANES, _IDENT))
    for u in range(3 * D_MODEL // LANES):
        units.append((3, u * LANES, o_merge + u * LANES, _IDENT))
    for u in range(3 * BRANCH_WIDTH // LANES):
        units.append((3, 3 * D_MODEL + u * LANES, o_gate + u * LANES, _IDENT))

    runs = []
    for out, dst0, base, lane_src in units:
        for t, idx in enumerate(lane_src):
            src = None if idx < 0 else base + int(idx)
            last = runs[-1] if runs else None
            if (last is not None and last[0] == out and last[1] + last[3] == dst0 + t
                    and last[3] < PREP_MAX_ROWS
                    and ((src is None and last[2] is None)
                         or (src is not None and last[2] is not None and last[2] + last[3] == src))):
                runs[-1] = (out, last[1], last[2], last[3] + 1)
            else:
                runs.append((out, dst0 + t, src, 1))
    assert all(dst % 16 == 0 and rows % 16 == 0 and (src is None or src % 16 == 0)
               for _, dst, src, rows in runs)
    return runs


def _prep_kernel(w_ref, *o_refs, runs):
    for out, dst, src, rows in runs:
        if src is None:
            o_refs[out][0, dst:dst + rows, :] = jnp.zeros((rows, o_refs[out].shape[-1]), BF16)
        else:
            o_refs[out][0, dst:dst + rows, :] = w_ref[0, src:src + rows, :].astype(BF16)


def _prep_call(w_in_t):
    depth, total, d = w_in_t.shape
    runs = _prep_runs()
    heights = [max(dst + rows for out, dst, _, rows in runs if out == o) for o in range(4)]
    td = 256
    return pl.pallas_call(
        functools.partial(_prep_kernel, runs=runs),
        grid=(depth, d // td),
        in_specs=[pl.BlockSpec((1, total, td), lambda l, i: (l, 0, i))],
        out_specs=[pl.BlockSpec((1, ht, td), lambda l, i: (l, 0, i)) for ht in heights],
        out_shape=[jax.ShapeDtypeStruct((depth, ht, d), BF16) for ht in heights],
        compiler_params=_params(("parallel", "parallel")),
        name="weight_layout",
    )(w_in_t)


def kernel(x, c, ctx, c_ctx, norm_w, w_ada, b_ada, w_in, b_merge, gqa_q_norm, gqa_k_norm, mla_q_nope_norm, mla_q_rope_norm, mla_kv_norm, mla_w_uk, mla_w_uv, mla_k_nope_norm, mla_k_rope_norm, diff_q_norm, diff_k_norm, diff_lambda_q1, diff_lambda_k1, diff_lambda_q2, diff_lambda_k2, diff_subln, w_br_gqa, w_br_mla, w_br_diff, w_out):
    b, n_lat, d = x.shape
    n_ctx = ctx.shape[1]
    n = n_lat + n_ctx
    depth = w_in.shape[0]
    assert n_lat % 2048 == 0 and n_ctx == 256 and n % 8 == 0 and b < 8 and d == D_MODEL

    c_all = jnp.concatenate([c, c_ctx[None], jnp.zeros((8 - b - 1, d), F32)], axis=0)
    mod = _ada_call(c_all, w_ada, b_ada).reshape(depth, 8, 3, d)

    w_a, w_b, w_c, w_g = _prep_call(jnp.swapaxes(w_in, 1, 2))
    w_ukv = jnp.concatenate([mla_w_uk, mla_w_uv], axis=-1).astype(BF16)
    w_brs = [w.astype(BF16) for w in (w_br_gqa, w_br_mla, w_br_diff)]
    w_o = w_out.astype(BF16)
    bias_g = jnp.concatenate([b_merge, jnp.zeros((depth, 3 * BRANCH_WIDTH), F32)], axis=-1)
    src_a, src_b, src_c = _lane_src_a(), _lane_src_b(), _lane_src_c()
    g_aq = _permute_lanes(gqa_q_norm, src_a)
    g_ak = _permute_lanes(gqa_k_norm, src_a)
    g_bq = jnp.concatenate([mla_q_nope_norm, _permute_lanes(mla_q_rope_norm, src_b)], axis=-1)
    g_kr = _permute_lanes(mla_k_rope_norm, src_b)
    g_dq = _permute_lanes(jnp.tile(diff_q_norm, (1, 2)), src_c)
    g_dk = _permute_lanes(jnp.tile(diff_k_norm, (1, 2)), src_c)
    lamv = jnp.stack([diff_lambda_q1, diff_lambda_k1, diff_lambda_q2, diff_lambda_k2], axis=1)

    cos_a, sin_a = _rope_tables(n_lat, n_ctx, HEAD_DIM, src_a)
    cos_b, sin_b = _rope_tables(n_lat, n_ctx, MLA_ROPE, src_b)
    cos_c, sin_c = _rope_tables(n_lat, n_ctx, DIFF_QK, src_c)

    lat_chunks = tuple((s, 1024) for s in range(0, n_lat, 1024)) + ((n_lat, n_ctx),)
    assert sum(size for _, size in lat_chunks) == n
    ctx_chunks = ((0, n_ctx),)
    ctx_tile = n_lat // n_ctx

    for l in range(depth):
        last = l == depth - 1
        lam_init = 0.8 - 0.6 * math.exp(-0.3 * l)
        h = _h_call(x, ctx, norm_w[l][None], mod[l])

        qa, ka, va, qc, kc, vc = _proj_ac_call(
            h, w_a, w_c, l, g_aq[l][None], g_ak[l][None], g_dq[l][None], g_dk[l][None],
            cos_a, sin_a, cos_c, sin_c)
        qb, kb, vb = _proj_b_call(h, w_b, w_ukv, l, g_bq[l][None], mla_kv_norm[l][None], g_kr[l][None],
                                  mla_k_nope_norm[l][None], cos_b, sin_b)
        gm = _gate_call(h, w_g, l, bias_g[l][None])

        qa = qa.reshape(b, GQA_KV_HEADS, GQA_GROUP, n, HEAD_DIM)
        qb = qb.reshape(b, MLA_HEADS, 1, n, MLA_QK_PAD)
        diff_kw = dict(diff=True, lamv=lamv[l][None], sub=diff_subln[l][None], lam_init=lam_init)

        ba = _score_bound([(gqa_q_norm[l], HEAD_DIM)], [(gqa_k_norm[l], HEAD_DIM)], HEAD_DIM ** -0.5)
        bb = _score_bound([(mla_q_nope_norm[l], MLA_NOPE), (mla_q_rope_norm[l], MLA_ROPE)],
                          [(mla_k_nope_norm[l], MLA_NOPE), (mla_k_rope_norm[l], MLA_ROPE)],
                          (MLA_NOPE + MLA_ROPE) ** -0.5)
        bc = _score_bound([(diff_q_norm[l], DIFF_QK)], [(diff_k_norm[l], DIFF_QK)], DIFF_QK ** -0.5)
        vta = _vt_call(va, "vt_gqa")
        vtb = _vt_call(vb, "vt_mla")
        vtc = _vt_call(vc, "vt_diff")
        if not last:
            ctx_kw = dict(tq=n_ctx, q_tile0=ctx_tile, n_q_tiles=1, kv_tile0=ctx_tile, chunks=ctx_chunks)
            oa = _flash_call(ba, qa, ka, va, vta, heads_per_step=GQA_KV_HEADS, name="flash_gqa_ctx",
                             **ctx_kw)
            ob = _flash_call(bb, qb, kb, vb, vtb, heads_per_step=MLA_HEADS, name="flash_mla_ctx",
                             **ctx_kw)
            oc = _flash_call(bc, qc, kc, vc, vtc, heads_per_step=DIFF_HEADS, name="flash_diff_ctx",
                             **ctx_kw, **diff_kw)
            y = _merge_call(oa, ob, oc, gm, w_brs, l, tm=n_ctx, gm_tile0=ctx_tile, name="branch_merge_ctx")
            ctx = _out_call(y, w_o, l, ctx, mod[l], tm=n_ctx, ctx_stream=True, name="out_proj_ctx")

        lat_kw = dict(q_tile0=0, kv_tile0=0, chunks=lat_chunks)
        oa = _flash_call(ba, qa, ka, va, vta, tq=512, n_q_tiles=n_lat // 512, name="flash_gqa", **lat_kw)
        ob = _flash_call(bb, qb, kb, vb, vtb, tq=2048, n_q_tiles=n_lat // 2048, name="flash_mla", **lat_kw)
        oc = _flash_call(bc, qc, kc, vc, vtc, tq=1024, n_q_tiles=n_lat // 1024, name="flash_diff",
                         **lat_kw, **diff_kw)
        y = _merge_call(oa, ob, oc, gm, w_brs, l, tm=512, gm_tile0=0, name="branch_merge")
        x = _out_call(y, w_o, l, x, mod[l], tm=512, ctx_stream=False, name="out_proj")
    return x
```
